```python
import jax, jax.numpy as jnp
from jax import lax
import numpy as np

D_MODEL = 2048
BATCH = 2
SEQ = 4096
DEPTH = 2
DEC_BATCH = 8
DEC_SEQ = 16
PAST_LEN = 1024

CHUNK = 64
N_PREV_CHUNKS = 8
BAND_PAST = N_PREV_CHUNKS * CHUNK
BAND = (N_PREV_CHUNKS + 1) * CHUNK
POOL_WIDTH = D_MODEL // 2
POOL_WINDOWS = (2, 4, 8, 16)
N_POOL_GROUPS = len(POOL_WINDOWS)
POOL_GROUP = POOL_WIDTH // N_POOL_GROUPS
POOL_HIST = max(POOL_WINDOWS) - 1
HEAD_DIM = 128
ATTN_WIDTH = D_MODEL // 2
N_HEADS = ATTN_WIDTH // HEAD_DIM
MAX_REL = 128
D_FF = 4 * D_MODEL
EPS = 1e-6
ATTN_SCALE = HEAD_DIM ** -0.5
NEG_INF = -1e30
IN_SPLITS = (POOL_WIDTH, POOL_WIDTH + ATTN_WIDTH, POOL_WIDTH + 2 * ATTN_WIDTH, POOL_WIDTH + 3 * ATTN_WIDTH)
IN_WIDTH = POOL_WIDTH + 3 * ATTN_WIDTH + 2 * D_MODEL

kernel_name = "chunk_streaming_pool_band_attn_hybrid"


def rmsnorm(x, g):
    xf = x.astype(jnp.float32)
    y = xf * lax.rsqrt(jnp.mean(xf * xf, axis=-1, keepdims=True) + EPS)
    return (y * g.astype(jnp.float32)).astype(x.dtype)


def mixer_inputs(x, norm1_g, w_in, b_gate, q_norm_g, k_norm_g):
    n, t, _ = x.shape
    h = rmsnorm(x, norm1_g)
    z = h @ w_in
    p, q, k, v, g = jnp.split(z, IN_SPLITS, axis=-1)
    q = rmsnorm(q.reshape(n, t, N_HEADS, HEAD_DIM), q_norm_g)
    k = rmsnorm(k.reshape(n, t, N_HEADS, HEAD_DIM), k_norm_g)
    v = v.reshape(n, t, N_HEADS, HEAD_DIM)
    gates = jax.nn.sigmoid((g + b_gate).astype(jnp.float32)).astype(x.dtype)
    return p, q, k, v, gates


def pool_mixer(p, hist, pos0, pool_w, pool_scale):
    n, t, _ = p.shape
    xp = jnp.concatenate([hist.astype(p.dtype), p], axis=1).astype(jnp.float32)
    cs = jnp.pad(jnp.cumsum(xp, axis=1), ((0, 0), (1, 0), (0, 0)))
    pos = pos0 + jnp.arange(t)
    pf = p.astype(jnp.float32)
    outs = []
    for gi, w in enumerate(POOL_WINDOWS):
        sl = slice(gi * POOL_GROUP, (gi + 1) * POOL_GROUP)
        end = cs[:, POOL_HIST + 1:POOL_HIST + 1 + t, sl]
        start = cs[:, POOL_HIST + 1 - w:POOL_HIST + 1 - w + t, sl]
        cnt = jnp.minimum(pos + 1, w).astype(jnp.float32)[None, :, None]
        d = ((end - start) / cnt - pf[:, :, sl]).astype(p.dtype)
        outs.append(jnp.einsum('ntc,cd->ntd', d, pool_w[gi]))
    return jnp.concatenate(outs, axis=-1) * pool_scale


def band_attention_prompt(q, k, v, rel_bias):
    n, s, h, d = q.shape
    nc = s // CHUNK
    pad = ((0, 0), (BAND_PAST, 0), (0, 0), (0, 0))
    kp = jnp.pad(k, pad).reshape(n, nc + N_PREV_CHUNKS, CHUNK, h, d)
    vp = jnp.pad(v, pad).reshape(n, nc + N_PREV_CHUNKS, CHUNK, h, d)
    kb = jnp.concatenate([kp[:, j:j + nc] for j in range(N_PREV_CHUNKS + 1)], axis=2)
    vb = jnp.concatenate([vp[:, j:j + nc] for j in range(N_PREV_CHUNKS + 1)], axis=2)
    qc = q.reshape(n, nc, CHUNK, h, d)
    scores = jnp.einsum('ncqhd,nckhd->nchqk', qc, kb,
                        preferred_element_type=jnp.float32) * ATTN_SCALE
    qi = jnp.arange(CHUNK)
    kj = jnp.arange(BAND)
    rel = qi[:, None] - (kj[None, :] - BAND_PAST)
    bias = rel_bias[:, jnp.clip(rel, -MAX_REL, MAX_REL) + MAX_REL].astype(jnp.float32)
    kpos = jnp.arange(nc)[:, None] * CHUNK - BAND_PAST + kj[None, :]
    scores = jnp.where((kpos >= 0)[None, :, None, None, :], scores + bias[None, None], NEG_INF)
    probs = jax.nn.softmax(scores, axis=-1).astype(v.dtype)
    out = jnp.einsum('nchqk,nckhd->ncqhd', probs, vb)
    return out.reshape(n, s, h * d)


def band_attention_sample(q, k, v, ck, cv, rel_bias):
    n, t, h, d = q.shape
    L = ck.shape[1]
    kk = jnp.concatenate([ck.astype(k.dtype), k], axis=1)
    vv = jnp.concatenate([cv.astype(v.dtype), v], axis=1)
    scores = jnp.einsum('nqhd,nkhd->nhqk', q, kk,
                        preferred_element_type=jnp.float32) * ATTN_SCALE
    qpos = PAST_LEN + jnp.arange(t)
    kpos = PAST_LEN - L + jnp.arange(L + t)
    rel = qpos[:, None] - kpos[None, :]
    bias = rel_bias[:, jnp.clip(rel, -MAX_REL, MAX_REL) + MAX_REL].astype(jnp.float32)
    probs = jax.nn.softmax(scores + bias[None], axis=-1).astype(v.dtype)
    out = jnp.einsum('nhqk,nkhd->nqhd', probs, vv).reshape(n, t, h * d)
    return out, kk[:, t:], vv[:, t:]


def merge_and_ffn(x, pool_out, attn_out, gates, w_branch_a, w_branch_b, w_out,
                  norm2_g, w_up, w_down):
    ga, gb = jnp.split(gates, 2, axis=-1)
    merged = ga * (pool_out @ w_branch_a) + gb * (attn_out @ w_branch_b)
    x = x + merged @ w_out
    u = jnp.square(jax.nn.relu(rmsnorm(x, norm2_g) @ w_up))
    return x + u @ w_down


def setup_inputs(seed: int = 0) -> dict:
    key = jax.random.key(seed)
    ks = jax.random.split(key, 20)
    f32 = jnp.float32
    L = min(BAND_PAST, PAST_LEN)
    nrm = lambda k, shape, s: jax.random.normal(k, shape, f32) * s
    return {
        "x_prompt": nrm(ks[0], (BATCH, SEQ, D_MODEL), 1.0),
        "x_sample": nrm(ks[1], (DEC_BATCH, DEC_SEQ, D_MODEL), 1.0),
        "state_pool": nrm(ks[2], (DEPTH, DEC_BATCH, POOL_HIST, POOL_WIDTH), 1.0),
        "cache_k": nrm(ks[3], (DEPTH, DEC_BATCH, L, N_HEADS, HEAD_DIM), 1.0),
        "cache_v": nrm(ks[4], (DEPTH, DEC_BATCH, L, N_HEADS, HEAD_DIM), 1.0),
        "norm1_g": 1.0 + nrm(ks[5], (DEPTH, D_MODEL), 0.05),
        "w_in": nrm(ks[6], (DEPTH, D_MODEL, IN_WIDTH), D_MODEL ** -0.5),
        "b_gate": nrm(ks[7], (DEPTH, 2 * D_MODEL), 0.01),
        "pool_w": nrm(ks[8], (DEPTH, N_POOL_GROUPS, POOL_GROUP, POOL_GROUP), POOL_GROUP ** -0.5),
        "pool_scale": 1.0 + nrm(ks[9], (DEPTH, POOL_WIDTH), 0.05),
        "q_norm_g": 1.0 + nrm(ks[10], (DEPTH, HEAD_DIM), 0.05),
        "k_norm_g": 1.0 + nrm(ks[11], (DEPTH, HEAD_DIM), 0.05),
        "rel_bias": nrm(ks[12], (DEPTH, N_HEADS, 2 * MAX_REL + 1), 0.1),
        "w_branch_a": nrm(ks[13], (DEPTH, POOL_WIDTH, D_MODEL), POOL_WIDTH ** -0.5),
        "w_branch_b": nrm(ks[14], (DEPTH, ATTN_WIDTH, D_MODEL), ATTN_WIDTH ** -0.5),
        "w_out": nrm(ks[15], (DEPTH, D_MODEL, D_MODEL), D_MODEL ** -0.5),
        "norm2_g": 1.0 + nrm(ks[16], (DEPTH, D_MODEL), 0.05),
        "w_up": nrm(ks[17], (DEPTH, D_MODEL, D_FF), D_MODEL ** -0.5),
        "w_down": nrm(ks[18], (DEPTH, D_FF, D_MODEL), 0.5 * D_FF ** -0.5),
    }


def reference(x_prompt, x_sample, state_pool, cache_k, cache_v, norm1_g, w_in, b_gate,
              pool_w, pool_scale, q_norm_g, k_norm_g, rel_bias, w_branch_a, w_branch_b,
              w_out, norm2_g, w_up, w_down):
    xp, xs = x_prompt, x_sample
    s = xp.shape[1]
    keep_p = max(s - BAND_PAST, 0)
    pool_p, kp_l, vp_l, pool_s, ks_l, vs_l = [], [], [], [], [], []
    for l in range(DEPTH):
        p, q, k, v, gates = mixer_inputs(xp, norm1_g[l], w_in[l], b_gate[l], q_norm_g[l], k_norm_g[l])
        hist0 = jnp.zeros((xp.shape[0], POOL_HIST, POOL_WIDTH), p.dtype)
        a_out = pool_mixer(p, hist0, 0, pool_w[l], pool_scale[l])
        b_out = band_attention_prompt(q, k, v, rel_bias[l])
        pool_p.append(p[:, -POOL_HIST:])
        kp_l.append(k[:, keep_p:])
        vp_l.append(v[:, keep_p:])
        xp = merge_and_ffn(xp, a_out, b_out, gates, w_branch_a[l], w_branch_b[l], w_out[l],
                           norm2_g[l], w_up[l], w_down[l])
        p, q, k, v, gates = mixer_inputs(xs, norm1_g[l], w_in[l], b_gate[l], q_norm_g[l], k_norm_g[l])
        hist = state_pool[l].astype(p.dtype)
        a_out = pool_mixer(p, hist, PAST_LEN, pool_w[l], pool_scale[l])
        b_out, k_new, v_new = band_attention_sample(q, k, v, cache_k[l], cache_v[l], rel_bias[l])
        pool_s.append(jnp.concatenate([hist, p], axis=1)[:, -POOL_HIST:])
        ks_l.append(k_new)
        vs_l.append(v_new)
        xs = merge_and_ffn(xs, a_out, b_out, gates, w_branch_a[l], w_branch_b[l], w_out[l],
                           norm2_g[l], w_up[l], w_down[l])
    pool_prompt = jnp.stack(pool_p)
    k_prompt = jnp.stack(kp_l)
    v_prompt = jnp.stack(vp_l)
    pool_sample = jnp.stack(pool_s)
    k_sample = jnp.stack(ks_l)
    v_sample = jnp.stack(vs_l)
    return (xp, xs, pool_prompt, k_prompt, v_prompt, pool_sample, k_sample, v_sample)
```

```python
import functools

import jax
import jax.numpy as jnp
from jax import lax
from jax.experimental import pallas as pl
from jax.experimental.pallas import tpu as pltpu

D_MODEL = 2048
PAST_LEN = 1024
CHUNK = 64
N_PREV_CHUNKS = 8
BAND_PAST = N_PREV_CHUNKS * CHUNK
POOL_WIDTH = D_MODEL // 2
POOL_WINDOWS = (2, 4, 8, 16)
POOL_GROUP = POOL_WIDTH // len(POOL_WINDOWS)
POOL_HIST = max(POOL_WINDOWS) - 1
HEAD_DIM = 128
ATTN_WIDTH = D_MODEL // 2
N_HEADS = ATTN_WIDTH // HEAD_DIM
MAX_REL = 128
D_FF = 4 * D_MODEL
EPS = 1e-6
ATTN_SCALE = HEAD_DIM ** -0.5
NEG_INF = -1e30
IN_WIDTH = POOL_WIDTH + 3 * ATTN_WIDTH + 2 * D_MODEL

BF16 = jnp.bfloat16
F32 = jnp.float32

VMEM_LIMIT_BYTES = 56 * 1024 * 1024

HIST_ROWS = 16
Q_BLOCK = 4 * CHUNK
KEY_BLOCKS = 3
SAMPLE_KEYS_PAD = 640


def _rms(xf, g):
    ms = jnp.mean(xf * xf, axis=-1, keepdims=True)
    return xf * lax.rsqrt(ms + EPS) * g


def _dot(a, b):
    return jnp.dot(a, b, preferred_element_type=F32)


def _dot_t(a, b):
    return lax.dot_general(a, b, (((1,), (1,)), ((), ())), preferred_element_type=F32)


def _params(semantics):
    return pltpu.CompilerParams(dimension_semantics=semantics, vmem_limit_bytes=VMEM_LIMIT_BYTES)


def _const_spec(shape):
    nd = len(shape)
    return pl.BlockSpec(shape, lambda *_: (0,) * nd, pipeline_mode=pl.Buffered(1))


IN_TN = 512
_SEG = POOL_WIDTH // IN_TN
_GATE0 = 4 * _SEG


def _in_kernel(x_ref, g1_ref, w_ref, bg_ref, qg_ref, kg_ref,
               p_ref, q_ref, k_ref, v_ref, kf_ref, vf_ref, gate_ref, h_scr):
    j = pl.program_id(1)

    @pl.when(j == 0)
    def _():
        h_scr[...] = _rms(x_ref[...], g1_ref[...]).astype(BF16)

    z = _dot(h_scr[...], w_ref[...])

    def head_norm(g_ref):
        return [_rms(z[:, h * HEAD_DIM:(h + 1) * HEAD_DIM], g_ref[...])
                for h in range(IN_TN // HEAD_DIM)]

    @pl.when(j < _SEG)
    def _():
        p_ref[...] = z

    @pl.when((j >= _SEG) & (j < 2 * _SEG))
    def _():
        for h, qn in enumerate(head_norm(qg_ref)):
            q_ref[:, h * HEAD_DIM:(h + 1) * HEAD_DIM] = qn.astype(BF16)

    @pl.when((j >= 2 * _SEG) & (j < 3 * _SEG))
    def _():
        for h, kn in enumerate(head_norm(kg_ref)):
            kf_ref[:, h * HEAD_DIM:(h + 1) * HEAD_DIM] = kn
            k_ref[:, h * HEAD_DIM:(h + 1) * HEAD_DIM] = kn.astype(BF16)

    @pl.when((j >= 3 * _SEG) & (j < _GATE0))
    def _():
        vf_ref[...] = z
        v_ref[...] = z.astype(BF16)

    @pl.when(j >= _GATE0)
    def _():
        gate_ref[...] = jax.nn.sigmoid(z + bg_ref[...])


def _in_proj(x, g1, w_in, b_gate, qg, kg, *, tm):
    m = x.shape[0]
    nj = IN_WIDTH // IN_TN

    def seg_spec(seg):
        return pl.BlockSpec((tm, IN_TN), lambda i, j: (i, jnp.clip(j - seg * _SEG, 0, _SEG - 1)))

    gate_blocks = 2 * D_MODEL // IN_TN
    out_shape = (
        jax.ShapeDtypeStruct((m, POOL_WIDTH), F32),
        jax.ShapeDtypeStruct((m, ATTN_WIDTH), BF16),
        jax.ShapeDtypeStruct((m, ATTN_WIDTH), BF16),
        jax.ShapeDtypeStruct((m, ATTN_WIDTH), BF16),
        jax.ShapeDtypeStruct((m, ATTN_WIDTH), F32),
        jax.ShapeDtypeStruct((m, ATTN_WIDTH), F32),
        jax.ShapeDtypeStruct((m, 2 * D_MODEL), F32),
    )
    return pl.pallas_call(
        _in_kernel,
        grid=(m // tm, nj),
        in_specs=[
            pl.BlockSpec((tm, D_MODEL), lambda i, j: (i, 0)),
            pl.BlockSpec((1, D_MODEL), lambda i, j: (0, 0)),
            pl.BlockSpec((D_MODEL, IN_TN), lambda i, j: (0, j)),
            pl.BlockSpec((1, IN_TN), lambda i, j: (0, jnp.clip(j - _GATE0, 0, gate_blocks - 1))),
            pl.BlockSpec((1, HEAD_DIM), lambda i, j: (0, 0)),
            pl.BlockSpec((1, HEAD_DIM), lambda i, j: (0, 0)),
        ],
        out_specs=(
            seg_spec(0), seg_spec(1), seg_spec(2), seg_spec(3), seg_spec(2), seg_spec(3),
            pl.BlockSpec((tm, IN_TN), lambda i, j: (i, jnp.clip(j - _GATE0, 0, gate_blocks - 1))),
        ),
        out_shape=out_shape,
        scratch_shapes=[pltpu.VMEM((tm, D_MODEL), BF16)],
        compiler_params=_params(("parallel", "arbitrary")),
        name="in_proj",
    )(x, g1, w_in, b_gate, qg, kg)


def _attn_prompt_kernel(q_ref, k0_ref, k1_ref, k2_ref, v0_ref, v1_ref, v2_ref, bias_ref, o_ref):
    i = pl.program_id(1)
    k_refs = (k0_ref, k1_ref, k2_ref)
    v_refs = (v0_ref, v1_ref, v2_ref)
    for h in range(N_HEADS):
        sl = slice(h * HEAD_DIM, (h + 1) * HEAD_DIM)
        qh = q_ref[:, sl]
        s = []
        for d in range(KEY_BLOCKS):
            sd = _dot_t(qh, k_refs[d][:, sl]) * ATTN_SCALE + bias_ref[h, :, d * Q_BLOCK:(d + 1) * Q_BLOCK]
            if d < KEY_BLOCKS - 1:
                sd = jnp.where(i + d >= KEY_BLOCKS - 1, sd, NEG_INF)
            s.append(sd)
        m = jnp.max(jnp.maximum(jnp.maximum(s[0], s[1]), s[2]), axis=-1, keepdims=True)
        e = [jnp.exp(sd - m) for sd in s]
        l = jnp.sum(e[0] + e[1] + e[2], axis=-1, keepdims=True)
        o = _dot(e[0].astype(BF16), v_refs[0][:, sl])
        for d in range(1, KEY_BLOCKS):
            o = o + _dot(e[d].astype(BF16), v_refs[d][:, sl])
        o_ref[:, sl] = (o / l).astype(BF16)


def _attn_prompt(q, k, v, bias, *, n, s):
    nb = s // Q_BLOCK
    q_spec = pl.BlockSpec((Q_BLOCK, ATTN_WIDTH), lambda b, i: (b * nb + i, 0))

    def kv_spec(d):
        return pl.BlockSpec((Q_BLOCK, ATTN_WIDTH),
                            lambda b, i: (b * nb + jnp.maximum(i - (KEY_BLOCKS - 1) + d, 0), 0))

    return pl.pallas_call(
        _attn_prompt_kernel,
        grid=(n, nb),
        in_specs=[q_spec] + [kv_spec(d) for d in range(KEY_BLOCKS)] * 2
                 + [_const_spec((N_HEADS, Q_BLOCK, KEY_BLOCKS * Q_BLOCK))],
        out_specs=q_spec,
        out_shape=jax.ShapeDtypeStruct((n * s, ATTN_WIDTH), BF16),
        compiler_params=_params(("parallel", "arbitrary")),
        name="attn_prompt",
    )(q, k, k, k, v, v, v, bias)


def _prompt_bias(rel_bias):
    qi = jnp.arange(Q_BLOCK)[:, None]
    kj = jnp.arange(KEY_BLOCKS * Q_BLOCK)[None, :]
    rel = (KEY_BLOCKS - 1) * Q_BLOCK + qi - kj
    qc = qi // CHUNK + (KEY_BLOCKS - 1) * (Q_BLOCK // CHUNK)
    kc = kj // CHUNK
    visible = (kc <= qc) & (kc >= qc - N_PREV_CHUNKS)
    bias = rel_bias[:, jnp.clip(rel, -MAX_REL, MAX_REL) + MAX_REL].astype(F32)
    return jnp.where(visible[None], bias, NEG_INF)


def _attn_sample_kernel(q_ref, kf_ref, vf_ref, ck_ref, cv_ref, bias_ref,
                        o_ref, ko_ref, vo_ref, kk_scr, vv_scr, *, t, past):
    for new_ref, cache_ref, out_ref, scr in ((kf_ref, ck_ref, ko_ref, kk_scr),
                                             (vf_ref, cv_ref, vo_ref, vv_scr)):
        out_ref[0, 0:past - t, :] = cache_ref[0, t:past, :]
        out_ref[0, past - t:past, :] = new_ref[...]
        scr[0:past, :] = cache_ref[0].astype(BF16)
        scr[past:past + t, :] = new_ref[...].astype(BF16)
        scr[past + t:, :] = jnp.zeros((SAMPLE_KEYS_PAD - past - t, ATTN_WIDTH), BF16)
    for h in range(N_HEADS):
        sl = slice(h * HEAD_DIM, (h + 1) * HEAD_DIM)
        s = _dot_t(q_ref[:, sl], kk_scr[:, sl]) * ATTN_SCALE + bias_ref[h]
        m = jnp.max(s, axis=-1, keepdims=True)
        e = jnp.exp(s - m)
        l = jnp.sum(e, axis=-1, keepdims=True)
        o = _dot(e.astype(BF16), vv_scr[:, sl])
        o_ref[:, sl] = (o / l).astype(BF16)


def _attn_sample(q, kf, vf, ck, cv, bias, *, n, t):
    past = ck.shape[1]
    row_spec = pl.BlockSpec((t, ATTN_WIDTH), lambda b: (b, 0))
    cache_spec = pl.BlockSpec((1, past, ATTN_WIDTH), lambda b: (b, 0, 0))
    return pl.pallas_call(
        functools.partial(_attn_sample_kernel, t=t, past=past),
        grid=(n,),
        in_specs=[row_spec, row_spec, row_spec, cache_spec, cache_spec,
                  _const_spec((N_HEADS, t, SAMPLE_KEYS_PAD))],
        out_specs=(row_spec, cache_spec, cache_spec),
        out_shape=(jax.ShapeDtypeStruct((n * t, ATTN_WIDTH), BF16),
                   jax.ShapeDtypeStruct((n, past, ATTN_WIDTH), F32),
                   jax.ShapeDtypeStruct((n, past, ATTN_WIDTH), F32)),
        scratch_shapes=[pltpu.VMEM((SAMPLE_KEYS_PAD, ATTN_WIDTH), BF16),
                        pltpu.VMEM((SAMPLE_KEYS_PAD, ATTN_WIDTH), BF16)],
        compiler_params=_params(("arbitrary",)),
        name="attn_sample",
    )(q, kf, vf, ck, cv, bias)


def _sample_bias(rel_bias, *, t, past):
    qi = jnp.arange(t)[:, None]
    kj = jnp.arange(SAMPLE_KEYS_PAD)[None, :]
    rel = past + qi - kj
    bias = rel_bias[:, jnp.clip(rel, -MAX_REL, MAX_REL) + MAX_REL].astype(F32)
    return jnp.where((kj < past + t)[None], bias, NEG_INF)


def _merge_kernel(x_ref, p_ref, hist_ref, a_ref, gate_ref, pw_ref, ps_ref, wa_ref, wb_ref, wo_ref,
                  o_ref, xp_scr, d_scr, *, nseq, t, blocks_per_seq, pos0):
    i = pl.program_id(0)
    row = lax.broadcasted_iota(jnp.int32, (t, 1), 0)
    if blocks_per_seq is None:
        pos = pos0 + row
    else:
        pos = (i % blocks_per_seq) * t + row
    for s in range(nseq):
        rows = slice(s * t, (s + 1) * t)
        hist = hist_ref[s * HIST_ROWS:(s + 1) * HIST_ROWS, :]
        if blocks_per_seq is not None:
            hist = jnp.where(i % blocks_per_seq == 0, 0.0, hist)
        xp_scr[0:HIST_ROWS, :] = hist
        xp_scr[HIST_ROWS:, :] = p_ref[rows, :]
        for gi, w in enumerate(POOL_WINDOWS):
            cols = slice(gi * POOL_GROUP, (gi + 1) * POOL_GROUP)
            acc = xp_scr[HIST_ROWS:HIST_ROWS + t, cols]
            for sh in range(1, w):
                acc = acc + xp_scr[HIST_ROWS - sh:HIST_ROWS - sh + t, cols]
            cnt = jnp.minimum(pos + 1, w).astype(F32)
            d_scr[rows, cols] = (acc / cnt - p_ref[rows, cols]).astype(BF16)
    pooled = jnp.concatenate(
        [_dot(d_scr[:, gi * POOL_GROUP:(gi + 1) * POOL_GROUP], pw_ref[gi]) for gi in range(len(POOL_WINDOWS))],
        axis=-1) * ps_ref[...]
    branch_a = _dot(pooled.astype(BF16), wa_ref[...])
    branch_b = _dot(a_ref[...], wb_ref[...])
    merged = gate_ref[:, :D_MODEL] * branch_a + gate_ref[:, D_MODEL:] * branch_b
    o_ref[...] = x_ref[...] + _dot(merged.astype(BF16), wo_ref[...])


def _merge(x, p, hist, attn, gates, pool_w, pool_scale, wa, wb, wo, *, tm, nseq, t, blocks_per_seq, pos0,
           hist_spec):
    m = x.shape[0]

    def row_spec(width):
        return pl.BlockSpec((tm, width), lambda i: (i, 0))

    return pl.pallas_call(
        functools.partial(_merge_kernel, nseq=nseq, t=t, blocks_per_seq=blocks_per_seq, pos0=pos0),
        grid=(m // tm,),
        in_specs=[row_spec(D_MODEL), row_spec(POOL_WIDTH), hist_spec, row_spec(ATTN_WIDTH),
                  row_spec(2 * D_MODEL),
                  _const_spec(pool_w.shape), _const_spec(pool_scale.shape),
                  _const_spec(wa.shape), _const_spec(wb.shape), _const_spec(wo.shape)],
        out_specs=row_spec(D_MODEL),
        out_shape=jax.ShapeDtypeStruct((m, D_MODEL), F32),
        scratch_shapes=[pltpu.VMEM((HIST_ROWS + t, POOL_WIDTH), F32),
                        pltpu.VMEM((tm, POOL_WIDTH), BF16)],
        compiler_params=_params(("arbitrary",)),
        name="merge",
    )(x, p, hist, attn, gates, pool_w, pool_scale, wa, wb, wo)


FFN_TF = 512


def _ffn_kernel(x_ref, g2_ref, wu_ref, wd_ref, o_ref, h_scr):
    j = pl.program_id(1)

    @pl.when(j == 0)
    def _():
        x = x_ref[...]
        h_scr[...] = _rms(x, g2_ref[...]).astype(BF16)
        o_ref[...] = x

    u = jnp.square(jnp.maximum(_dot(h_scr[...], wu_ref[...]), 0.0))
    o_ref[...] += _dot(u.astype(BF16), wd_ref[...])


def _ffn(x, g2, wu, wd, *, tm):
    m = x.shape[0]
    return pl.pallas_call(
        _ffn_kernel,
        grid=(m // tm, D_FF // FFN_TF),
        in_specs=[pl.BlockSpec((tm, D_MODEL), lambda i, j: (i, 0)),
                  pl.BlockSpec((1, D_MODEL), lambda i, j: (0, 0)),
                  pl.BlockSpec((D_MODEL, FFN_TF), lambda i, j: (0, j)),
                  pl.BlockSpec((FFN_TF, D_MODEL), lambda i, j: (j, 0))],
        out_specs=pl.BlockSpec((tm, D_MODEL), lambda i, j: (i, 0)),
        out_shape=jax.ShapeDtypeStruct((m, D_MODEL), F32),
        scratch_shapes=[pltpu.VMEM((tm, D_MODEL), BF16)],
        compiler_params=_params(("parallel", "arbitrary")),
        name="ffn",
    )(x, g2, wu, wd)


PROMPT_TM = 1024
MERGE_TM = 256


def kernel(x_prompt, x_sample, state_pool, cache_k, cache_v, norm1_g, w_in, b_gate, pool_w, pool_scale,
           q_norm_g, k_norm_g, rel_bias, w_branch_a, w_branch_b, w_out, norm2_g, w_up, w_down):
    nb, seq, _ = x_prompt.shape
    ns, t, _ = x_sample.shape
    depth = w_in.shape[0]
    past = cache_k.shape[2]
    assert seq % PROMPT_TM == 0 and seq % MERGE_TM == 0 and seq % Q_BLOCK == 0 and seq >= BAND_PAST
    assert t >= POOL_HIST and t % HIST_ROWS == 0 and past + t <= SAMPLE_KEYS_PAD and past <= BAND_PAST

    xp = x_prompt.reshape(nb * seq, D_MODEL)
    xs = x_sample.reshape(ns * t, D_MODEL)
    blocks_per_seq = seq // MERGE_TM
    prompt_hist_spec = pl.BlockSpec(
        (HIST_ROWS, POOL_WIDTH), lambda i: (jnp.maximum(i * (MERGE_TM // HIST_ROWS) - 1, 0), 0))
    sample_hist_spec = pl.BlockSpec((ns * HIST_ROWS, POOL_WIDTH), lambda i: (0, 0))

    pool_p, kp_l, vp_l, pool_s, ks_l, vs_l = [], [], [], [], [], []
    for l in range(depth):
        g1 = norm1_g[l].reshape(1, D_MODEL)
        g2 = norm2_g[l].reshape(1, D_MODEL)
        bg = b_gate[l].reshape(1, 2 * D_MODEL)
        qg = q_norm_g[l].reshape(1, HEAD_DIM)
        kg = k_norm_g[l].reshape(1, HEAD_DIM)
        ps = pool_scale[l].reshape(1, POOL_WIDTH)
        w_in_l = w_in[l].astype(BF16)
        pw = pool_w[l].astype(BF16)
        wa = w_branch_a[l].astype(BF16)
        wb = w_branch_b[l].astype(BF16)
        wo = w_out[l].astype(BF16)
        wu = w_up[l].astype(BF16)
        wd = w_down[l].astype(BF16)

        p, q, k, v, kf, vf, gates = _in_proj(xp, g1, w_in_l, bg, qg, kg, tm=PROMPT_TM)
        attn = _attn_prompt(q, k, v, _prompt_bias(rel_bias[l]), n=nb, s=seq)
        pool_p.append(p.reshape(nb, seq, POOL_WIDTH)[:, seq - POOL_HIST:])
        kp_l.append(kf.reshape(nb, seq, N_HEADS, HEAD_DIM)[:, seq - BAND_PAST:])
        vp_l.append(vf.reshape(nb, seq, N_HEADS, HEAD_DIM)[:, seq - BAND_PAST:])
        xp = _merge(xp, p, p, attn, gates, pw, ps, wa, wb, wo, tm=MERGE_TM, nseq=1, t=MERGE_TM,
                    blocks_per_seq=blocks_per_seq, pos0=None, hist_spec=prompt_hist_spec)
        xp = _ffn(xp, g2, wu, wd, tm=PROMPT_TM)

        p, q, k, v, kf, vf, gates = _in_proj(xs, g1, w_in_l, bg, qg, kg, tm=ns * t)
        attn, k_new, v_new = _attn_sample(
            q, kf, vf, cache_k[l].reshape(ns, past, ATTN_WIDTH), cache_v[l].reshape(ns, past, ATTN_WIDTH),
            _sample_bias(rel_bias[l], t=t, past=past), n=ns, t=t)
        hist = jnp.pad(state_pool[l], ((0, 0), (HIST_ROWS - POOL_HIST, 0), (0, 0)))
        hist = hist.reshape(ns * HIST_ROWS, POOL_WIDTH)
        pool_s.append(p.reshape(ns, t, POOL_WIDTH)[:, t - POOL_HIST:])
        ks_l.append(k_new.reshape(ns, past, N_HEADS, HEAD_DIM))
        vs_l.append(v_new.reshape(ns, past, N_HEADS, HEAD_DIM))
        xs = _merge(xs, p, hist, attn, gates, pw, ps, wa, wb, wo, tm=ns * t, nseq=ns, t=t,
                    blocks_per_seq=None, pos0=PAST_LEN, hist_spec=sample_hist_spec)
        xs = _ffn(xs, g2, wu, wd, tm=ns * t)

    return (xp.reshape(nb, seq, D_MODEL), xs.reshape(ns, t, D_MODEL),
            jnp.stack(pool_p), jnp.stack(kp_l), jnp.stack(vp_l),
            jnp.stack(pool_s), jnp.stack(ks_l), jnp.stack(vs_l))
```

```python
import functools

import numpy as np

import jax
import jax.numpy as jnp
from jax import lax
from jax.experimental import pallas as pl
from jax.experimental.pallas import tpu as pltpu

D_MODEL = 2048
PAST_LEN = 1024
CHUNK = 64
N_PREV_CHUNKS = 8
BAND_PAST = N_PREV_CHUNKS * CHUNK
POOL_WIDTH = D_MODEL // 2
POOL_WINDOWS = (2, 4, 8, 16)
POOL_GROUP = POOL_WIDTH // len(POOL_WINDOWS)
POOL_HIST = max(POOL_WINDOWS) - 1
HEAD_DIM = 128
ATTN_WIDTH = D_MODEL // 2
N_HEADS = ATTN_WIDTH // HEAD_DIM
MAX_REL = 128
D_FF = 4 * D_MODEL
EPS = 1e-6
ATTN_SCALE = HEAD_DIM ** -0.5
NEG_INF = -1e30
IN_WIDTH = POOL_WIDTH + 3 * ATTN_WIDTH + 2 * D_MODEL

BF16 = jnp.bfloat16
F32 = jnp.float32

VMEM_LIMIT_BYTES = 56 * 1024 * 1024
LANES = 128

HIST_ROWS = 16
Q_BLOCK = 4 * CHUNK
KEY_BLOCKS = 3
SAMPLE_KEYS_PAD = 640


def _rms(xf, g):
    ms = jnp.mean(xf * xf, axis=-1, keepdims=True)
    return xf * lax.rsqrt(ms + EPS) * g


def _dot(a, b):
    return jnp.dot(a, b, preferred_element_type=F32)


def _dot_t(a, b):
    return lax.dot_general(a, b, (((1,), (1,)), ((), ())), preferred_element_type=F32)


def _params(semantics):
    return pltpu.CompilerParams(dimension_semantics=semantics, vmem_limit_bytes=VMEM_LIMIT_BYTES)


def _layer_spec(shape, layer):
    nd = len(shape)
    return pl.BlockSpec((None,) + tuple(shape), lambda *_: (layer,) + (0,) * nd,
                        pipeline_mode=pl.Buffered(1))


def _bias_table_kernel(ext_ref, o_ref, *, q_rows, keys, visible_fn):
    width = ext_ref.shape[-1]
    qi = lax.broadcasted_iota(jnp.int32, (q_rows, keys), 0)
    kj = lax.broadcasted_iota(jnp.int32, (q_rows, keys), 1)
    visible = visible_fn(qi, kj)
    for g in range(ext_ref.shape[0]):
        row = jnp.broadcast_to(ext_ref[g:g + 1, :], (q_rows, width))
        skew = pltpu.roll(row, 0, 1, stride=1, stride_axis=0)
        o_ref[g] = jnp.where(visible, skew[:, :keys], NEG_INF)


def _bias_table(rel_bias, *, q_rows, keys, offset, visible_fn):
    groups = rel_bias.shape[0] * rel_bias.shape[1]
    width = -(-(keys + q_rows - 1) // LANES) * LANES
    idx = np.zeros((width,), np.int32)
    idx[:keys] = np.clip(offset - np.arange(keys), -MAX_REL, MAX_REL) + MAX_REL
    back = np.arange(1, q_rows)
    idx[width - back] = np.clip(offset + back, -MAX_REL, MAX_REL) + MAX_REL
    ext = rel_bias.reshape(groups, 2 * MAX_REL + 1)[:, idx].astype(F32)
    return pl.pallas_call(
        functools.partial(_bias_table_kernel, q_rows=q_rows, keys=keys, visible_fn=visible_fn),
        out_shape=jax.ShapeDtypeStruct((groups, q_rows, keys), F32),
        compiler_params=pltpu.CompilerParams(vmem_limit_bytes=VMEM_LIMIT_BYTES),
        name="bias_table",
    )(ext)


def _prompt_visible(qi, kj):
    own = qi // CHUNK + (KEY_BLOCKS - 1) * (Q_BLOCK // CHUNK)
    kc = kj // CHUNK
    return (kc <= own) & (kc >= own - N_PREV_CHUNKS)


IN_TN = 512
_SEG = POOL_WIDTH // IN_TN
_GATE0 = 4 * _SEG


def _in_kernel(x_ref, g1_ref, w_ref, bg_ref, qg_ref, kg_ref,
               p_ref, q_ref, k_ref, v_ref, kf_ref, vf_ref, gate_ref, h_scr):
    j = pl.program_id(1)

    @pl.when(j == 0)
    def _():
        h_scr[...] = _rms(x_ref[...], g1_ref[...]).astype(BF16)

    z = _dot(h_scr[...], w_ref[...])

    def head_norm(g_ref):
        return [_rms(z[:, h * HEAD_DIM:(h + 1) * HEAD_DIM], g_ref[...])
                for h in range(IN_TN // HEAD_DIM)]

    @pl.when(j < _SEG)
    def _():
        p_ref[...] = z

    @pl.when((j >= _SEG) & (j < 2 * _SEG))
    def _():
        for h, qn in enumerate(head_norm(qg_ref)):
            q_ref[:, h * HEAD_DIM:(h + 1) * HEAD_DIM] = qn.astype(BF16)

    @pl.when((j >= 2 * _SEG) & (j < 3 * _SEG))
    def _():
        for h, kn in enumerate(head_norm(kg_ref)):
            kf_ref[:, h * HEAD_DIM:(h + 1) * HEAD_DIM] = kn
            k_ref[:, h * HEAD_DIM:(h + 1) * HEAD_DIM] = kn.astype(BF16)

    @pl.when((j >= 3 * _SEG) & (j < _GATE0))
    def _():
        vf_ref[...] = z
        v_ref[...] = z.astype(BF16)

    @pl.when(j >= _GATE0)
    def _():
        gate_ref[...] = jax.nn.sigmoid(z + bg_ref[...])


def _in_proj(x, g1, w_in, b_gate, qg, kg, *, layer, tm):
    m = x.shape[0]
    nj = IN_WIDTH // IN_TN

    def seg_spec(seg):
        return pl.BlockSpec((tm, IN_TN), lambda i, j: (i, jnp.clip(j - seg * _SEG, 0, _SEG - 1)))

    gate_blocks = 2 * D_MODEL // IN_TN
    out_shape = (
        jax.ShapeDtypeStruct((m, POOL_WIDTH), F32),
        jax.ShapeDtypeStruct((m, ATTN_WIDTH), BF16),
        jax.ShapeDtypeStruct((m, ATTN_WIDTH), BF16),
        jax.ShapeDtypeStruct((m, ATTN_WIDTH), BF16),
        jax.ShapeDtypeStruct((m, ATTN_WIDTH), F32),
        jax.ShapeDtypeStruct((m, ATTN_WIDTH), F32),
        jax.ShapeDtypeStruct((m, 2 * D_MODEL), F32),
    )
    return pl.pallas_call(
        _in_kernel,
        grid=(m // tm, nj),
        in_specs=[
            pl.BlockSpec((tm, D_MODEL), lambda i, j: (i, 0)),
            pl.BlockSpec((None, 1, D_MODEL), lambda i, j: (layer, 0, 0)),
            pl.BlockSpec((None, D_MODEL, IN_TN), lambda i, j: (layer, 0, j)),
            pl.BlockSpec((None, 1, IN_TN), lambda i, j: (layer, 0, jnp.clip(j - _GATE0, 0, gate_blocks - 1))),
            pl.BlockSpec((None, 1, HEAD_DIM), lambda i, j: (layer, 0, 0)),
            pl.BlockSpec((None, 1, HEAD_DIM), lambda i, j: (layer, 0, 0)),
        ],
        out_specs=(
            seg_spec(0), seg_spec(1), seg_spec(2), seg_spec(3), seg_spec(2), seg_spec(3),
            pl.BlockSpec((tm, IN_TN), lambda i, j: (i, jnp.clip(j - _GATE0, 0, gate_blocks - 1))),
        ),
        out_shape=out_shape,
        scratch_shapes=[pltpu.VMEM((tm, D_MODEL), BF16)],
        compiler_params=_params(("parallel", "arbitrary")),
        name="in_proj",
    )(x, g1, w_in, b_gate, qg, kg)


def _attn_prompt_kernel(q_ref, k0_ref, k1_ref, k2_ref, v0_ref, v1_ref, v2_ref, bias_ref, o_ref):
    i = pl.program_id(1)
    k_refs = (k0_ref, k1_ref, k2_ref)
    v_refs = (v0_ref, v1_ref, v2_ref)
    for h in range(N_HEADS):
        sl = slice(h * HEAD_DIM, (h + 1) * HEAD_DIM)
        qh = q_ref[:, sl]
        s = []
        for d in range(KEY_BLOCKS):
            sd = _dot_t(qh, k_refs[d][:, sl]) * ATTN_SCALE + bias_ref[h, :, d * Q_BLOCK:(d + 1) * Q_BLOCK]
            if d < KEY_BLOCKS - 1:
                sd = jnp.where(i + d >= KEY_BLOCKS - 1, sd, NEG_INF)
            s.append(sd)
        m = jnp.max(jnp.maximum(jnp.maximum(s[0], s[1]), s[2]), axis=-1, keepdims=True)
        e = [jnp.exp(sd - m) for sd in s]
        l = jnp.sum(e[0] + e[1] + e[2], axis=-1, keepdims=True)
        o = _dot(e[0].astype(BF16), v_refs[0][:, sl])
        for d in range(1, KEY_BLOCKS):
            o = o + _dot(e[d].astype(BF16), v_refs[d][:, sl])
        o_ref[:, sl] = (o / l).astype(BF16)


def _attn_prompt(q, k, v, bias, *, layer, n, s):
    nb = s // Q_BLOCK
    q_spec = pl.BlockSpec((Q_BLOCK, ATTN_WIDTH), lambda b, i: (b * nb + i, 0))

    def kv_spec(d):
        return pl.BlockSpec((Q_BLOCK, ATTN_WIDTH),
                            lambda b, i: (b * nb + jnp.maximum(i - (KEY_BLOCKS - 1) + d, 0), 0))

    bias_spec = pl.BlockSpec((N_HEADS, Q_BLOCK, KEY_BLOCKS * Q_BLOCK), lambda b, i: (layer, 0, 0),
                             pipeline_mode=pl.Buffered(1))
    return pl.pallas_call(
        _attn_prompt_kernel,
        grid=(n, nb),
        in_specs=[q_spec] + [kv_spec(d) for d in range(KEY_BLOCKS)] * 2 + [bias_spec],
        out_specs=q_spec,
        out_shape=jax.ShapeDtypeStruct((n * s, ATTN_WIDTH), BF16),
        compiler_params=_params(("parallel", "arbitrary")),
        name="attn_prompt",
    )(q, k, k, k, v, v, v, bias)


def _attn_sample_kernel(q_ref, kf_ref, vf_ref, ck_ref, cv_ref, bias_ref,
                        o_ref, ko_ref, vo_ref, kk_scr, vv_scr, *, t, past):
    for new_ref, cache_ref, out_ref, scr in ((kf_ref, ck_ref, ko_ref, kk_scr),
                                             (vf_ref, cv_ref, vo_ref, vv_scr)):
        out_ref[0:past - t] = cache_ref[t:past]
        for h in range(N_HEADS):
            sl = slice(h * HEAD_DIM, (h + 1) * HEAD_DIM)
            out_ref[past - t:past, h, :] = new_ref[:, sl]
            scr[0:past, sl] = cache_ref[:, h, :].astype(BF16)
        scr[past:past + t, :] = new_ref[...].astype(BF16)
        scr[past + t:, :] = jnp.zeros((SAMPLE_KEYS_PAD - past - t, ATTN_WIDTH), BF16)
    for h in range(N_HEADS):
        sl = slice(h * HEAD_DIM, (h + 1) * HEAD_DIM)
        s = _dot_t(q_ref[:, sl], kk_scr[:, sl]) * ATTN_SCALE + bias_ref[h]
        m = jnp.max(s, axis=-1, keepdims=True)
        e = jnp.exp(s - m)
        l = jnp.sum(e, axis=-1, keepdims=True)
        o = _dot(e.astype(BF16), vv_scr[:, sl])
        o_ref[:, sl] = (o / l).astype(BF16)


def _attn_sample(q, kf, vf, cache_k, cache_v, bias, *, layer, n, t):
    past = cache_k.shape[2]
    row_spec = pl.BlockSpec((t, ATTN_WIDTH), lambda b: (b, 0))
    cache_spec = pl.BlockSpec((None, None, past, N_HEADS, HEAD_DIM), lambda b: (layer, b, 0, 0, 0))
    new_cache_spec = pl.BlockSpec((None, past, N_HEADS, HEAD_DIM), lambda b: (b, 0, 0, 0))
    bias_spec = pl.BlockSpec((N_HEADS, t, SAMPLE_KEYS_PAD), lambda b: (layer, 0, 0),
                             pipeline_mode=pl.Buffered(1))
    new_cache = jax.ShapeDtypeStruct((n, past, N_HEADS, HEAD_DIM), F32)
    return pl.pallas_call(
        functools.partial(_attn_sample_kernel, t=t, past=past),
        grid=(n,),
        in_specs=[row_spec, row_spec, row_spec, cache_spec, cache_spec, bias_spec],
        out_specs=(row_spec, new_cache_spec, new_cache_spec),
        out_shape=(jax.ShapeDtypeStruct((n * t, ATTN_WIDTH), BF16), new_cache, new_cache),
        scratch_shapes=[pltpu.VMEM((SAMPLE_KEYS_PAD, ATTN_WIDTH), BF16),
                        pltpu.VMEM((SAMPLE_KEYS_PAD, ATTN_WIDTH), BF16)],
        compiler_params=_params(("arbitrary",)),
        name="attn_sample",
    )(q, kf, vf, cache_k, cache_v, bias)


def _merge_kernel(x_ref, p_ref, hist_ref, a_ref, gate_ref, pw_ref, ps_ref, wa_ref, wb_ref, wo_ref,
                  o_ref, xp_scr, d_scr, *, nseq, t, blocks_per_seq, pos0):
    i = pl.program_id(0)
    row = lax.broadcasted_iota(jnp.int32, (t, 1), 0)
    if blocks_per_seq is None:
        pos = pos0 + row
    else:
        pos = (i % blocks_per_seq) * t + row
    for s in range(nseq):
        rows = slice(s * t, (s + 1) * t)
        hist = hist_ref[s * HIST_ROWS:(s + 1) * HIST_ROWS, :]
        if blocks_per_seq is not None:
            hist = jnp.where(i % blocks_per_seq == 0, 0.0, hist)
        xp_scr[0:HIST_ROWS, :] = hist
        xp_scr[HIST_ROWS:, :] = p_ref[rows, :]
        for gi, w in enumerate(POOL_WINDOWS):
            cols = slice(gi * POOL_GROUP, (gi + 1) * POOL_GROUP)
            acc = xp_scr[HIST_ROWS:HIST_ROWS + t, cols]
            for sh in range(1, w):
                acc = acc + xp_scr[HIST_ROWS - sh:HIST_ROWS - sh + t, cols]
            cnt = jnp.minimum(pos + 1, w).astype(F32)
            d_scr[rows, cols] = (acc / cnt - p_ref[rows, cols]).astype(BF16)
    pooled = jnp.concatenate(
        [_dot(d_scr[:, gi * POOL_GROUP:(gi + 1) * POOL_GROUP], pw_ref[gi]) for gi in range(len(POOL_WINDOWS))],
        axis=-1) * ps_ref[...]
    branch_a = _dot(pooled.astype(BF16), wa_ref[...])
    branch_b = _dot(a_ref[...], wb_ref[...])
    merged = gate_ref[:, :D_MODEL] * branch_a + gate_ref[:, D_MODEL:] * branch_b
    o_ref[...] = x_ref[...] + _dot(merged.astype(BF16), wo_ref[...])


def _merge(x, p, hist, attn, gates, pool_w, pool_scale, wa, wb, wo, *, layer, tm, nseq, t, blocks_per_seq, pos0,
           hist_spec):
    m = x.shape[0]

    def row_spec(width):
        return pl.BlockSpec((tm, width), lambda i: (i, 0))

    return pl.pallas_call(
        functools.partial(_merge_kernel, nseq=nseq, t=t, blocks_per_seq=blocks_per_seq, pos0=pos0),
        grid=(m // tm,),
        in_specs=[row_spec(D_MODEL), row_spec(POOL_WIDTH), hist_spec, row_spec(ATTN_WIDTH),
                  row_spec(2 * D_MODEL),
                  _layer_spec(pool_w.shape[1:], layer), _layer_spec(pool_scale.shape[1:], layer),
                  _layer_spec(wa.shape[1:], layer), _layer_spec(wb.shape[1:], layer),
                  _layer_spec(wo.shape[1:], layer)],
        out_specs=row_spec(D_MODEL),
        out_shape=jax.ShapeDtypeStruct((m, D_MODEL), F32),
        scratch_shapes=[pltpu.VMEM((HIST_ROWS + t, POOL_WIDTH), F32),
                        pltpu.VMEM((tm, POOL_WIDTH), BF16)],
        compiler_params=_params(("arbitrary",)),
        name="merge",
    )(x, p, hist, attn, gates, pool_w, pool_scale, wa, wb, wo)


FFN_TF = 512


def _ffn_kernel(x_ref, g2_ref, wu_ref, wd_ref, o_ref, h_scr):
    j = pl.program_id(1)

    @pl.when(j == 0)
    def _():
        x = x_ref[...]
        h_scr[...] = _rms(x, g2_ref[...]).astype(BF16)
        o_ref[...] = x

    u = jnp.square(jnp.maximum(_dot(h_scr[...], wu_ref[...]), 0.0))
    o_ref[...] += _dot(u.astype(BF16), wd_ref[...])


def _ffn(x, g2, wu, wd, *, layer, tm):
    m = x.shape[0]
    return pl.pallas_call(
        _ffn_kernel,
        grid=(m // tm, D_FF // FFN_TF),
        in_specs=[pl.BlockSpec((tm, D_MODEL), lambda i, j: (i, 0)),
                  pl.BlockSpec((None, 1, D_MODEL), lambda i, j: (layer, 0, 0)),
                  pl.BlockSpec((None, D_MODEL, FFN_TF), lambda i, j: (layer, 0, j)),
                  pl.BlockSpec((None, FFN_TF, D_MODEL), lambda i, j: (layer, j, 0))],
        out_specs=pl.BlockSpec((tm, D_MODEL), lambda i, j: (i, 0)),
        out_shape=jax.ShapeDtypeStruct((m, D_MODEL), F32),
        scratch_shapes=[pltpu.VMEM((tm, D_MODEL), BF16)],
        compiler_params=_params(("parallel", "arbitrary")),
        name="ffn",
    )(x, g2, wu, wd)


PROMPT_TM = 1024
MERGE_TM = 256


def kernel(x_prompt, x_sample, state_pool, cache_k, cache_v, norm1_g, w_in, b_gate, pool_w, pool_scale,
           q_norm_g, k_norm_g, rel_bias, w_branch_a, w_branch_b, w_out, norm2_g, w_up, w_down):
    nb, seq, _ = x_prompt.shape
    ns, t, _ = x_sample.shape
    depth = w_in.shape[0]
    past = cache_k.shape[2]
    assert seq % PROMPT_TM == 0 and seq % MERGE_TM == 0 and seq % Q_BLOCK == 0 and seq >= BAND_PAST
    assert t >= POOL_HIST and t % HIST_ROWS == 0 and past + t <= SAMPLE_KEYS_PAD and past <= BAND_PAST

    xp = x_prompt.reshape(nb * seq, D_MODEL)
    xs = x_sample.reshape(ns * t, D_MODEL)
    blocks_per_seq = seq // MERGE_TM
    prompt_hist_spec = pl.BlockSpec(
        (HIST_ROWS, POOL_WIDTH), lambda i: (jnp.maximum(i * (MERGE_TM // HIST_ROWS) - 1, 0), 0))
    sample_hist_spec = pl.BlockSpec((ns * HIST_ROWS, POOL_WIDTH), lambda i: (0, 0))

    g1 = norm1_g.reshape(depth, 1, D_MODEL)
    g2 = norm2_g.reshape(depth, 1, D_MODEL)
    bg = b_gate.reshape(depth, 1, 2 * D_MODEL)
    qg = q_norm_g.reshape(depth, 1, HEAD_DIM)
    kg = k_norm_g.reshape(depth, 1, HEAD_DIM)
    ps = pool_scale.reshape(depth, 1, POOL_WIDTH)
    w_in_b = w_in.astype(BF16)
    pw = pool_w.astype(BF16)
    wa = w_branch_a.astype(BF16)
    wb = w_branch_b.astype(BF16)
    wo = w_out.astype(BF16)
    wu = w_up.astype(BF16)
    wd = w_down.astype(BF16)
    hist_s = jnp.pad(state_pool, ((0, 0), (0, 0), (HIST_ROWS - POOL_HIST, 0), (0, 0)))
    hist_s = hist_s.reshape(depth, ns * HIST_ROWS, POOL_WIDTH)

    bias_p = _bias_table(rel_bias, q_rows=Q_BLOCK, keys=KEY_BLOCKS * Q_BLOCK,
                         offset=(KEY_BLOCKS - 1) * Q_BLOCK, visible_fn=_prompt_visible)
    bias_s = _bias_table(rel_bias, q_rows=t, keys=SAMPLE_KEYS_PAD, offset=past,
                         visible_fn=lambda qi, kj: kj < past + t)

    pool_p, kp_l, vp_l, pool_s, ks_l, vs_l = [], [], [], [], [], []
    for l in range(depth):
        p, q, k, v, kf, vf, gates = _in_proj(xp, g1, w_in_b, bg, qg, kg, layer=l, tm=PROMPT_TM)
        attn = _attn_prompt(q, k, v, bias_p, layer=l, n=nb, s=seq)
        pool_p.append(p.reshape(nb, seq, POOL_WIDTH)[:, seq - POOL_HIST:])
        kp_l.append(kf.reshape(nb, seq, N_HEADS, HEAD_DIM)[:, seq - BAND_PAST:])
        vp_l.append(vf.reshape(nb, seq, N_HEADS, HEAD_DIM)[:, seq - BAND_PAST:])
        xp = _merge(xp, p, p, attn, gates, pw, ps, wa, wb, wo, layer=l, tm=MERGE_TM, nseq=1, t=MERGE_TM,
                    blocks_per_seq=blocks_per_seq, pos0=None, hist_spec=prompt_hist_spec)
        xp = _ffn(xp, g2, wu, wd, layer=l, tm=PROMPT_TM)

        p, q, k, v, kf, vf, gates = _in_proj(xs, g1, w_in_b, bg, qg, kg, layer=l, tm=ns * t)
        attn, k_new, v_new = _attn_sample(q, kf, vf, cache_k, cache_v, bias_s, layer=l, n=ns, t=t)
        pool_s.append(p.reshape(ns, t, POOL_WIDTH)[:, t - POOL_HIST:])
        ks_l.append(k_new)
        vs_l.append(v_new)
        xs = _merge(xs, p, hist_s[l], attn, gates, pw, ps, wa, wb, wo, layer=l, tm=ns * t, nseq=ns, t=t,
                    blocks_per_seq=None, pos0=PAST_LEN, hist_spec=sample_hist_spec)
        xs = _ffn(xs, g2, wu, wd, layer=l, tm=ns * t)

    return (xp.reshape(nb, seq, D_MODEL), xs.reshape(ns, t, D_MODEL),
            jnp.stack(pool_p), jnp.stack(kp_l), jnp.stack(vp_l),
            jnp.stack(pool_s), jnp.stack(ks_l), jnp.stack(vs_l))
```

```python
import functools

import numpy as np

import jax
import jax.numpy as jnp
from jax import lax
from jax.experimental import pallas as pl
from jax.experimental.pallas import tpu as pltpu

D_MODEL = 2048
PAST_LEN = 1024
CHUNK = 64
N_PREV_CHUNKS = 8
BAND_PAST = N_PREV_CHUNKS * CHUNK
POOL_WIDTH = D_MODEL // 2
POOL_WINDOWS = (2, 4, 8, 16)
POOL_GROUP = POOL_WIDTH // len(POOL_WINDOWS)
POOL_HIST = max(POOL_WINDOWS) - 1
HEAD_DIM = 128
ATTN_WIDTH = D_MODEL // 2
N_HEADS = ATTN_WIDTH // HEAD_DIM
MAX_REL = 128
D_FF = 4 * D_MODEL
EPS = 1e-6
ATTN_SCALE = HEAD_DIM ** -0.5
NEG_INF = -1e30
IN_WIDTH = POOL_WIDTH + 3 * ATTN_WIDTH + 2 * D_MODEL

BF16 = jnp.bfloat16
F32 = jnp.float32

VMEM_LIMIT_BYTES = 56 * 1024 * 1024
LANES = 128

HIST_ROWS = 16
Q_BLOCK = 4 * CHUNK
KEY_BLOCKS = 3
SAMPLE_KEYS_PAD = 640


def _rms(xf, g):
    ms = jnp.mean(xf * xf, axis=-1, keepdims=True)
    return xf * lax.rsqrt(ms + EPS) * g


def _dot(a, b):
    return jnp.dot(a, b, preferred_element_type=F32)


def _dot_t(a, b):
    return lax.dot_general(a, b, (((1,), (1,)), ((), ())), preferred_element_type=F32)


def _params(semantics):
    return pltpu.CompilerParams(dimension_semantics=semantics, vmem_limit_bytes=VMEM_LIMIT_BYTES)


def _layer_spec(shape, layer):
    nd = len(shape)
    return pl.BlockSpec((None,) + tuple(shape), lambda *_: (layer,) + (0,) * nd,
                        pipeline_mode=pl.Buffered(1))


def _bias_table_kernel(ext_ref, o_ref, *, q_rows, keys, visible_fn):
    width = ext_ref.shape[-1]
    qi = lax.broadcasted_iota(jnp.int32, (q_rows, keys), 0)
    kj = lax.broadcasted_iota(jnp.int32, (q_rows, keys), 1)
    visible = visible_fn(qi, kj)
    for g in range(ext_ref.shape[0]):
        row = jnp.broadcast_to(ext_ref[g:g + 1, :], (q_rows, width))
        skew = pltpu.roll(row, 0, 1, stride=1, stride_axis=0)
        o_ref[g] = jnp.where(visible, skew[:, :keys], NEG_INF)


def _bias_table(rel_bias, *, q_rows, keys, offset, visible_fn):
    groups = rel_bias.shape[0] * rel_bias.shape[1]
    width = -(-(keys + q_rows - 1) // LANES) * LANES
    idx = np.zeros((width,), np.int32)
    idx[:keys] = np.clip(offset - np.arange(keys), -MAX_REL, MAX_REL) + MAX_REL
    back = np.arange(1, q_rows)
    idx[width - back] = np.clip(offset + back, -MAX_REL, MAX_REL) + MAX_REL
    ext = rel_bias.reshape(groups, 2 * MAX_REL + 1)[:, idx].astype(F32)
    return pl.pallas_call(
        functools.partial(_bias_table_kernel, q_rows=q_rows, keys=keys, visible_fn=visible_fn),
        out_shape=jax.ShapeDtypeStruct((groups, q_rows, keys), F32),
        compiler_params=pltpu.CompilerParams(vmem_limit_bytes=VMEM_LIMIT_BYTES),
        name="bias_table",
    )(ext)


def _prompt_visible(qi, kj):
    own = qi // CHUNK + (KEY_BLOCKS - 1) * (Q_BLOCK // CHUNK)
    kc = kj // CHUNK
    return (kc <= own) & (kc >= own - N_PREV_CHUNKS)


IN_STEPS = N_HEADS
IN_GATE_TN = 2 * D_MODEL // IN_STEPS


def _regroup_in_weights(w_in):
    depth = w_in.shape[0]
    mixed = w_in[:, :, :4 * ATTN_WIDTH].reshape(depth, D_MODEL, 4, N_HEADS, HEAD_DIM)
    mixed = mixed.transpose(0, 1, 3, 2, 4).reshape(depth, D_MODEL, 4 * ATTN_WIDTH)
    return mixed.astype(BF16), w_in[:, :, 4 * ATTN_WIDTH:].astype(BF16)


def _in_kernel(x_ref, g1_ref, wm_ref, wg_ref, bg_ref, qg_ref, kg_ref,
               p_ref, q_ref, k_ref, v_ref, kf_ref, vf_ref, gate_ref, h_scr):
    @pl.when(pl.program_id(1) == 0)
    def _():
        h_scr[...] = _rms(x_ref[...], g1_ref[...]).astype(BF16)

    h = h_scr[...]
    z = _dot(h, wm_ref[...])
    p_ref[...] = z[:, 0:HEAD_DIM]
    q_ref[...] = _rms(z[:, HEAD_DIM:2 * HEAD_DIM], qg_ref[...]).astype(BF16)
    kn = _rms(z[:, 2 * HEAD_DIM:3 * HEAD_DIM], kg_ref[...])
    kf_ref[...] = kn
    k_ref[...] = kn.astype(BF16)
    v = z[:, 3 * HEAD_DIM:4 * HEAD_DIM]
    vf_ref[...] = v
    v_ref[...] = v.astype(BF16)
    gate_ref[...] = jax.nn.sigmoid(_dot(h, wg_ref[...]) + bg_ref[...])


def _in_proj(x, g1, w_mixed, w_gate, b_gate, qg, kg, *, layer, tm):
    m = x.shape[0]
    head_spec = pl.BlockSpec((tm, HEAD_DIM), lambda i, j: (i, j))
    out_shape = (
        jax.ShapeDtypeStruct((m, POOL_WIDTH), F32),
        jax.ShapeDtypeStruct((m, ATTN_WIDTH), BF16),
        jax.ShapeDtypeStruct((m, ATTN_WIDTH), BF16),
        jax.ShapeDtypeStruct((m, ATTN_WIDTH), BF16),
        jax.ShapeDtypeStruct((m, ATTN_WIDTH), F32),
        jax.ShapeDtypeStruct((m, ATTN_WIDTH), F32),
        jax.ShapeDtypeStruct((m, 2 * D_MODEL), F32),
    )
    return pl.pallas_call(
        _in_kernel,
        grid=(m // tm, IN_STEPS),
        in_specs=[
            pl.BlockSpec((tm, D_MODEL), lambda i, j: (i, 0)),
            pl.BlockSpec((None, 1, D_MODEL), lambda i, j: (layer, 0, 0)),
            pl.BlockSpec((None, D_MODEL, 4 * HEAD_DIM), lambda i, j: (layer, 0, j)),
            pl.BlockSpec((None, D_MODEL, IN_GATE_TN), lambda i, j: (layer, 0, j)),
            pl.BlockSpec((None, 1, IN_GATE_TN), lambda i, j: (layer, 0, j)),
            pl.BlockSpec((None, 1, HEAD_DIM), lambda i, j: (layer, 0, 0)),
            pl.BlockSpec((None, 1, HEAD_DIM), lambda i, j: (layer, 0, 0)),
        ],
        out_specs=(head_spec,) * 6 + (pl.BlockSpec((tm, IN_GATE_TN), lambda i, j: (i, j)),),
        out_shape=out_shape,
        scratch_shapes=[pltpu.VMEM((tm, D_MODEL), BF16)],
        compiler_params=_params(("parallel", "arbitrary")),
        name="in_proj",
    )(x, g1, w_mixed, w_gate, b_gate, qg, kg)


def _attn_prompt_kernel(q_ref, k0_ref, k1_ref, k2_ref, v0_ref, v1_ref, v2_ref, bias_ref, o_ref):
    i = pl.program_id(1)
    k_refs = (k0_ref, k1_ref, k2_ref)
    v_refs = (v0_ref, v1_ref, v2_ref)
    for h in range(N_HEADS):
        sl = slice(h * HEAD_DIM, (h + 1) * HEAD_DIM)
        qh = q_ref[:, sl]
        s = []
        for d in range(KEY_BLOCKS):
            sd = _dot_t(qh, k_refs[d][:, sl]) * ATTN_SCALE + bias_ref[h, :, d * Q_BLOCK:(d + 1) * Q_BLOCK]
            if d < KEY_BLOCKS - 1:
                sd = jnp.where(i + d >= KEY_BLOCKS - 1, sd, NEG_INF)
            s.append(sd)
        m = jnp.max(jnp.maximum(jnp.maximum(s[0], s[1]), s[2]), axis=-1, keepdims=True)
        e = [jnp.exp(sd - m) for sd in s]
        l = jnp.sum(e[0] + e[1] + e[2], axis=-1, keepdims=True)
        o = _dot(e[0].astype(BF16), v_refs[0][:, sl])
        for d in range(1, KEY_BLOCKS):
            o = o + _dot(e[d].astype(BF16), v_refs[d][:, sl])
        o_ref[:, sl] = (o / l).astype(BF16)


def _attn_prompt(q, k, v, bias, *, layer, n, s):
    nb = s // Q_BLOCK
    q_spec = pl.BlockSpec((Q_BLOCK, ATTN_WIDTH), lambda b, i: (b * nb + i, 0))

    def kv_spec(d):
        return pl.BlockSpec((Q_BLOCK, ATTN_WIDTH),
                            lambda b, i: (b * nb + jnp.maximum(i - (KEY_BLOCKS - 1) + d, 0), 0))

    bias_spec = pl.BlockSpec((N_HEADS, Q_BLOCK, KEY_BLOCKS * Q_BLOCK), lambda b, i: (layer, 0, 0),
                             pipeline_mode=pl.Buffered(1))
    return pl.pallas_call(
        _attn_prompt_kernel,
        grid=(n, nb),
        in_specs=[q_spec] + [kv_spec(d) for d in range(KEY_BLOCKS)] * 2 + [bias_spec],
        out_specs=q_spec,
        out_shape=jax.ShapeDtypeStruct((n * s, ATTN_WIDTH), BF16),
        compiler_params=_params(("parallel", "arbitrary")),
        name="attn_prompt",
    )(q, k, k, k, v, v, v, bias)


def _attn_sample_kernel(q_ref, kf_ref, vf_ref, ck_ref, cv_ref, bias_ref,
                        o_ref, ko_ref, vo_ref, kk_scr, vv_scr, *, t, past):
    for new_ref, cache_ref, out_ref, scr in ((kf_ref, ck_ref, ko_ref, kk_scr),
                                             (vf_ref, cv_ref, vo_ref, vv_scr)):
        out_ref[0:past - t] = cache_ref[t:past]
        for h in range(N_HEADS):
            sl = slice(h * HEAD_DIM, (h + 1) * HEAD_DIM)
            out_ref[past - t:past, h, :] = new_ref[:, sl]
            scr[0:past, sl] = cache_ref[:, h, :].astype(BF16)
        scr[past:past + t, :] = new_ref[...].astype(BF16)
        scr[past + t:, :] = jnp.zeros((SAMPLE_KEYS_PAD - past - t, ATTN_WIDTH), BF16)
    for h in range(N_HEADS):
        sl = slice(h * HEAD_DIM, (h + 1) * HEAD_DIM)
        s = _dot_t(q_ref[:, sl], kk_scr[:, sl]) * ATTN_SCALE + bias_ref[h]
        m = jnp.max(s, axis=-1, keepdims=True)
        e = jnp.exp(s - m)
        l = jnp.sum(e, axis=-1, keepdims=True)
        o = _dot(e.astype(BF16), vv_scr[:, sl])
        o_ref[:, sl] = (o / l).astype(BF16)


def _attn_sample(q, kf, vf, cache_k, cache_v, bias, *, layer, n, t):
    past = cache_k.shape[2]
    row_spec = pl.BlockSpec((t, ATTN_WIDTH), lambda b: (b, 0))
    cache_spec = pl.BlockSpec((None, None, past, N_HEADS, HEAD_DIM), lambda b: (layer, b, 0, 0, 0))
    new_cache_spec = pl.BlockSpec((None, past, N_HEADS, HEAD_DIM), lambda b: (b, 0, 0, 0))
    bias_spec = pl.BlockSpec((N_HEADS, t, SAMPLE_KEYS_PAD), lambda b: (layer, 0, 0),
                             pipeline_mode=pl.Buffered(1))
    new_cache = jax.ShapeDtypeStruct((n, past, N_HEADS, HEAD_DIM), F32)
    return pl.pallas_call(
        functools.partial(_attn_sample_kernel, t=t, past=past),
        grid=(n,),
        in_specs=[row_spec, row_spec, row_spec, cache_spec, cache_spec, bias_spec],
        out_specs=(row_spec, new_cache_spec, new_cache_spec),
        out_shape=(jax.ShapeDtypeStruct((n * t, ATTN_WIDTH), BF16), new_cache, new_cache),
        scratch_shapes=[pltpu.VMEM((SAMPLE_KEYS_PAD, ATTN_WIDTH), BF16),
                        pltpu.VMEM((SAMPLE_KEYS_PAD, ATTN_WIDTH), BF16)],
        compiler_params=_params(("arbitrary",)),
        name="attn_sample",
    )(q, kf, vf, cache_k, cache_v, bias)


def _merge_kernel(x_ref, p_ref, hist_ref, a_ref, gate_ref, pw_ref, ps_ref, wa_ref, wb_ref, wo_ref,
                  o_ref, xp_scr, d_scr, *, nseq, t, blocks_per_seq, pos0):
    i = pl.program_id(0)
    row = lax.broadcasted_iota(jnp.int32, (t, 1), 0)
    if blocks_per_seq is None:
        pos = pos0 + row
    else:
        pos = (i % blocks_per_seq) * t + row
    for s in range(nseq):
        rows = slice(s * t, (s + 1) * t)
        hist = hist_ref[s * HIST_ROWS:(s + 1) * HIST_ROWS, :]
        if blocks_per_seq is not None:
            hist = jnp.where(i % blocks_per_seq == 0, 0.0, hist)
        xp_scr[0:HIST_ROWS, :] = hist
        xp_scr[HIST_ROWS:, :] = p_ref[rows, :]
        for gi, w in enumerate(POOL_WINDOWS):
            cols = slice(gi * POOL_GROUP, (gi + 1) * POOL_GROUP)
            acc = xp_scr[HIST_ROWS:HIST_ROWS + t, cols]
            for sh in range(1, w):
                acc = acc + xp_scr[HIST_ROWS - sh:HIST_ROWS - sh + t, cols]
            cnt = jnp.minimum(pos + 1, w).astype(F32)
            d_scr[rows, cols] = (acc / cnt - p_ref[rows, cols]).astype(BF16)
    pooled = jnp.concatenate(
        [_dot(d_scr[:, gi * POOL_GROUP:(gi + 1) * POOL_GROUP], pw_ref[gi]) for gi in range(len(POOL_WINDOWS))],
        axis=-1) * ps_ref[...]
    branch_a = _dot(pooled.astype(BF16), wa_ref[...])
    branch_b = _dot(a_ref[...], wb_ref[...])
    merged = gate_ref[:, :D_MODEL] * branch_a + gate_ref[:, D_MODEL:] * branch_b
    o_ref[...] = x_ref[...] + _dot(merged.astype(BF16), wo_ref[...])


def _merge(x, p, hist, attn, gates, pool_w, pool_scale, wa, wb, wo, *, layer, tm, nseq, t, blocks_per_seq, pos0,
           hist_spec):
    m = x.shape[0]

    def row_spec(width):
        return pl.BlockSpec((tm, width), lambda i: (i, 0))

    return pl.pallas_call(
        functools.partial(_merge_kernel, nseq=nseq, t=t, blocks_per_seq=blocks_per_seq, pos0=pos0),
        grid=(m // tm,),
        in_specs=[row_spec(D_MODEL), row_spec(POOL_WIDTH), hist_spec, row_spec(ATTN_WIDTH),
                  row_spec(2 * D_MODEL),
                  _layer_spec(pool_w.shape[1:], layer), _layer_spec(pool_scale.shape[1:], layer),
                  _layer_spec(wa.shape[1:], layer), _layer_spec(wb.shape[1:], layer),
                  _layer_spec(wo.shape[1:], layer)],
        out_specs=row_spec(D_MODEL),
        out_shape=jax.ShapeDtypeStruct((m, D_MODEL), F32),
        scratch_shapes=[pltpu.VMEM((HIST_ROWS + t, POOL_WIDTH), F32),
                        pltpu.VMEM((tm, POOL_WIDTH), BF16)],
        compiler_params=_params(("arbitrary",)),
        name="merge",
    )(x, p, hist, attn, gates, pool_w, pool_scale, wa, wb, wo)


FFN_TF = 512


def _ffn_kernel(x_ref, g2_ref, wu_ref, wd_ref, o_ref, h_scr):
    j = pl.program_id(1)

    @pl.when(j == 0)
    def _():
        x = x_ref[...]
        h_scr[...] = _rms(x, g2_ref[...]).astype(BF16)
        o_ref[...] = x

    u = jnp.square(jnp.maximum(_dot(h_scr[...], wu_ref[...]), 0.0))
    o_ref[...] += _dot(u.astype(BF16), wd_ref[...])


def _ffn(x, g2, wu, wd, *, layer, tm):
    m = x.shape[0]
    return pl.pallas_call(
        _ffn_kernel,
        grid=(m // tm, D_FF // FFN_TF),
        in_specs=[pl.BlockSpec((tm, D_MODEL), lambda i, j: (i, 0)),
                  pl.BlockSpec((None, 1, D_MODEL), lambda i, j: (layer, 0, 0)),
                  pl.BlockSpec((None, D_MODEL, FFN_TF), lambda i, j: (layer, 0, j)),
                  pl.BlockSpec((None, FFN_TF, D_MODEL), lambda i, j: (layer, j, 0))],
        out_specs=pl.BlockSpec((tm, D_MODEL), lambda i, j: (i, 0)),
        out_shape=jax.ShapeDtypeStruct((m, D_MODEL), F32),
        scratch_shapes=[pltpu.VMEM((tm, D_MODEL), BF16)],
        compiler_params=_params(("parallel", "arbitrary")),
        name="ffn",
    )(x, g2, wu, wd)


PROMPT_TM = 1024
MERGE_TM = 256


def kernel(x_prompt, x_sample, state_pool, cache_k, cache_v, norm1_g, w_in, b_gate, pool_w, pool_scale,
           q_norm_g, k_norm_g, rel_bias, w_branch_a, w_branch_b, w_out, norm2_g, w_up, w_down):
    nb, seq, _ = x_prompt.shape
    ns, t, _ = x_sample.shape
    depth = w_in.shape[0]
    past = cache_k.shape[2]
    assert seq % PROMPT_TM == 0 and seq % MERGE_TM == 0 and seq % Q_BLOCK == 0 and seq >= BAND_PAST
    assert t >= POOL_HIST and t % HIST_ROWS == 0 and past + t <= SAMPLE_KEYS_PAD and past <= BAND_PAST

    xp = x_prompt.reshape(nb * seq, D_MODEL)
    xs = x_sample.reshape(ns * t, D_MODEL)
    blocks_per_seq = seq // MERGE_TM
    prompt_hist_spec = pl.BlockSpec(
        (HIST_ROWS, POOL_WIDTH), lambda i: (jnp.maximum(i * (MERGE_TM // HIST_ROWS) - 1, 0), 0))
    sample_hist_spec = pl.BlockSpec((ns * HIST_ROWS, POOL_WIDTH), lambda i: (0, 0))

    g1 = norm1_g.reshape(depth, 1, D_MODEL)
    g2 = norm2_g.reshape(depth, 1, D_MODEL)
    bg = b_gate.reshape(depth, 1, 2 * D_MODEL)
    qg = q_norm_g.reshape(depth, 1, HEAD_DIM)
    kg = k_norm_g.reshape(depth, 1, HEAD_DIM)
    ps = pool_scale.reshape(depth, 1, POOL_WIDTH)
    w_mixed, w_gate = _regroup_in_weights(w_in)
    pw = pool_w.astype(BF16)
    wa = w_branch_a.astype(BF16)
    wb = w_branch_b.astype(BF16)
    wo = w_out.astype(BF16)
    wu = w_up.astype(BF16)
    wd = w_down.astype(BF16)
    hist_s = jnp.pad(state_pool, ((0, 0), (0, 0), (HIST_ROWS - POOL_HIST, 0), (0, 0)))
    hist_s = hist_s.reshape(depth, ns * HIST_ROWS, POOL_WIDTH)

    bias_p = _bias_table(rel_bias, q_rows=Q_BLOCK, keys=KEY_BLOCKS * Q_BLOCK,
                         offset=(KEY_BLOCKS - 1) * Q_BLOCK, visible_fn=_prompt_visible)
    bias_s = _bias_table(rel_bias, q_rows=t, keys=SAMPLE_KEYS_PAD, offset=past,
                         visible_fn=lambda qi, kj: kj < past + t)

    pool_p, kp_l, vp_l, pool_s, ks_l, vs_l = [], [], [], [], [], []
    for l in range(depth):
        p, q, k, v, kf, vf, gates = _in_proj(xp, g1, w_mixed, w_gate, bg, qg, kg, layer=l, tm=PROMPT_TM)
        attn = _attn_prompt(q, k, v, bias_p, layer=l, n=nb, s=seq)
        pool_p.append(p.reshape(nb, seq, POOL_WIDTH)[:, seq - POOL_HIST:])
        kp_l.append(kf.reshape(nb, seq, N_HEADS, HEAD_DIM)[:, seq - BAND_PAST:])
        vp_l.append(vf.reshape(nb, seq, N_HEADS, HEAD_DIM)[:, seq - BAND_PAST:])
        xp = _merge(xp, p, p, attn, gates, pw, ps, wa, wb, wo, layer=l, tm=MERGE_TM, nseq=1, t=MERGE_TM,
                    blocks_per_seq=blocks_per_seq, pos0=None, hist_spec=prompt_hist_spec)
        xp = _ffn(xp, g2, wu, wd, layer=l, tm=PROMPT_TM)

        p, q, k, v, kf, vf, gates = _in_proj(xs, g1, w_mixed, w_gate, bg, qg, kg, layer=l, tm=ns * t)
        attn, k_new, v_new = _attn_sample(q, kf, vf, cache_k, cache_v, bias_s, layer=l, n=ns, t=t)
        pool_s.append(p.reshape(ns, t, POOL_WIDTH)[:, t - POOL_HIST:])
        ks_l.append(k_new)
        vs_l.append(v_new)
        xs = _merge(xs, p, hist_s[l], attn, gates, pw, ps, wa, wb, wo, layer=l, tm=ns * t, nseq=ns, t=t,
                    blocks_per_seq=None, pos0=PAST_LEN, hist_spec=sample_hist_spec)
        xs = _ffn(xs, g2, wu, wd, layer=l, tm=ns * t)

    return (xp.reshape(nb, seq, D_MODEL), xs.reshape(ns, t, D_MODEL),
            jnp.stack(pool_p), jnp.stack(kp_l), jnp.stack(vp_l),
            jnp.stack(pool_s), jnp.stack(ks_l), jnp.stack(vs_l))
```

```python
import functools

import numpy as np

import jax
import jax.numpy as jnp
from jax import lax
from jax.experimental import pallas as pl
from jax.experimental.pallas import tpu as pltpu

D_MODEL = 2048
PAST_LEN = 1024
CHUNK = 64
N_PREV_CHUNKS = 8
BAND_PAST = N_PREV_CHUNKS * CHUNK
POOL_WIDTH = D_MODEL // 2
POOL_WINDOWS = (2, 4, 8, 16)
POOL_GROUP = POOL_WIDTH // len(POOL_WINDOWS)
POOL_HIST = max(POOL_WINDOWS) - 1
HEAD_DIM = 128
ATTN_WIDTH = D_MODEL // 2
N_HEADS = ATTN_WIDTH // HEAD_DIM
MAX_REL = 128
D_FF = 4 * D_MODEL
EPS = 1e-6
ATTN_SCALE = HEAD_DIM ** -0.5
NEG_INF = -1e30
IN_WIDTH = POOL_WIDTH + 3 * ATTN_WIDTH + 2 * D_MODEL

BF16 = jnp.bfloat16
F32 = jnp.float32

VMEM_LIMIT_BYTES = 56 * 1024 * 1024
LANES = 128

HIST_ROWS = 16
Q_BLOCK = 4 * CHUNK
KEY_BLOCKS = 3
SAMPLE_KEYS_PAD = 640


def _rms(xf, g):
    ms = jnp.mean(xf * xf, axis=-1, keepdims=True)
    return xf * lax.rsqrt(ms + EPS) * g


def _dot(a, b):
    return jnp.dot(a, b, preferred_element_type=F32)


def _dot_t(a, b):
    return lax.dot_general(a, b, (((1,), (1,)), ((), ())), preferred_element_type=F32)


def _params(semantics):
    return pltpu.CompilerParams(dimension_semantics=semantics, vmem_limit_bytes=VMEM_LIMIT_BYTES)


def _layer_spec(shape, layer):
    nd = len(shape)
    return pl.BlockSpec((None,) + tuple(shape), lambda *_: (layer,) + (0,) * nd,
                        pipeline_mode=pl.Buffered(1))


def _bias_table_kernel(ext_ref, o_ref, *, q_rows, keys, visible_fn):
    width = ext_ref.shape[-1]
    qi = lax.broadcasted_iota(jnp.int32, (q_rows, keys), 0)
    kj = lax.broadcasted_iota(jnp.int32, (q_rows, keys), 1)
    visible = visible_fn(qi, kj)
    for g in range(ext_ref.shape[0]):
        row = jnp.broadcast_to(ext_ref[g:g + 1, :], (q_rows, width))
        skew = pltpu.roll(row, 0, 1, stride=1, stride_axis=0)
        o_ref[g] = jnp.where(visible, skew[:, :keys], NEG_INF)


def _bias_table(rel_bias, *, q_rows, keys, offset, visible_fn):
    groups = rel_bias.shape[0] * rel_bias.shape[1]
    width = -(-(keys + q_rows - 1) // LANES) * LANES
    idx = np.zeros((width,), np.int32)
    idx[:keys] = np.clip(offset - np.arange(keys), -MAX_REL, MAX_REL) + MAX_REL
    back = np.arange(1, q_rows)
    idx[width - back] = np.clip(offset + back, -MAX_REL, MAX_REL) + MAX_REL
    ext = rel_bias.reshape(groups, 2 * MAX_REL + 1)[:, idx].astype(F32)
    return pl.pallas_call(
        functools.partial(_bias_table_kernel, q_rows=q_rows, keys=keys, visible_fn=visible_fn),
        out_shape=jax.ShapeDtypeStruct((groups, q_rows, keys), F32),
        compiler_params=pltpu.CompilerParams(vmem_limit_bytes=VMEM_LIMIT_BYTES),
        name="bias_table",
    )(ext)


def _prompt_visible(qi, kj):
    own = qi // CHUNK + (KEY_BLOCKS - 1) * (Q_BLOCK // CHUNK)
    kc = kj // CHUNK
    return (kc <= own) & (kc >= own - N_PREV_CHUNKS)


IN_STEPS = N_HEADS
IN_GATE_TN = 2 * D_MODEL // IN_STEPS


def _regroup_kernel(wp_ref, wq_ref, wk_ref, wv_ref, wg_ref, mixed_ref, gate_ref):
    for s, src in enumerate((wp_ref, wq_ref, wk_ref, wv_ref)):
        mixed_ref[:, s * HEAD_DIM:(s + 1) * HEAD_DIM] = src[...].astype(BF16)
    gate_ref[...] = wg_ref[...].astype(BF16)


def _regroup_in_weights(w_in):
    depth = w_in.shape[0]

    def head_cols(seg):
        return pl.BlockSpec((None, D_MODEL, HEAD_DIM), lambda l, j: (l, 0, seg * N_HEADS + j))

    gate0 = 4 * ATTN_WIDTH // IN_GATE_TN
    return pl.pallas_call(
        _regroup_kernel,
        grid=(depth, IN_STEPS),
        in_specs=[head_cols(0), head_cols(1), head_cols(2), head_cols(3),
                  pl.BlockSpec((None, D_MODEL, IN_GATE_TN), lambda l, j: (l, 0, gate0 + j))],
        out_specs=(pl.BlockSpec((None, D_MODEL, 4 * HEAD_DIM), lambda l, j: (l, 0, j)),
                   pl.BlockSpec((None, D_MODEL, IN_GATE_TN), lambda l, j: (l, 0, j))),
        out_shape=(jax.ShapeDtypeStruct((depth, D_MODEL, 4 * ATTN_WIDTH), BF16),
                   jax.ShapeDtypeStruct((depth, D_MODEL, 2 * D_MODEL), BF16)),
        compiler_params=_params(("parallel", "parallel")),
        name="regroup_w_in",
    )(w_in, w_in, w_in, w_in, w_in)


def _in_kernel(x_ref, g1_ref, wm_ref, wg_ref, bg_ref, qg_ref, kg_ref,
               p_ref, q_ref, k_ref, v_ref, kf_ref, vf_ref, gate_ref, h_scr):
    @pl.when(pl.program_id(1) == 0)
    def _():
        h_scr[...] = _rms(x_ref[...], g1_ref[...]).astype(BF16)

    h = h_scr[...]
    z = _dot(h, wm_ref[...])
    p_ref[...] = z[:, 0:HEAD_DIM]
    q_ref[...] = _rms(z[:, HEAD_DIM:2 * HEAD_DIM], qg_ref[...]).astype(BF16)
    kn = _rms(z[:, 2 * HEAD_DIM:3 * HEAD_DIM], kg_ref[...])
    kf_ref[...] = kn
    k_ref[...] = kn.astype(BF16)
    v = z[:, 3 * HEAD_DIM:4 * HEAD_DIM]
    vf_ref[...] = v
    v_ref[...] = v.astype(BF16)
    gate_ref[...] = jax.nn.sigmoid(_dot(h, wg_ref[...]) + bg_ref[...])


def _in_proj(x, g1, w_mixed, w_gate, b_gate, qg, kg, *, layer, tm):
    m = x.shape[0]
    head_spec = pl.BlockSpec((tm, HEAD_DIM), lambda i, j: (i, j))
    out_shape = (
        jax.ShapeDtypeStruct((m, POOL_WIDTH), F32),
        jax.ShapeDtypeStruct((m, ATTN_WIDTH), BF16),
        jax.ShapeDtypeStruct((m, ATTN_WIDTH), BF16),
        jax.ShapeDtypeStruct((m, ATTN_WIDTH), BF16),
        jax.ShapeDtypeStruct((m, ATTN_WIDTH), F32),
        jax.ShapeDtypeStruct((m, ATTN_WIDTH), F32),
        jax.ShapeDtypeStruct((m, 2 * D_MODEL), F32),
    )
    return pl.pallas_call(
        _in_kernel,
        grid=(m // tm, IN_STEPS),
        in_specs=[
            pl.BlockSpec((tm, D_MODEL), lambda i, j: (i, 0)),
            pl.BlockSpec((None, 1, D_MODEL), lambda i, j: (layer, 0, 0)),
            pl.BlockSpec((None, D_MODEL, 4 * HEAD_DIM), lambda i, j: (layer, 0, j)),
            pl.BlockSpec((None, D_MODEL, IN_GATE_TN), lambda i, j: (layer, 0, j)),
            pl.BlockSpec((None, 1, IN_GATE_TN), lambda i, j: (layer, 0, j)),
            pl.BlockSpec((None, 1, HEAD_DIM), lambda i, j: (layer, 0, 0)),
            pl.BlockSpec((None, 1, HEAD_DIM), lambda i, j: (layer, 0, 0)),
        ],
        out_specs=(head_spec,) * 6 + (pl.BlockSpec((tm, IN_GATE_TN), lambda i, j: (i, j)),),
        out_shape=out_shape,
        scratch_shapes=[pltpu.VMEM((tm, D_MODEL), BF16)],
        compiler_params=_params(("parallel", "arbitrary")),
        name="in_proj",
    )(x, g1, w_mixed, w_gate, b_gate, qg, kg)


def _attn_prompt_kernel(q_ref, k0_ref, k1_ref, k2_ref, v0_ref, v1_ref, v2_ref, bias_ref, o_ref):
    i = pl.program_id(1)
    k_refs = (k0_ref, k1_ref, k2_ref)
    v_refs = (v0_ref, v1_ref, v2_ref)
    for h in range(N_HEADS):
        sl = slice(h * HEAD_DIM, (h + 1) * HEAD_DIM)
        qh = q_ref[:, sl]
        s = []
        for d in range(KEY_BLOCKS):
            sd = _dot_t(qh, k_refs[d][:, sl]) * ATTN_SCALE + bias_ref[h, :, d * Q_BLOCK:(d + 1) * Q_BLOCK]
            if d < KEY_BLOCKS - 1:
                sd = jnp.where(i + d >= KEY_BLOCKS - 1, sd, NEG_INF)
            s.append(sd)
        m = jnp.max(jnp.maximum(jnp.maximum(s[0], s[1]), s[2]), axis=-1, keepdims=True)
        e = [jnp.exp(sd - m) for sd in s]
        l = jnp.sum(e[0] + e[1] + e[2], axis=-1, keepdims=True)
        o = _dot(e[0].astype(BF16), v_refs[0][:, sl])
        for d in range(1, KEY_BLOCKS):
            o = o + _dot(e[d].astype(BF16), v_refs[d][:, sl])
        o_ref[:, sl] = (o / l).astype(BF16)


def _attn_prompt(q, k, v, bias, *, layer, n, s):
    nb = s // Q_BLOCK
    q_spec = pl.BlockSpec((Q_BLOCK, ATTN_WIDTH), lambda b, i: (b * nb + i, 0))

    def kv_spec(d):
        return pl.BlockSpec((Q_BLOCK, ATTN_WIDTH),
                            lambda b, i: (b * nb + jnp.maximum(i - (KEY_BLOCKS - 1) + d, 0), 0))

    bias_spec = pl.BlockSpec((N_HEADS, Q_BLOCK, KEY_BLOCKS * Q_BLOCK), lambda b, i: (layer, 0, 0),
                             pipeline_mode=pl.Buffered(1))
    return pl.pallas_call(
        _attn_prompt_kernel,
        grid=(n, nb),
        in_specs=[q_spec] + [kv_spec(d) for d in range(KEY_BLOCKS)] * 2 + [bias_spec],
        out_specs=q_spec,
        out_shape=jax.ShapeDtypeStruct((n * s, ATTN_WIDTH), BF16),
        compiler_params=_params(("parallel", "arbitrary")),
        name="attn_prompt",
    )(q, k, k, k, v, v, v, bias)


def _shift_cache_kernel(ck_ref, cv_ref, ko_ref, vo_ref, *, new_rows):
    for src, dst in ((ck_ref, ko_ref), (cv_ref, vo_ref)):
        keep = src.shape[0] - new_rows
        dst[0:keep, :] = src[new_rows:, :]
        dst[keep:, :] = jnp.zeros((new_rows, HEAD_DIM), F32)


def _shift_caches(cache_k, cache_v, *, t):
    depth, n, past = cache_k.shape[:3]
    rows = past * N_HEADS
    spec = pl.BlockSpec((None, None, rows, HEAD_DIM), lambda l, b: (l, b, 0, 0))
    shape = jax.ShapeDtypeStruct((depth, n, rows, HEAD_DIM), F32)
    return pl.pallas_call(
        functools.partial(_shift_cache_kernel, new_rows=t * N_HEADS),
        grid=(depth, n),
        in_specs=[spec, spec],
        out_specs=(spec, spec),
        out_shape=(shape, shape),
        compiler_params=_params(("parallel", "parallel")),
        name="shift_caches",
    )(cache_k.reshape(depth, n, rows, HEAD_DIM), cache_v.reshape(depth, n, rows, HEAD_DIM))


def _attn_sample_kernel(q_ref, kf_ref, vf_ref, ck_ref, cv_ref, bias_ref, k_alias, v_alias,
                        o_ref, ko_ref, vo_ref, kk_scr, vv_scr, *, t, past):
    del k_alias, v_alias
    for new_ref, cache_ref, out_ref, scr in ((kf_ref, ck_ref, ko_ref, kk_scr),
                                             (vf_ref, cv_ref, vo_ref, vv_scr)):
        for h in range(N_HEADS):
            sl = slice(h * HEAD_DIM, (h + 1) * HEAD_DIM)
            out_ref[pl.ds(h, t, stride=N_HEADS), :] = new_ref[:, sl]
            scr[0:past, sl] = cache_ref[pl.ds(h, past, stride=N_HEADS), :].astype(BF16)
        scr[past:past + t, :] = new_ref[...].astype(BF16)
        scr[past + t:, :] = jnp.zeros((SAMPLE_KEYS_PAD - past - t, ATTN_WIDTH), BF16)
    for h in range(N_HEADS):
        sl = slice(h * HEAD_DIM, (h + 1) * HEAD_DIM)
        s = _dot_t(q_ref[:, sl], kk_scr[:, sl]) * ATTN_SCALE + bias_ref[h]
        m = jnp.max(s, axis=-1, keepdims=True)
        e = jnp.exp(s - m)
        l = jnp.sum(e, axis=-1, keepdims=True)
        o = _dot(e.astype(BF16), vv_scr[:, sl])
        o_ref[:, sl] = (o / l).astype(BF16)


def _attn_sample(q, kf, vf, cache_k, cache_v, bias, k_upd, v_upd, *, layer, n, t):
    depth, _, past = cache_k.shape[:3]
    rows = past * N_HEADS
    new_rows = t * N_HEADS
    row_spec = pl.BlockSpec((t, ATTN_WIDTH), lambda b: (b, 0))
    cache_spec = pl.BlockSpec((None, None, rows, HEAD_DIM), lambda b: (layer, b, 0, 0))
    new_spec = pl.BlockSpec((None, None, new_rows, HEAD_DIM), lambda b: (layer, b, rows // new_rows - 1, 0))
    bias_spec = pl.BlockSpec((N_HEADS, t, SAMPLE_KEYS_PAD), lambda b: (layer, 0, 0),
                             pipeline_mode=pl.Buffered(1))
    any_spec = pl.BlockSpec(memory_space=pl.ANY)
    upd = jax.ShapeDtypeStruct(k_upd.shape, F32)
    return pl.pallas_call(
        functools.partial(_attn_sample_kernel, t=t, past=past),
        grid=(n,),
        in_specs=[row_spec, row_spec, row_spec, cache_spec, cache_spec, bias_spec, any_spec, any_spec],
        out_specs=(row_spec, new_spec, new_spec),
        out_shape=(jax.ShapeDtypeStruct((n * t, ATTN_WIDTH), BF16), upd, upd),
        input_output_aliases={6: 1, 7: 2},
        scratch_shapes=[pltpu.VMEM((SAMPLE_KEYS_PAD, ATTN_WIDTH), BF16),
                        pltpu.VMEM((SAMPLE_KEYS_PAD, ATTN_WIDTH), BF16)],
        compiler_params=_params(("arbitrary",)),
        name="attn_sample",
    )(q, kf, vf, cache_k.reshape(depth, n, rows, HEAD_DIM), cache_v.reshape(depth, n, rows, HEAD_DIM),
      bias, k_upd, v_upd)


def _merge_kernel(x_ref, p_ref, hist_ref, a_ref, gate_ref, pw_ref, ps_ref, wa_ref, wb_ref, wo_ref,
                  o_ref, xp_scr, d_scr, *, nseq, t, blocks_per_seq, pos0):
    i = pl.program_id(0)
    row = lax.broadcasted_iota(jnp.int32, (t, 1), 0)
    if blocks_per_seq is None:
        pos = pos0 + row
    else:
        pos = (i % blocks_per_seq) * t + row
    for s in range(nseq):
        rows = slice(s * t, (s + 1) * t)
        hist = hist_ref[s * HIST_ROWS:(s + 1) * HIST_ROWS, :]
        if blocks_per_seq is not None:
            hist = jnp.where(i % blocks_per_seq == 0, 0.0, hist)
        xp_scr[0:HIST_ROWS, :] = hist
        xp_scr[HIST_ROWS:, :] = p_ref[rows, :]
        for gi, w in enumerate(POOL_WINDOWS):
            cols = slice(gi * POOL_GROUP, (gi + 1) * POOL_GROUP)
            acc = xp_scr[HIST_ROWS:HIST_ROWS + t, cols]
            for sh in range(1, w):
                acc = acc + xp_scr[HIST_ROWS - sh:HIST_ROWS - sh + t, cols]
            cnt = jnp.minimum(pos + 1, w).astype(F32)
            d_scr[rows, cols] = (acc / cnt - p_ref[rows, cols]).astype(BF16)
    pooled = jnp.concatenate(
        [_dot(d_scr[:, gi * POOL_GROUP:(gi + 1) * POOL_GROUP], pw_ref[gi]) for gi in range(len(POOL_WINDOWS))],
        axis=-1) * ps_ref[...]
    branch_a = _dot(pooled.astype(BF16), wa_ref[...])
    branch_b = _dot(a_ref[...], wb_ref[...])
    merged = gate_ref[:, :D_MODEL] * branch_a + gate_ref[:, D_MODEL:] * branch_b
    o_ref[...] = x_ref[...] + _dot(merged.astype(BF16), wo_ref[...])


def _merge(x, p, hist, attn, gates, pool_w, pool_scale, wa, wb, wo, *, layer, tm, nseq, t, blocks_per_seq, pos0,
           hist_spec):
    m = x.shape[0]

    def row_spec(width):
        return pl.BlockSpec((tm, width), lambda i: (i, 0))

    return pl.pallas_call(
        functools.partial(_merge_kernel, nseq=nseq, t=t, blocks_per_seq=blocks_per_seq, pos0=pos0),
        grid=(m // tm,),
        in_specs=[row_spec(D_MODEL), row_spec(POOL_WIDTH), hist_spec, row_spec(ATTN_WIDTH),
                  row_spec(2 * D_MODEL),
                  _layer_spec(pool_w.shape[1:], layer), _layer_spec(pool_scale.shape[1:], layer),
                  _layer_spec(wa.shape[1:], layer), _layer_spec(wb.shape[1:], layer),
                  _layer_spec(wo.shape[1:], layer)],
        out_specs=row_spec(D_MODEL),
        out_shape=jax.ShapeDtypeStruct((m, D_MODEL), F32),
        scratch_shapes=[pltpu.VMEM((HIST_ROWS + t, POOL_WIDTH), F32),
                        pltpu.VMEM((tm, POOL_WIDTH), BF16)],
        compiler_params=_params(("arbitrary",)),
        name="merge",
    )(x, p, hist, attn, gates, pool_w, pool_scale, wa, wb, wo)


FFN_TF = 512


def _ffn_kernel(x_ref, g2_ref, wu_ref, wd_ref, o_ref, h_scr):
    j = pl.program_id(1)

    @pl.when(j == 0)
    def _():
        x = x_ref[...]
        h_scr[...] = _rms(x, g2_ref[...]).astype(BF16)
        o_ref[...] = x

    u = jnp.square(jnp.maximum(_dot(h_scr[...], wu_ref[...]), 0.0))
    o_ref[...] += _dot(u.astype(BF16), wd_ref[...])


def _ffn(x, g2, wu, wd, *, layer, tm):
    m = x.shape[0]
    return pl.pallas_call(
        _ffn_kernel,
        grid=(m // tm, D_FF // FFN_TF),
        in_specs=[pl.BlockSpec((tm, D_MODEL), lambda i, j: (i, 0)),
                  pl.BlockSpec((None, 1, D_MODEL), lambda i, j: (layer, 0, 0)),
                  pl.BlockSpec((None, D_MODEL, FFN_TF), lambda i, j: (layer, 0, j)),
                  pl.BlockSpec((None, FFN_TF, D_MODEL), lambda i, j: (layer, j, 0))],
        out_specs=pl.BlockSpec((tm, D_MODEL), lambda i, j: (i, 0)),
        out_shape=jax.ShapeDtypeStruct((m, D_MODEL), F32),
        scratch_shapes=[pltpu.VMEM((tm, D_MODEL), BF16)],
        compiler_params=_params(("parallel", "arbitrary")),
        name="ffn",
    )(x, g2, wu, wd)


PROMPT_TM = 1024
MERGE_TM = 256


def kernel(x_prompt, x_sample, state_pool, cache_k, cache_v, norm1_g, w_in, b_gate, pool_w, pool_scale,
           q_norm_g, k_norm_g, rel_bias, w_branch_a, w_branch_b, w_out, norm2_g, w_up, w_down):
    nb, seq, _ = x_prompt.shape
    ns, t, _ = x_sample.shape
    depth = w_in.shape[0]
    past = cache_k.shape[2]
    assert seq % PROMPT_TM == 0 and seq % MERGE_TM == 0 and seq % Q_BLOCK == 0 and seq >= BAND_PAST
    assert t >= POOL_HIST and t % HIST_ROWS == 0 and past + t <= SAMPLE_KEYS_PAD and past <= BAND_PAST
    assert past % t == 0 and past > t

    xp = x_prompt.reshape(nb * seq, D_MODEL)
    xs = x_sample.reshape(ns * t, D_MODEL)
    blocks_per_seq = seq // MERGE_TM
    prompt_hist_spec = pl.BlockSpec(
        (HIST_ROWS, POOL_WIDTH), lambda i: (jnp.maximum(i * (MERGE_TM // HIST_ROWS) - 1, 0), 0))
    sample_hist_spec = pl.BlockSpec((ns * HIST_ROWS, POOL_WIDTH), lambda i: (0, 0))

    g1 = norm1_g.reshape(depth, 1, D_MODEL)
    g2 = norm2_g.reshape(depth, 1, D_MODEL)
    bg = b_gate.reshape(depth, 1, 2 * D_MODEL)
    qg = q_norm_g.reshape(depth, 1, HEAD_DIM)
    kg = k_norm_g.reshape(depth, 1, HEAD_DIM)
    ps = pool_scale.reshape(depth, 1, POOL_WIDTH)
    w_mixed, w_gate = _regroup_in_weights(w_in)
    pw = pool_w.astype(BF16)
    wa = w_branch_a.astype(BF16)
    wb = w_branch_b.astype(BF16)
    wo = w_out.astype(BF16)
    wu = w_up.astype(BF16)
    wd = w_down.astype(BF16)
    hist_s = jnp.pad(state_pool, ((0, 0), (0, 0), (HIST_ROWS - POOL_HIST, 0), (0, 0)))
    hist_s = hist_s.reshape(depth, ns * HIST_ROWS, POOL_WIDTH)

    bias_p = _bias_table(rel_bias, q_rows=Q_BLOCK, keys=KEY_BLOCKS * Q_BLOCK,
                         offset=(KEY_BLOCKS - 1) * Q_BLOCK, visible_fn=_prompt_visible)
    bias_s = _bias_table(rel_bias, q_rows=t, keys=SAMPLE_KEYS_PAD, offset=past,
                         visible_fn=lambda qi, kj: kj < past + t)

    k_upd, v_upd = _shift_caches(cache_k, cache_v, t=t)

    pool_p, kp_l, vp_l, pool_s = [], [], [], []
    for l in range(depth):
        p, q, k, v, kf, vf, gates = _in_proj(xp, g1, w_mixed, w_gate, bg, qg, kg, layer=l, tm=PROMPT_TM)
        attn = _attn_prompt(q, k, v, bias_p, layer=l, n=nb, s=seq)
        pool_p.append(p.reshape(nb, seq, POOL_WIDTH)[:, seq - POOL_HIST:])
        kp_l.append(kf.reshape(nb, seq, N_HEADS, HEAD_DIM)[:, seq - BAND_PAST:])
        vp_l.append(vf.reshape(nb, seq, N_HEADS, HEAD_DIM)[:, seq - BAND_PAST:])
        xp = _merge(xp, p, p, attn, gates, pw, ps, wa, wb, wo, layer=l, tm=MERGE_TM, nseq=1, t=MERGE_TM,
                    blocks_per_seq=blocks_per_seq, pos0=None, hist_spec=prompt_hist_spec)
        xp = _ffn(xp, g2, wu, wd, layer=l, tm=PROMPT_TM)

        p, q, k, v, kf, vf, gates = _in_proj(xs, g1, w_mixed, w_gate, bg, qg, kg, layer=l, tm=ns * t)
        attn, k_upd, v_upd = _attn_sample(q, kf, vf, cache_k, cache_v, bias_s, k_upd, v_upd, layer=l, n=ns, t=t)
        pool_s.append(p.reshape(ns, t, POOL_WIDTH)[:, t - POOL_HIST:])
        xs = _merge(xs, p, hist_s[l], attn, gates, pw, ps, wa, wb, wo, layer=l, tm=ns * t, nseq=ns, t=t,
                    blocks_per_seq=None, pos0=PAST_LEN, hist_spec=sample_hist_spec)
        xs = _ffn(xs, g2, wu, wd, layer=l, tm=ns * t)

    return (xp.reshape(nb, seq, D_MODEL), xs.reshape(ns, t, D_MODEL),
            jnp.stack(pool_p), jnp.stack(kp_l), jnp.stack(vp_l),
            jnp.stack(pool_s), k_upd.reshape(cache_k.shape), v_upd.reshape(cache_v.shape))
```

```python
import functools

import numpy as np

import jax
import jax.numpy as jnp
from jax import lax
from jax.experimental import pallas as pl
from jax.experimental.pallas import tpu as pltpu

D_MODEL = 2048
PAST_LEN = 1024
CHUNK = 64
N_PREV_CHUNKS = 8
BAND_PAST = N_PREV_CHUNKS * CHUNK
POOL_WIDTH = D_MODEL // 2
POOL_WINDOWS = (2, 4, 8, 16)
POOL_GROUP = POOL_WIDTH // len(POOL_WINDOWS)
POOL_HIST = max(POOL_WINDOWS) - 1
HEAD_DIM = 128
ATTN_WIDTH = D_MODEL // 2
N_HEADS = ATTN_WIDTH // HEAD_DIM
MAX_REL = 128
D_FF = 4 * D_MODEL
EPS = 1e-6
ATTN_SCALE = HEAD_DIM ** -0.5
LOG2E = 1.4426950408889634
Q_SCALE = ATTN_SCALE * LOG2E
NEG_INF = -1e30
IN_WIDTH = POOL_WIDTH + 3 * ATTN_WIDTH + 2 * D_MODEL

BF16 = jnp.bfloat16
F32 = jnp.float32

VMEM_LIMIT_BYTES = 56 * 1024 * 1024
LANES = 128

HIST_ROWS = 16
Q_BLOCK = 4 * CHUNK
KEY_BLOCKS = 3
SAMPLE_KEYS_PAD = 640


def _rms(xf, g):
    ms = jnp.mean(xf * xf, axis=-1, keepdims=True)
    return xf * lax.rsqrt(ms + EPS) * g


def _dot(a, b):
    return jnp.dot(a, b, preferred_element_type=F32)


def _dot_t(a, b):
    return lax.dot_general(a, b, (((1,), (1,)), ((), ())), preferred_element_type=F32)


def _params(semantics):
    return pltpu.CompilerParams(dimension_semantics=semantics, vmem_limit_bytes=VMEM_LIMIT_BYTES)


def _layer_spec(shape, layer):
    nd = len(shape)
    return pl.BlockSpec((None,) + tuple(shape), lambda *_: (layer,) + (0,) * nd,
                        pipeline_mode=pl.Buffered(1))


def _bias_table_kernel(ext_ref, o_ref, *, q_rows, keys, visible_fn):
    width = ext_ref.shape[-1]
    qi = lax.broadcasted_iota(jnp.int32, (q_rows, keys), 0)
    kj = lax.broadcasted_iota(jnp.int32, (q_rows, keys), 1)
    visible = visible_fn(qi, kj)
    for g in range(ext_ref.shape[0]):
        row = jnp.broadcast_to(ext_ref[g:g + 1, :], (q_rows, width))
        skew = pltpu.roll(row, 0, 1, stride=1, stride_axis=0)
        o_ref[g] = jnp.where(visible, skew[:, :keys] * LOG2E, NEG_INF)


def _bias_table(rel_bias, *, q_rows, keys, offset, visible_fn):
    groups = rel_bias.shape[0] * rel_bias.shape[1]
    width = -(-(keys + q_rows - 1) // LANES) * LANES
    idx = np.zeros((width,), np.int32)
    idx[:keys] = np.clip(offset - np.arange(keys), -MAX_REL, MAX_REL) + MAX_REL
    back = np.arange(1, q_rows)
    idx[width - back] = np.clip(offset + back, -MAX_REL, MAX_REL) + MAX_REL
    ext = rel_bias.reshape(groups, 2 * MAX_REL + 1)[:, idx].astype(F32)
    return pl.pallas_call(
        functools.partial(_bias_table_kernel, q_rows=q_rows, keys=keys, visible_fn=visible_fn),
        out_shape=jax.ShapeDtypeStruct((groups, q_rows, keys), F32),
        compiler_params=pltpu.CompilerParams(vmem_limit_bytes=VMEM_LIMIT_BYTES),
        name="bias_table",
    )(ext)


def _prompt_visible(qi, kj):
    own = qi // CHUNK + (KEY_BLOCKS - 1) * (Q_BLOCK // CHUNK)
    kc = kj // CHUNK
    return (kc <= own) & (kc >= own - N_PREV_CHUNKS)


IN_STEPS = N_HEADS
IN_GATE_TN = 2 * D_MODEL // IN_STEPS


def _regroup_kernel(wp_ref, wq_ref, wk_ref, wv_ref, wg_ref, mixed_ref, gate_ref):
    for s, src in enumerate((wp_ref, wq_ref, wk_ref, wv_ref)):
        mixed_ref[:, s * HEAD_DIM:(s + 1) * HEAD_DIM] = src[...].astype(BF16)
    gate_ref[...] = wg_ref[...].astype(BF16)


def _regroup_in_weights(w_in):
    depth = w_in.shape[0]

    def head_cols(seg):
        return pl.BlockSpec((None, D_MODEL, HEAD_DIM), lambda l, j: (l, 0, seg * N_HEADS + j))

    gate0 = 4 * ATTN_WIDTH // IN_GATE_TN
    return pl.pallas_call(
        _regroup_kernel,
        grid=(depth, IN_STEPS),
        in_specs=[head_cols(0), head_cols(1), head_cols(2), head_cols(3),
                  pl.BlockSpec((None, D_MODEL, IN_GATE_TN), lambda l, j: (l, 0, gate0 + j))],
        out_specs=(pl.BlockSpec((None, D_MODEL, 4 * HEAD_DIM), lambda l, j: (l, 0, j)),
                   pl.BlockSpec((None, D_MODEL, IN_GATE_TN), lambda l, j: (l, 0, j))),
        out_shape=(jax.ShapeDtypeStruct((depth, D_MODEL, 4 * ATTN_WIDTH), BF16),
                   jax.ShapeDtypeStruct((depth, D_MODEL, 2 * D_MODEL), BF16)),
        compiler_params=_params(("parallel", "parallel")),
        name="regroup_w_in",
    )(w_in, w_in, w_in, w_in, w_in)


def _in_rows(h, wm_ref, wg_ref, bg_ref, qg_ref, kg_ref, p_ref, q_ref, kf_ref, vf_ref, gate_ref,
             k_ref=None, v_ref=None):
    z = _dot(h, wm_ref[...])
    p_ref[...] = z[:, 0:HEAD_DIM]
    q_ref[...] = (_rms(z[:, HEAD_DIM:2 * HEAD_DIM], qg_ref[...]) * Q_SCALE).astype(BF16)
    kn = _rms(z[:, 2 * HEAD_DIM:3 * HEAD_DIM], kg_ref[...])
    v = z[:, 3 * HEAD_DIM:4 * HEAD_DIM]
    kf_ref[...] = kn
    vf_ref[...] = v
    if k_ref is not None:
        k_ref[...] = kn.astype(BF16)
        v_ref[...] = v.astype(BF16)
    gate_ref[...] = jax.nn.sigmoid(_dot(h, wg_ref[...]) + bg_ref[...])


def _in_kernel(x_ref, xs_ref, g1_ref, wm_ref, wg_ref, bg_ref, qg_ref, kg_ref,
               p_ref, q_ref, k_ref, v_ref, kf_ref, vf_ref, gate_ref,
               ps_ref, qs_ref, kfs_ref, vfs_ref, gates_ref, h_scr, hs_scr):
    i = pl.program_id(0)
    j = pl.program_id(1)
    weights = (wm_ref, wg_ref, bg_ref, qg_ref, kg_ref)

    @pl.when(j == 0)
    def _():
        h_scr[...] = _rms(x_ref[...], g1_ref[...]).astype(BF16)

    @pl.when((i == 0) & (j == 0))
    def _():
        hs_scr[...] = _rms(xs_ref[...], g1_ref[...]).astype(BF16)

    _in_rows(h_scr[...], *weights, p_ref, q_ref, kf_ref, vf_ref, gate_ref, k_ref, v_ref)

    @pl.when(i == 0)
    def _():
        _in_rows(hs_scr[...], *weights, ps_ref, qs_ref, kfs_ref, vfs_ref, gates_ref)


def _in_proj(x, xs, g1, w_mixed, w_gate, b_gate, qg, kg, *, layer, tm):
    m = x.shape[0]
    ms = xs.shape[0]
    head_spec = pl.BlockSpec((tm, HEAD_DIM), lambda i, j: (i, j))

    def sample_spec(width):
        return pl.BlockSpec((ms, width), lambda i, j: (0, jnp.where(i == 0, j, IN_STEPS - 1)))

    out_shape = (
        jax.ShapeDtypeStruct((m, POOL_WIDTH), F32),
        jax.ShapeDtypeStruct((m, ATTN_WIDTH), BF16),
        jax.ShapeDtypeStruct((m, ATTN_WIDTH), BF16),
        jax.ShapeDtypeStruct((m, ATTN_WIDTH), BF16),
        jax.ShapeDtypeStruct((m, ATTN_WIDTH), F32),
        jax.ShapeDtypeStruct((m, ATTN_WIDTH), F32),
        jax.ShapeDtypeStruct((m, 2 * D_MODEL), F32),
        jax.ShapeDtypeStruct((ms, POOL_WIDTH), F32),
        jax.ShapeDtypeStruct((ms, ATTN_WIDTH), BF16),
        jax.ShapeDtypeStruct((ms, ATTN_WIDTH), F32),
        jax.ShapeDtypeStruct((ms, ATTN_WIDTH), F32),
        jax.ShapeDtypeStruct((ms, 2 * D_MODEL), F32),
    )
    return pl.pallas_call(
        _in_kernel,
        grid=(m // tm, IN_STEPS),
        in_specs=[
            pl.BlockSpec((tm, D_MODEL), lambda i, j: (i, 0)),
            pl.BlockSpec((ms, D_MODEL), lambda i, j: (0, 0)),
            pl.BlockSpec((None, 1, D_MODEL), lambda i, j: (layer, 0, 0)),
            pl.BlockSpec((None, D_MODEL, 4 * HEAD_DIM), lambda i, j: (layer, 0, j)),
            pl.BlockSpec((None, D_MODEL, IN_GATE_TN), lambda i, j: (layer, 0, j)),
            pl.BlockSpec((None, 1, IN_GATE_TN), lambda i, j: (layer, 0, j)),
            pl.BlockSpec((None, 1, HEAD_DIM), lambda i, j: (layer, 0, 0)),
            pl.BlockSpec((None, 1, HEAD_DIM), lambda i, j: (layer, 0, 0)),
        ],
        out_specs=(head_spec,) * 6 + (pl.BlockSpec((tm, IN_GATE_TN), lambda i, j: (i, j)),)
                  + (sample_spec(HEAD_DIM),) * 4 + (sample_spec(IN_GATE_TN),),
        out_shape=out_shape,
        scratch_shapes=[pltpu.VMEM((tm, D_MODEL), BF16), pltpu.VMEM((ms, D_MODEL), BF16)],
        compiler_params=_params(("arbitrary", "arbitrary")),
        name="in_proj",
    )(x, xs, g1, w_mixed, w_gate, b_gate, qg, kg)


def _attn_prompt_kernel(q_ref, k0_ref, k1_ref, k2_ref, v0_ref, v1_ref, v2_ref, bias_ref, o_ref):
    i = pl.program_id(1)
    k_refs = (k0_ref, k1_ref, k2_ref)
    v_refs = (v0_ref, v1_ref, v2_ref)

    def heads(n_valid):
        blocks = range(KEY_BLOCKS - n_valid, KEY_BLOCKS)
        for h in range(N_HEADS):
            sl = slice(h * HEAD_DIM, (h + 1) * HEAD_DIM)
            qh = q_ref[:, sl]
            m = l = o = None
            for d in blocks:
                s = _dot_t(qh, k_refs[d][:, sl]) + bias_ref[h, :, d * Q_BLOCK:(d + 1) * Q_BLOCK]
                m_d = jnp.max(s, axis=-1, keepdims=True)
                if m is None:
                    m = m_d
                    e = jnp.exp2(s - m)
                    l = jnp.sum(e, axis=-1, keepdims=True)
                    o = _dot(e.astype(BF16), v_refs[d][:, sl])
                else:
                    m_new = jnp.maximum(m, m_d)
                    alpha = jnp.exp2(m - m_new)
                    e = jnp.exp2(s - m_new)
                    l = alpha * l + jnp.sum(e, axis=-1, keepdims=True)
                    o = alpha * o + _dot(e.astype(BF16), v_refs[d][:, sl])
                    m = m_new
            o_ref[:, sl] = (o / l).astype(BF16)

    for n_valid in range(1, KEY_BLOCKS):
        pl.when(i == n_valid - 1)(functools.partial(heads, n_valid))
    pl.when(i >= KEY_BLOCKS - 1)(functools.partial(heads, KEY_BLOCKS))


def _attn_prompt(q, k, v, bias, *, layer, n, s):
    nb = s // Q_BLOCK
    q_spec = pl.BlockSpec((Q_BLOCK, ATTN_WIDTH), lambda b, i: (b * nb + i, 0))

    def kv_spec(d):
        return pl.BlockSpec((Q_BLOCK, ATTN_WIDTH),
                            lambda b, i: (b * nb + jnp.maximum(i - (KEY_BLOCKS - 1) + d, 0), 0))

    bias_spec = pl.BlockSpec((N_HEADS, Q_BLOCK, KEY_BLOCKS * Q_BLOCK), lambda b, i: (layer, 0, 0),
                             pipeline_mode=pl.Buffered(1))
    return pl.pallas_call(
        _attn_prompt_kernel,
        grid=(n, nb),
        in_specs=[q_spec] + [kv_spec(d) for d in range(KEY_BLOCKS)] * 2 + [bias_spec],
        out_specs=q_spec,
        out_shape=jax.ShapeDtypeStruct((n * s, ATTN_WIDTH), BF16),
        compiler_params=_params(("parallel", "arbitrary")),
        name="attn_prompt",
    )(q, k, k, k, v, v, v, bias)


def _shift_cache_kernel(ck_ref, cv_ref, ko_ref, vo_ref, *, new_rows):
    for src, dst in ((ck_ref, ko_ref), (cv_ref, vo_ref)):
        keep = src.shape[0] - new_rows
        dst[0:keep, :] = src[new_rows:, :]
        dst[keep:, :] = jnp.zeros((new_rows, HEAD_DIM), F32)


def _shift_caches(cache_k, cache_v, *, t):
    depth, n, past = cache_k.shape[:3]
    rows = past * N_HEADS
    spec = pl.BlockSpec((None, None, rows, HEAD_DIM), lambda l, b: (l, b, 0, 0))
    shape = jax.ShapeDtypeStruct((depth, n, rows, HEAD_DIM), F32)
    return pl.pallas_call(
        functools.partial(_shift_cache_kernel, new_rows=t * N_HEADS),
        grid=(depth, n),
        in_specs=[spec, spec],
        out_specs=(spec, spec),
        out_shape=(shape, shape),
        compiler_params=_params(("parallel", "parallel")),
        name="shift_caches",
    )(cache_k.reshape(depth, n, rows, HEAD_DIM), cache_v.reshape(depth, n, rows, HEAD_DIM))


def _attn_sample_kernel(q_ref, kf_ref, vf_ref, ck_ref, cv_ref, bias_ref, k_alias, v_alias,
                        o_ref, ko_ref, vo_ref, kk_scr, vv_scr, *, t, past):
    del k_alias, v_alias
    for new_ref, cache_ref, out_ref, scr in ((kf_ref, ck_ref, ko_ref, kk_scr),
                                             (vf_ref, cv_ref, vo_ref, vv_scr)):
        for h in range(N_HEADS):
            sl = slice(h * HEAD_DIM, (h + 1) * HEAD_DIM)
            out_ref[pl.ds(h, t, stride=N_HEADS), :] = new_ref[:, sl]
            scr[0:past, sl] = cache_ref[pl.ds(h, past, stride=N_HEADS), :].astype(BF16)
        scr[past:past + t, :] = new_ref[...].astype(BF16)
        scr[past + t:, :] = jnp.zeros((SAMPLE_KEYS_PAD - past - t, ATTN_WIDTH), BF16)
    for h in range(N_HEADS):
        sl = slice(h * HEAD_DIM, (h + 1) * HEAD_DIM)
        s = _dot_t(q_ref[:, sl], kk_scr[:, sl]) + bias_ref[h]
        m = jnp.max(s, axis=-1, keepdims=True)
        e = jnp.exp2(s - m)
        l = jnp.sum(e, axis=-1, keepdims=True)
        o = _dot(e.astype(BF16), vv_scr[:, sl])
        o_ref[:, sl] = (o / l).astype(BF16)


def _attn_sample(q, kf, vf, cache_k, cache_v, bias, k_upd, v_upd, *, layer, n, t):
    depth, _, past = cache_k.shape[:3]
    rows = past * N_HEADS
    new_rows = t * N_HEADS
    row_spec = pl.BlockSpec((t, ATTN_WIDTH), lambda b: (b, 0))
    cache_spec = pl.BlockSpec((None, None, rows, HEAD_DIM), lambda b: (layer, b, 0, 0))
    new_spec = pl.BlockSpec((None, None, new_rows, HEAD_DIM), lambda b: (layer, b, rows // new_rows - 1, 0))
    bias_spec = pl.BlockSpec((N_HEADS, t, SAMPLE_KEYS_PAD), lambda b: (layer, 0, 0),
                             pipeline_mode=pl.Buffered(1))
    any_spec = pl.BlockSpec(memory_space=pl.ANY)
    upd = jax.ShapeDtypeStruct(k_upd.shape, F32)
    return pl.pallas_call(
        functools.partial(_attn_sample_kernel, t=t, past=past),
        grid=(n,),
        in_specs=[row_spec, row_spec, row_spec, cache_spec, cache_spec, bias_spec, any_spec, any_spec],
        out_specs=(row_spec, new_spec, new_spec),
        out_shape=(jax.ShapeDtypeStruct((n * t, ATTN_WIDTH), BF16), upd, upd),
        input_output_aliases={6: 1, 7: 2},
        scratch_shapes=[pltpu.VMEM((SAMPLE_KEYS_PAD, ATTN_WIDTH), BF16),
                        pltpu.VMEM((SAMPLE_KEYS_PAD, ATTN_WIDTH), BF16)],
        compiler_params=_params(("arbitrary",)),
        name="attn_sample",
    )(q, kf, vf, cache_k.reshape(depth, n, rows, HEAD_DIM), cache_v.reshape(depth, n, rows, HEAD_DIM),
      bias, k_upd, v_upd)


def _merge_kernel(x_ref, p_ref, hist_ref, a_ref, gate_ref, pw_ref, ps_ref, wa_ref, wb_ref, wo_ref,
                  o_ref, xp_scr, d_scr, *, nseq, t, blocks_per_seq, pos0):
    i = pl.program_id(0)
    row = lax.broadcasted_iota(jnp.int32, (t, 1), 0)
    if blocks_per_seq is None:
        pos = pos0 + row
    else:
        pos = (i % blocks_per_seq) * t + row
    for s in range(nseq):
        rows = slice(s * t, (s + 1) * t)
        hist = hist_ref[s * HIST_ROWS:(s + 1) * HIST_ROWS, :]
        if blocks_per_seq is not None:
            hist = jnp.where(i % blocks_per_seq == 0, 0.0, hist)
        xp_scr[0:HIST_ROWS, :] = hist
        xp_scr[HIST_ROWS:, :] = p_ref[rows, :]
        for gi, w in enumerate(POOL_WINDOWS):
            cols = slice(gi * POOL_GROUP, (gi + 1) * POOL_GROUP)
            acc = xp_scr[HIST_ROWS:HIST_ROWS + t, cols]
            for sh in range(1, w):
                acc = acc + xp_scr[HIST_ROWS - sh:HIST_ROWS - sh + t, cols]
            cnt = jnp.minimum(pos + 1, w).astype(F32)
            d_scr[rows, cols] = (acc / cnt - p_ref[rows, cols]).astype(BF16)
    pooled = jnp.concatenate(
        [_dot(d_scr[:, gi * POOL_GROUP:(gi + 1) * POOL_GROUP], pw_ref[gi]) for gi in range(len(POOL_WINDOWS))],
        axis=-1) * ps_ref[...]
    branch_a = _dot(pooled.astype(BF16), wa_ref[...])
    branch_b = _dot(a_ref[...], wb_ref[...])
    merged = gate_ref[:, :D_MODEL] * branch_a + gate_ref[:, D_MODEL:] * branch_b
    o_ref[...] = x_ref[...] + _dot(merged.astype(BF16), wo_ref[...])


def _merge(x, p, hist, attn, gates, pool_w, pool_scale, wa, wb, wo, *, layer, tm, nseq, t, blocks_per_seq, pos0,
           hist_spec):
    m = x.shape[0]

    def row_spec(width):
        return pl.BlockSpec((tm, width), lambda i: (i, 0))

    return pl.pallas_call(
        functools.partial(_merge_kernel, nseq=nseq, t=t, blocks_per_seq=blocks_per_seq, pos0=pos0),
        grid=(m // tm,),
        in_specs=[row_spec(D_MODEL), row_spec(POOL_WIDTH), hist_spec, row_spec(ATTN_WIDTH),
                  row_spec(2 * D_MODEL),
                  _layer_spec(pool_w.shape[1:], layer), _layer_spec(pool_scale.shape[1:], layer),
                  _layer_spec(wa.shape[1:], layer), _layer_spec(wb.shape[1:], layer),
                  _layer_spec(wo.shape[1:], layer)],
        out_specs=row_spec(D_MODEL),
        out_shape=jax.ShapeDtypeStruct((m, D_MODEL), F32),
        scratch_shapes=[pltpu.VMEM((HIST_ROWS + t, POOL_WIDTH), F32),
                        pltpu.VMEM((tm, POOL_WIDTH), BF16)],
        compiler_params=_params(("arbitrary",)),
        name="merge",
    )(x, p, hist, attn, gates, pool_w, pool_scale, wa, wb, wo)


FFN_TF = 512


def _ffn_kernel(x_ref, xs_ref, g2_ref, wu_ref, wd_ref, o_ref, os_ref, h_scr, hs_scr):
    i = pl.program_id(0)
    j = pl.program_id(1)

    def start(src_ref, acc_ref, scr):
        x = src_ref[...]
        scr[...] = _rms(x, g2_ref[...]).astype(BF16)
        acc_ref[...] = x

    def step(acc_ref, scr):
        u = jnp.square(jnp.maximum(_dot(scr[...], wu_ref[...]), 0.0))
        acc_ref[...] += _dot(u.astype(BF16), wd_ref[...])

    pl.when(j == 0)(functools.partial(start, x_ref, o_ref, h_scr))
    pl.when((i == 0) & (j == 0))(functools.partial(start, xs_ref, os_ref, hs_scr))
    step(o_ref, h_scr)
    pl.when(i == 0)(functools.partial(step, os_ref, hs_scr))


def _ffn(x, xs, g2, wu, wd, *, layer, tm):
    m = x.shape[0]
    ms = xs.shape[0]
    sample_spec = pl.BlockSpec((ms, D_MODEL), lambda i, j: (0, 0))
    return pl.pallas_call(
        _ffn_kernel,
        grid=(m // tm, D_FF // FFN_TF),
        in_specs=[pl.BlockSpec((tm, D_MODEL), lambda i, j: (i, 0)),
                  sample_spec,
                  pl.BlockSpec((None, 1, D_MODEL), lambda i, j: (layer, 0, 0)),
                  pl.BlockSpec((None, D_MODEL, FFN_TF), lambda i, j: (layer, 0, j)),
                  pl.BlockSpec((None, FFN_TF, D_MODEL), lambda i, j: (layer, j, 0))],
        out_specs=(pl.BlockSpec((tm, D_MODEL), lambda i, j: (i, 0)), sample_spec),
        out_shape=(jax.ShapeDtypeStruct((m, D_MODEL), F32), jax.ShapeDtypeStruct((ms, D_MODEL), F32)),
        scratch_shapes=[pltpu.VMEM((tm, D_MODEL), BF16), pltpu.VMEM((ms, D_MODEL), BF16)],
        compiler_params=_params(("arbitrary", "arbitrary")),
        name="ffn",
    )(x, xs, g2, wu, wd)


PROMPT_TM = 1024
MERGE_TM = 256


def kernel(x_prompt, x_sample, state_pool, cache_k, cache_v, norm1_g, w_in, b_gate, pool_w, pool_scale,
           q_norm_g, k_norm_g, rel_bias, w_branch_a, w_branch_b, w_out, norm2_g, w_up, w_down):
    nb, seq, _ = x_prompt.shape
    ns, t, _ = x_sample.shape
    depth = w_in.shape[0]
    past = cache_k.shape[2]
    assert seq % PROMPT_TM == 0 and seq % MERGE_TM == 0 and seq % Q_BLOCK == 0 and seq >= BAND_PAST
    assert t >= POOL_HIST and t % HIST_ROWS == 0 and past + t <= SAMPLE_KEYS_PAD and past <= BAND_PAST
    assert past % t == 0 and past > t

    xp = x_prompt.reshape(nb * seq, D_MODEL)
    xs = x_sample.reshape(ns * t, D_MODEL)
    blocks_per_seq = seq // MERGE_TM
    prompt_hist_spec = pl.BlockSpec(
        (HIST_ROWS, POOL_WIDTH), lambda i: (jnp.maximum(i * (MERGE_TM // HIST_ROWS) - 1, 0), 0))
    sample_hist_spec = pl.BlockSpec((ns * HIST_ROWS, POOL_WIDTH), lambda i: (0, 0))

    g1 = norm1_g.reshape(depth, 1, D_MODEL)
    g2 = norm2_g.reshape(depth, 1, D_MODEL)
    bg = b_gate.reshape(depth, 1, 2 * D_MODEL)
    qg = q_norm_g.reshape(depth, 1, HEAD_DIM)
    kg = k_norm_g.reshape(depth, 1, HEAD_DIM)
    ps = pool_scale.reshape(depth, 1, POOL_WIDTH)
    w_mixed, w_gate = _regroup_in_weights(w_in)
    pw = pool_w.astype(BF16)
    wa = w_branch_a.astype(BF16)
    wb = w_branch_b.astype(BF16)
    wo = w_out.astype(BF16)
    wu = w_up.astype(BF16)
    wd = w_down.astype(BF16)
    hist_s = jnp.pad(state_pool, ((0, 0), (0, 0), (HIST_ROWS - POOL_HIST, 0), (0, 0)))
    hist_s = hist_s.reshape(depth, ns * HIST_ROWS, POOL_WIDTH)

    bias_p = _bias_table(rel_bias, q_rows=Q_BLOCK, keys=KEY_BLOCKS * Q_BLOCK,
                         offset=(KEY_BLOCKS - 1) * Q_BLOCK, visible_fn=_prompt_visible)
    bias_s = _bias_table(rel_bias, q_rows=t, keys=SAMPLE_KEYS_PAD, offset=past,
                         visible_fn=lambda qi, kj: kj < past + t)

    k_upd, v_upd = _shift_caches(cache_k, cache_v, t=t)

    pool_p, kp_l, vp_l, pool_s = [], [], [], []
    for l in range(depth):
        (p, q, k, v, kf, vf, gates,
         p_s, q_s, kf_s, vf_s, gates_s) = _in_proj(xp, xs, g1, w_mixed, w_gate, bg, qg, kg, layer=l, tm=PROMPT_TM)

        attn = _attn_prompt(q, k, v, bias_p, layer=l, n=nb, s=seq)
        pool_p.append(p.reshape(nb, seq, POOL_WIDTH)[:, seq - POOL_HIST:])
        kp_l.append(kf.reshape(nb, seq, N_HEADS, HEAD_DIM)[:, seq - BAND_PAST:])
        vp_l.append(vf.reshape(nb, seq, N_HEADS, HEAD_DIM)[:, seq - BAND_PAST:])
        xp = _merge(xp, p, p, attn, gates, pw, ps, wa, wb, wo, layer=l, tm=MERGE_TM, nseq=1, t=MERGE_TM,
                    blocks_per_seq=blocks_per_seq, pos0=None, hist_spec=prompt_hist_spec)

        attn_s, k_upd, v_upd = _attn_sample(q_s, kf_s, vf_s, cache_k, cache_v, bias_s, k_upd, v_upd,
                                            layer=l, n=ns, t=t)
        pool_s.append(p_s.reshape(ns, t, POOL_WIDTH)[:, t - POOL_HIST:])
        xs = _merge(xs, p_s, hist_s[l], attn_s, gates_s, pw, ps, wa, wb, wo, layer=l, tm=ns * t, nseq=ns, t=t,
                    blocks_per_seq=None, pos0=PAST_LEN, hist_spec=sample_hist_spec)

        xp, xs = _ffn(xp, xs, g2, wu, wd, layer=l, tm=PROMPT_TM)

    return (xp.reshape(nb, seq, D_MODEL), xs.reshape(ns, t, D_MODEL),
            jnp.stack(pool_p), jnp.stack(kp_l), jnp.stack(vp_l),
            jnp.stack(pool_s), k_upd.reshape(cache_k.shape), v_upd.reshape(cache_v.shape))
```

```python
import functools

import numpy as np

import jax
import jax.numpy as jnp
from jax import lax
from jax.experimental import pallas as pl
from jax.experimental.pallas import tpu as pltpu

D_MODEL = 2048
PAST_LEN = 1024
CHUNK = 64
N_PREV_CHUNKS = 8
BAND_PAST = N_PREV_CHUNKS * CHUNK
POOL_WIDTH = D_MODEL // 2
POOL_WINDOWS = (2, 4, 8, 16)
POOL_GROUP = POOL_WIDTH // len(POOL_WINDOWS)
POOL_HIST = max(POOL_WINDOWS) - 1
HEAD_DIM = 128
ATTN_WIDTH = D_MODEL // 2
N_HEADS = ATTN_WIDTH // HEAD_DIM
MAX_REL = 128
D_FF = 4 * D_MODEL
EPS = 1e-6
ATTN_SCALE = HEAD_DIM ** -0.5
LOG2E = 1.4426950408889634
Q_SCALE = ATTN_SCALE * LOG2E
NEG_INF = -1e30
IN_WIDTH = POOL_WIDTH + 3 * ATTN_WIDTH + 2 * D_MODEL

BF16 = jnp.bfloat16
F32 = jnp.float32

VMEM_LIMIT_BYTES = 60 * 1024 * 1024
LANES = 128

HIST_ROWS = 16
Q_BLOCK = 4 * CHUNK
KEY_BLOCKS = 3
SAMPLE_KEYS_PAD = 640


def _rms(xf, g):
    ms = jnp.mean(xf * xf, axis=-1, keepdims=True)
    return xf * lax.rsqrt(ms + EPS) * g


def _dot(a, b):
    return jnp.dot(a, b, preferred_element_type=F32)


def _dot_t(a, b):
    return lax.dot_general(a, b, (((1,), (1,)), ((), ())), preferred_element_type=F32)


def _params(semantics):
    return pltpu.CompilerParams(dimension_semantics=semantics, vmem_limit_bytes=VMEM_LIMIT_BYTES)


def _layer_spec(shape, layer):
    nd = len(shape)
    return pl.BlockSpec((None,) + tuple(shape), lambda *_: (layer,) + (0,) * nd,
                        pipeline_mode=pl.Buffered(1))


def _bias_table_kernel(ext_ref, o_ref, *, q_rows, keys, visible_fn):
    width = ext_ref.shape[-1]
    qi = lax.broadcasted_iota(jnp.int32, (q_rows, keys), 0)
    kj = lax.broadcasted_iota(jnp.int32, (q_rows, keys), 1)
    visible = visible_fn(qi, kj)
    for g in range(ext_ref.shape[0]):
        row = jnp.broadcast_to(ext_ref[g:g + 1, :], (q_rows, width))
        skew = pltpu.roll(row, 0, 1, stride=1, stride_axis=0)
        o_ref[g] = jnp.where(visible, skew[:, :keys] * LOG2E, NEG_INF)


def _bias_table(rel_bias, *, q_rows, keys, offset, visible_fn):
    groups = rel_bias.shape[0] * rel_bias.shape[1]
    width = -(-(keys + q_rows - 1) // LANES) * LANES
    idx = np.zeros((width,), np.int32)
    idx[:keys] = np.clip(offset - np.arange(keys), -MAX_REL, MAX_REL) + MAX_REL
    back = np.arange(1, q_rows)
    idx[width - back] = np.clip(offset + back, -MAX_REL, MAX_REL) + MAX_REL
    ext = rel_bias.reshape(groups, 2 * MAX_REL + 1)[:, idx].astype(F32)
    return pl.pallas_call(
        functools.partial(_bias_table_kernel, q_rows=q_rows, keys=keys, visible_fn=visible_fn),
        out_shape=jax.ShapeDtypeStruct((groups, q_rows, keys), F32),
        compiler_params=pltpu.CompilerParams(vmem_limit_bytes=VMEM_LIMIT_BYTES),
        name="bias_table",
    )(ext)


def _prompt_visible(qi, kj):
    own = qi // CHUNK + (KEY_BLOCKS - 1) * (Q_BLOCK // CHUNK)
    kc = kj // CHUNK
    return (kc <= own) & (kc >= own - N_PREV_CHUNKS)


IN_STEPS = N_HEADS
IN_GATE_TN = 2 * D_MODEL // IN_STEPS


def _regroup_kernel(wp_ref, wq_ref, wk_ref, wv_ref, wg_ref, mixed_ref, gate_ref):
    for s, src in enumerate((wp_ref, wq_ref, wk_ref, wv_ref)):
        mixed_ref[:, s * HEAD_DIM:(s + 1) * HEAD_DIM] = src[...].astype(BF16)
    gate_ref[...] = wg_ref[...].astype(BF16)


def _regroup_in_weights(w_in):
    depth = w_in.shape[0]

    def head_cols(seg):
        return pl.BlockSpec((None, D_MODEL, HEAD_DIM), lambda l, j: (l, 0, seg * N_HEADS + j))

    gate0 = 4 * ATTN_WIDTH // IN_GATE_TN
    return pl.pallas_call(
        _regroup_kernel,
        grid=(depth, IN_STEPS),
        in_specs=[head_cols(0), head_cols(1), head_cols(2), head_cols(3),
                  pl.BlockSpec((None, D_MODEL, IN_GATE_TN), lambda l, j: (l, 0, gate0 + j))],
        out_specs=(pl.BlockSpec((None, D_MODEL, 4 * HEAD_DIM), lambda l, j: (l, 0, j)),
                   pl.BlockSpec((None, D_MODEL, IN_GATE_TN), lambda l, j: (l, 0, j))),
        out_shape=(jax.ShapeDtypeStruct((depth, D_MODEL, 4 * ATTN_WIDTH), BF16),
                   jax.ShapeDtypeStruct((depth, D_MODEL, 2 * D_MODEL), BF16)),
        compiler_params=_params(("parallel", "parallel")),
        name="regroup_w_in",
    )(w_in, w_in, w_in, w_in, w_in)


def _in_rows(h, wm_ref, wg_ref, bg_ref, qg_ref, kg_ref, p_ref, q_ref, kf_ref, vf_ref, gate_ref,
             k_ref=None, v_ref=None):
    z = _dot(h, wm_ref[...])
    p_ref[...] = z[:, 0:HEAD_DIM]
    q_ref[...] = (_rms(z[:, HEAD_DIM:2 * HEAD_DIM], qg_ref[...]) * Q_SCALE).astype(BF16)
    kn = _rms(z[:, 2 * HEAD_DIM:3 * HEAD_DIM], kg_ref[...])
    v = z[:, 3 * HEAD_DIM:4 * HEAD_DIM]
    kf_ref[...] = kn
    vf_ref[...] = v
    if k_ref is not None:
        k_ref[...] = kn.astype(BF16)
        v_ref[...] = v.astype(BF16)
    gate_ref[...] = jax.nn.sigmoid(_dot(h, wg_ref[...]) + bg_ref[...])


def _in_kernel(x_ref, xs_ref, g1_ref, wm_ref, wg_ref, bg_ref, qg_ref, kg_ref,
               p_ref, q_ref, k_ref, v_ref, kf_ref, vf_ref, gate_ref,
               ps_ref, qs_ref, kfs_ref, vfs_ref, gates_ref, h_scr, hs_scr):
    i = pl.program_id(0)
    j = pl.program_id(1)
    weights = (wm_ref, wg_ref, bg_ref, qg_ref, kg_ref)

    @pl.when(j == 0)
    def _():
        h_scr[...] = _rms(x_ref[...], g1_ref[...]).astype(BF16)

    @pl.when((i == 0) & (j == 0))
    def _():
        hs_scr[...] = _rms(xs_ref[...], g1_ref[...]).astype(BF16)

    _in_rows(h_scr[...], *weights, p_ref, q_ref, kf_ref, vf_ref, gate_ref, k_ref, v_ref)

    @pl.when(i == 0)
    def _():
        _in_rows(hs_scr[...], *weights, ps_ref, qs_ref, kfs_ref, vfs_ref, gates_ref)


def _in_proj(x, xs, g1, w_mixed, w_gate, b_gate, qg, kg, *, layer, tm):
    m = x.shape[0]
    ms = xs.shape[0]
    head_spec = pl.BlockSpec((tm, HEAD_DIM), lambda i, j: (i, j))

    def sample_spec(width):
        return pl.BlockSpec((ms, width), lambda i, j: (0, jnp.where(i == 0, j, IN_STEPS - 1)))

    out_shape = (
        jax.ShapeDtypeStruct((m, POOL_WIDTH), F32),
        jax.ShapeDtypeStruct((m, ATTN_WIDTH), BF16),
        jax.ShapeDtypeStruct((m, ATTN_WIDTH), BF16),
        jax.ShapeDtypeStruct((m, ATTN_WIDTH), BF16),
        jax.ShapeDtypeStruct((m, ATTN_WIDTH), F32),
        jax.ShapeDtypeStruct((m, ATTN_WIDTH), F32),
        jax.ShapeDtypeStruct((m, 2 * D_MODEL), F32),
        jax.ShapeDtypeStruct((ms, POOL_WIDTH), F32),
        jax.ShapeDtypeStruct((ms, ATTN_WIDTH), BF16),
        jax.ShapeDtypeStruct((ms, ATTN_WIDTH), F32),
        jax.ShapeDtypeStruct((ms, ATTN_WIDTH), F32),
        jax.ShapeDtypeStruct((ms, 2 * D_MODEL), F32),
    )
    return pl.pallas_call(
        _in_kernel,
        grid=(m // tm, IN_STEPS),
        in_specs=[
            pl.BlockSpec((tm, D_MODEL), lambda i, j: (i, 0)),
            pl.BlockSpec((ms, D_MODEL), lambda i, j: (0, 0)),
            pl.BlockSpec((None, 1, D_MODEL), lambda i, j: (layer, 0, 0)),
            pl.BlockSpec((None, D_MODEL, 4 * HEAD_DIM), lambda i, j: (layer, 0, j)),
            pl.BlockSpec((None, D_MODEL, IN_GATE_TN), lambda i, j: (layer, 0, j)),
            pl.BlockSpec((None, 1, IN_GATE_TN), lambda i, j: (layer, 0, j)),
            pl.BlockSpec((None, 1, HEAD_DIM), lambda i, j: (layer, 0, 0)),
            pl.BlockSpec((None, 1, HEAD_DIM), lambda i, j: (layer, 0, 0)),
        ],
        out_specs=(head_spec,) * 6 + (pl.BlockSpec((tm, IN_GATE_TN), lambda i, j: (i, j)),)
                  + (sample_spec(HEAD_DIM),) * 4 + (sample_spec(IN_GATE_TN),),
        out_shape=out_shape,
        scratch_shapes=[pltpu.VMEM((tm, D_MODEL), BF16), pltpu.VMEM((ms, D_MODEL), BF16)],
        compiler_params=_params(("arbitrary", "arbitrary")),
        name="in_proj",
    )(x, xs, g1, w_mixed, w_gate, b_gate, qg, kg)


def _attn_prompt_kernel(q_ref, k0_ref, k1_ref, k2_ref, v0_ref, v1_ref, v2_ref, bias_ref, o_ref):
    i = pl.program_id(1)
    k_refs = (k0_ref, k1_ref, k2_ref)
    v_refs = (v0_ref, v1_ref, v2_ref)

    def heads(n_valid):
        blocks = range(KEY_BLOCKS - n_valid, KEY_BLOCKS)
        for h in range(N_HEADS):
            sl = slice(h * HEAD_DIM, (h + 1) * HEAD_DIM)
            qh = q_ref[:, sl]
            m = l = o = None
            for d in blocks:
                s = _dot_t(qh, k_refs[d][:, sl]) + bias_ref[h, :, d * Q_BLOCK:(d + 1) * Q_BLOCK]
                m_d = jnp.max(s, axis=-1, keepdims=True)
                if m is None:
                    m = m_d
                    e = jnp.exp2(s - m)
                    l = jnp.sum(e, axis=-1, keepdims=True)
                    o = _dot(e.astype(BF16), v_refs[d][:, sl])
                else:
                    m_new = jnp.maximum(m, m_d)
                    alpha = jnp.exp2(m - m_new)
                    e = jnp.exp2(s - m_new)
                    l = alpha * l + jnp.sum(e, axis=-1, keepdims=True)
                    o = alpha * o + _dot(e.astype(BF16), v_refs[d][:, sl])
                    m = m_new
            o_ref[:, sl] = (o / l).astype(BF16)

    for n_valid in range(1, KEY_BLOCKS):
        pl.when(i == n_valid - 1)(functools.partial(heads, n_valid))
    pl.when(i >= KEY_BLOCKS - 1)(functools.partial(heads, KEY_BLOCKS))


def _attn_prompt(q, k, v, bias, *, layer, n, s):
    nb = s // Q_BLOCK
    q_spec = pl.BlockSpec((Q_BLOCK, ATTN_WIDTH), lambda b, i: (b * nb + i, 0))

    def kv_spec(d):
        return pl.BlockSpec((Q_BLOCK, ATTN_WIDTH),
                            lambda b, i: (b * nb + jnp.maximum(i - (KEY_BLOCKS - 1) + d, 0), 0))

    bias_spec = pl.BlockSpec((N_HEADS, Q_BLOCK, KEY_BLOCKS * Q_BLOCK), lambda b, i: (layer, 0, 0),
                             pipeline_mode=pl.Buffered(1))
    return pl.pallas_call(
        _attn_prompt_kernel,
        grid=(n, nb),
        in_specs=[q_spec] + [kv_spec(d) for d in range(KEY_BLOCKS)] * 2 + [bias_spec],
        out_specs=q_spec,
        out_shape=jax.ShapeDtypeStruct((n * s, ATTN_WIDTH), BF16),
        compiler_params=_params(("parallel", "arbitrary")),
        name="attn_prompt",
    )(q, k, k, k, v, v, v, bias)


def _shift_cache_kernel(ck_ref, cv_ref, ko_ref, vo_ref, *, new_rows):
    for src, dst in ((ck_ref, ko_ref), (cv_ref, vo_ref)):
        keep = src.shape[0] - new_rows
        dst[0:keep, :] = src[new_rows:, :]
        dst[keep:, :] = jnp.zeros((new_rows, HEAD_DIM), F32)


def _shift_caches(cache_k, cache_v, *, t):
    depth, n, past = cache_k.shape[:3]
    rows = past * N_HEADS
    spec = pl.BlockSpec((None, None, rows, HEAD_DIM), lambda l, b: (l, b, 0, 0))
    shape = jax.ShapeDtypeStruct((depth, n, rows, HEAD_DIM), F32)
    return pl.pallas_call(
        functools.partial(_shift_cache_kernel, new_rows=t * N_HEADS),
        grid=(depth, n),
        in_specs=[spec, spec],
        out_specs=(spec, spec),
        out_shape=(shape, shape),
        compiler_params=_params(("parallel", "parallel")),
        name="shift_caches",
    )(cache_k.reshape(depth, n, rows, HEAD_DIM), cache_v.reshape(depth, n, rows, HEAD_DIM))


def _attn_sample_kernel(q_ref, kf_ref, vf_ref, ck_ref, cv_ref, bias_ref, k_alias, v_alias,
                        o_ref, ko_ref, vo_ref, kk_scr, vv_scr, *, t, past):
    del k_alias, v_alias
    for new_ref, cache_ref, out_ref, scr in ((kf_ref, ck_ref, ko_ref, kk_scr),
                                             (vf_ref, cv_ref, vo_ref, vv_scr)):
        for h in range(N_HEADS):
            sl = slice(h * HEAD_DIM, (h + 1) * HEAD_DIM)
            out_ref[pl.ds(h, t, stride=N_HEADS), :] = new_ref[:, sl]
            scr[0:past, sl] = cache_ref[pl.ds(h, past, stride=N_HEADS), :].astype(BF16)
        scr[past:past + t, :] = new_ref[...].astype(BF16)
        scr[past + t:, :] = jnp.zeros((SAMPLE_KEYS_PAD - past - t, ATTN_WIDTH), BF16)
    for h in range(N_HEADS):
        sl = slice(h * HEAD_DIM, (h + 1) * HEAD_DIM)
        s = _dot_t(q_ref[:, sl], kk_scr[:, sl]) + bias_ref[h]
        m = jnp.max(s, axis=-1, keepdims=True)
        e = jnp.exp2(s - m)
        l = jnp.sum(e, axis=-1, keepdims=True)
        o = _dot(e.astype(BF16), vv_scr[:, sl])
        o_ref[:, sl] = (o / l).astype(BF16)


def _attn_sample(q, kf, vf, cache_k, cache_v, bias, k_upd, v_upd, *, layer, n, t):
    depth, _, past = cache_k.shape[:3]
    rows = past * N_HEADS
    new_rows = t * N_HEADS
    row_spec = pl.BlockSpec((t, ATTN_WIDTH), lambda b: (b, 0))
    cache_spec = pl.BlockSpec((None, None, rows, HEAD_DIM), lambda b: (layer, b, 0, 0))
    new_spec = pl.BlockSpec((None, None, new_rows, HEAD_DIM), lambda b: (layer, b, rows // new_rows - 1, 0))
    bias_spec = pl.BlockSpec((N_HEADS, t, SAMPLE_KEYS_PAD), lambda b: (layer, 0, 0),
                             pipeline_mode=pl.Buffered(1))
    any_spec = pl.BlockSpec(memory_space=pl.ANY)
    upd = jax.ShapeDtypeStruct(k_upd.shape, F32)
    return pl.pallas_call(
        functools.partial(_attn_sample_kernel, t=t, past=past),
        grid=(n,),
        in_specs=[row_spec, row_spec, row_spec, cache_spec, cache_spec, bias_spec, any_spec, any_spec],
        out_specs=(row_spec, new_spec, new_spec),
        out_shape=(jax.ShapeDtypeStruct((n * t, ATTN_WIDTH), BF16), upd, upd),
        input_output_aliases={6: 1, 7: 2},
        scratch_shapes=[pltpu.VMEM((SAMPLE_KEYS_PAD, ATTN_WIDTH), BF16),
                        pltpu.VMEM((SAMPLE_KEYS_PAD, ATTN_WIDTH), BF16)],
        compiler_params=_params(("arbitrary",)),
        name="attn_sample",
    )(q, kf, vf, cache_k.reshape(depth, n, rows, HEAD_DIM), cache_v.reshape(depth, n, rows, HEAD_DIM),
      bias, k_upd, v_upd)


def _merge_kernel(x_ref, p_ref, hist_ref, a_ref, gate_ref, pw_ref, ps_ref, wa_ref, wb_ref, wo_ref,
                  o_ref, xp_scr, d_scr, *, nseq, t, blocks_per_seq, pos0):
    i = pl.program_id(0)
    row = lax.broadcasted_iota(jnp.int32, (t, 1), 0)
    if blocks_per_seq is None:
        pos = pos0 + row
    else:
        pos = (i % blocks_per_seq) * t + row
    for s in range(nseq):
        rows = slice(s * t, (s + 1) * t)
        hist = hist_ref[s * HIST_ROWS:(s + 1) * HIST_ROWS, :]
        if blocks_per_seq is not None:
            hist = jnp.where(i % blocks_per_seq == 0, 0.0, hist)
        xp_scr[0:HIST_ROWS, :] = hist
        xp_scr[HIST_ROWS:, :] = p_ref[rows, :]
        for gi, w in enumerate(POOL_WINDOWS):
            cols = slice(gi * POOL_GROUP, (gi + 1) * POOL_GROUP)
            acc = xp_scr[HIST_ROWS:HIST_ROWS + t, cols]
            for sh in range(1, w):
                acc = acc + xp_scr[HIST_ROWS - sh:HIST_ROWS - sh + t, cols]
            cnt = jnp.minimum(pos + 1, w).astype(F32)
            d_scr[rows, cols] = (acc / cnt - p_ref[rows, cols]).astype(BF16)
    pooled = jnp.concatenate(
        [_dot(d_scr[:, gi * POOL_GROUP:(gi + 1) * POOL_GROUP], pw_ref[gi]) for gi in range(len(POOL_WINDOWS))],
        axis=-1) * ps_ref[...]
    branch_a = _dot(pooled.astype(BF16), wa_ref[...])
    branch_b = _dot(a_ref[...], wb_ref[...])
    merged = gate_ref[:, :D_MODEL] * branch_a + gate_ref[:, D_MODEL:] * branch_b
    o_ref[...] = x_ref[...] + _dot(merged.astype(BF16), wo_ref[...])


def _merge(x, p, hist, attn, gates, pool_w, pool_scale, wa, wb, wo, *, layer, tm, nseq, t, blocks_per_seq, pos0,
           hist_spec):
    m = x.shape[0]

    def row_spec(width):
        return pl.BlockSpec((tm, width), lambda i: (i, 0))

    return pl.pallas_call(
        functools.partial(_merge_kernel, nseq=nseq, t=t, blocks_per_seq=blocks_per_seq, pos0=pos0),
        grid=(m // tm,),
        in_specs=[row_spec(D_MODEL), row_spec(POOL_WIDTH), hist_spec, row_spec(ATTN_WIDTH),
                  row_spec(2 * D_MODEL),
                  _layer_spec(pool_w.shape[1:], layer), _layer_spec(pool_scale.shape[1:], layer),
                  _layer_spec(wa.shape[1:], layer), _layer_spec(wb.shape[1:], layer),
                  _layer_spec(wo.shape[1:], layer)],
        out_specs=row_spec(D_MODEL),
        out_shape=jax.ShapeDtypeStruct((m, D_MODEL), F32),
        scratch_shapes=[pltpu.VMEM((HIST_ROWS + t, POOL_WIDTH), F32),
                        pltpu.VMEM((tm, POOL_WIDTH), BF16)],
        compiler_params=_params(("arbitrary",)),
        name="merge",
    )(x, p, hist, attn, gates, pool_w, pool_scale, wa, wb, wo)


FFN_TF = 512


def _ffn_kernel(x_ref, xs_ref, g2_ref, wu_ref, wd_ref, o_ref, os_ref, h_scr, hs_scr):
    i = pl.program_id(0)
    j = pl.program_id(1)

    def start(src_ref, acc_ref, scr):
        x = src_ref[...]
        scr[...] = _rms(x, g2_ref[...]).astype(BF16)
        acc_ref[...] = x

    def step(acc_ref, scr):
        u = jnp.square(jnp.maximum(_dot(scr[...], wu_ref[...].astype(BF16)), 0.0))
        acc_ref[...] += _dot(u.astype(BF16), wd_ref[...].astype(BF16))

    pl.when(j == 0)(functools.partial(start, x_ref, o_ref, h_scr))
    pl.when((i == 0) & (j == 0))(functools.partial(start, xs_ref, os_ref, hs_scr))
    step(o_ref, h_scr)
    pl.when(i == 0)(functools.partial(step, os_ref, hs_scr))


def _ffn(x, xs, g2, wu, wd, *, layer, tm):
    m = x.shape[0]
    ms = xs.shape[0]
    sample_spec = pl.BlockSpec((ms, D_MODEL), lambda i, j: (0, 0))
    return pl.pallas_call(
        _ffn_kernel,
        grid=(m // tm, D_FF // FFN_TF),
        in_specs=[pl.BlockSpec((tm, D_MODEL), lambda i, j: (i, 0)),
                  sample_spec,
                  pl.BlockSpec((None, 1, D_MODEL), lambda i, j: (layer, 0, 0)),
                  pl.BlockSpec((None, D_MODEL, FFN_TF), lambda i, j: (layer, 0, j)),
                  pl.BlockSpec((None, FFN_TF, D_MODEL), lambda i, j: (layer, j, 0))],
        out_specs=(pl.BlockSpec((tm, D_MODEL), lambda i, j: (i, 0)), sample_spec),
        out_shape=(jax.ShapeDtypeStruct((m, D_MODEL), F32), jax.ShapeDtypeStruct((ms, D_MODEL), F32)),
        scratch_shapes=[pltpu.VMEM((tm, D_MODEL), BF16), pltpu.VMEM((ms, D_MODEL), BF16)],
        compiler_params=_params(("arbitrary", "arbitrary")),
        name="ffn",
    )(x, xs, g2, wu, wd)


PROMPT_TM = 1024
MERGE_TM = 256


def kernel(x_prompt, x_sample, state_pool, cache_k, cache_v, norm1_g, w_in, b_gate, pool_w, pool_scale,
           q_norm_g, k_norm_g, rel_bias, w_branch_a, w_branch_b, w_out, norm2_g, w_up, w_down):
    nb, seq, _ = x_prompt.shape
    ns, t, _ = x_sample.shape
    depth = w_in.shape[0]
    past = cache_k.shape[2]
    assert seq % PROMPT_TM == 0 and seq % MERGE_TM == 0 and seq % Q_BLOCK == 0 and seq >= BAND_PAST
    assert t >= POOL_HIST and t % HIST_ROWS == 0 and past + t <= SAMPLE_KEYS_PAD and past <= BAND_PAST
    assert past % t == 0 and past > t

    xp = x_prompt.reshape(nb * seq, D_MODEL)
    xs = x_sample.reshape(ns * t, D_MODEL)
    blocks_per_seq = seq // MERGE_TM
    prompt_hist_spec = pl.BlockSpec(
        (HIST_ROWS, POOL_WIDTH), lambda i: (jnp.maximum(i * (MERGE_TM // HIST_ROWS) - 1, 0), 0))
    sample_hist_spec = pl.BlockSpec((ns * HIST_ROWS, POOL_WIDTH), lambda i: (0, 0))

    g1 = norm1_g.reshape(depth, 1, D_MODEL)
    g2 = norm2_g.reshape(depth, 1, D_MODEL)
    bg = b_gate.reshape(depth, 1, 2 * D_MODEL)
    qg = q_norm_g.reshape(depth, 1, HEAD_DIM)
    kg = k_norm_g.reshape(depth, 1, HEAD_DIM)
    ps = pool_scale.reshape(depth, 1, POOL_WIDTH)
    w_mixed, w_gate = _regroup_in_weights(w_in)
    pw = pool_w.astype(BF16)
    wa = w_branch_a.astype(BF16)
    wb = w_branch_b.astype(BF16)
    wo = w_out.astype(BF16)
    hist_s = jnp.pad(state_pool, ((0, 0), (0, 0), (HIST_ROWS - POOL_HIST, 0), (0, 0)))
    hist_s = hist_s.reshape(depth, ns * HIST_ROWS, POOL_WIDTH)

    bias_p = _bias_table(rel_bias, q_rows=Q_BLOCK, keys=KEY_BLOCKS * Q_BLOCK,
                         offset=(KEY_BLOCKS - 1) * Q_BLOCK, visible_fn=_prompt_visible)
    bias_s = _bias_table(rel_bias, q_rows=t, keys=SAMPLE_KEYS_PAD, offset=past,
                         visible_fn=lambda qi, kj: kj < past + t)

    k_upd, v_upd = _shift_caches(cache_k, cache_v, t=t)

    pool_p, kp_l, vp_l, pool_s = [], [], [], []
    for l in range(depth):
        (p, q, k, v, kf, vf, gates,
         p_s, q_s, kf_s, vf_s, gates_s) = _in_proj(xp, xs, g1, w_mixed, w_gate, bg, qg, kg, layer=l, tm=PROMPT_TM)

        attn = _attn_prompt(q, k, v, bias_p, layer=l, n=nb, s=seq)
        pool_p.append(p.reshape(nb, seq, POOL_WIDTH)[:, seq - POOL_HIST:])
        kp_l.append(kf.reshape(nb, seq, N_HEADS, HEAD_DIM)[:, seq - BAND_PAST:])
        vp_l.append(vf.reshape(nb, seq, N_HEADS, HEAD_DIM)[:, seq - BAND_PAST:])
        xp = _merge(xp, p, p, attn, gates, pw, ps, wa, wb, wo, layer=l, tm=MERGE_TM, nseq=1, t=MERGE_TM,
                    blocks_per_seq=blocks_per_seq, pos0=None, hist_spec=prompt_hist_spec)

        attn_s, k_upd, v_upd = _attn_sample(q_s, kf_s, vf_s, cache_k, cache_v, bias_s, k_upd, v_upd,
                                            layer=l, n=ns, t=t)
        pool_s.append(p_s.reshape(ns, t, POOL_WIDTH)[:, t - POOL_HIST:])
        xs = _merge(xs, p_s, hist_s[l], attn_s, gates_s, pw, ps, wa, wb, wo, layer=l, tm=ns * t, nseq=ns, t=t,
                    blocks_per_seq=None, pos0=PAST_LEN, hist_spec=sample_hist_spec)

        xp, xs = _ffn(xp, xs, g2, w_up, w_down, layer=l, tm=PROMPT_TM)

    return (xp.reshape(nb, seq, D_MODEL), xs.reshape(ns, t, D_MODEL),
            jnp.stack(pool_p), jnp.stack(kp_l), jnp.stack(vp_l),
            jnp.stack(pool_s), k_upd.reshape(cache_k.shape), v_upd.reshape(cache_v.shape))
```

```python
import functools

import numpy as np

import jax
import jax.numpy as jnp
from jax import lax
from jax.experimental import pallas as pl
from jax.experimental.pallas import tpu as pltpu

D_MODEL = 2048
PAST_LEN = 1024
CHUNK = 64
N_PREV_CHUNKS = 8
BAND_PAST = N_PREV_CHUNKS * CHUNK
POOL_WIDTH = D_MODEL // 2
POOL_WINDOWS = (2, 4, 8, 16)
POOL_GROUP = POOL_WIDTH // len(POOL_WINDOWS)
POOL_HIST = max(POOL_WINDOWS) - 1
HEAD_DIM = 128
ATTN_WIDTH = D_MODEL // 2
N_HEADS = ATTN_WIDTH // HEAD_DIM
MAX_REL = 128
D_FF = 4 * D_MODEL
EPS = 1e-6
ATTN_SCALE = HEAD_DIM ** -0.5
LOG2E = 1.4426950408889634
Q_SCALE = ATTN_SCALE * LOG2E
NEG_INF = -1e30
IN_WIDTH = POOL_WIDTH + 3 * ATTN_WIDTH + 2 * D_MODEL

BF16 = jnp.bfloat16
F32 = jnp.float32

VMEM_LIMIT_BYTES = 60 * 1024 * 1024
LANES = 128

HIST_ROWS = 16
POOL_PAD = 8
Q_BLOCK = 4 * CHUNK
KEY_BLOCKS = 3
SAMPLE_KEYS_PAD = 640


def _rms(xf, g):
    ms = jnp.mean(xf * xf, axis=-1, keepdims=True)
    return xf * lax.rsqrt(ms + EPS) * g


def _dot(a, b):
    return jnp.dot(a, b, preferred_element_type=F32)


def _dot_t(a, b):
    return lax.dot_general(a, b, (((1,), (1,)), ((), ())), preferred_element_type=F32)


def _params(semantics):
    return pltpu.CompilerParams(dimension_semantics=semantics, vmem_limit_bytes=VMEM_LIMIT_BYTES)


def _layer_spec(shape, layer):
    nd = len(shape)
    return pl.BlockSpec((None,) + tuple(shape), lambda *_: (layer,) + (0,) * nd,
                        pipeline_mode=pl.Buffered(1))


def _bias_table_kernel(ext_ref, o_ref, *, q_rows, keys, visible_fn):
    width = ext_ref.shape[-1]
    qi = lax.broadcasted_iota(jnp.int32, (q_rows, keys), 0)
    kj = lax.broadcasted_iota(jnp.int32, (q_rows, keys), 1)
    visible = visible_fn(qi, kj)
    for g in range(ext_ref.shape[0]):
        row = jnp.broadcast_to(ext_ref[g:g + 1, :], (q_rows, width))
        skew = pltpu.roll(row, 0, 1, stride=1, stride_axis=0)
        o_ref[g] = jnp.where(visible, skew[:, :keys] * LOG2E, NEG_INF)


def _bias_table(rel_bias, *, q_rows, keys, offset, visible_fn):
    groups = rel_bias.shape[0] * rel_bias.shape[1]
    width = -(-(keys + q_rows - 1) // LANES) * LANES
    idx = np.zeros((width,), np.int32)
    idx[:keys] = np.clip(offset - np.arange(keys), -MAX_REL, MAX_REL) + MAX_REL
    back = np.arange(1, q_rows)
    idx[width - back] = np.clip(offset + back, -MAX_REL, MAX_REL) + MAX_REL
    ext = rel_bias.reshape(groups, 2 * MAX_REL + 1)[:, idx].astype(F32)
    return pl.pallas_call(
        functools.partial(_bias_table_kernel, q_rows=q_rows, keys=keys, visible_fn=visible_fn),
        out_shape=jax.ShapeDtypeStruct((groups, q_rows, keys), F32),
        compiler_params=pltpu.CompilerParams(vmem_limit_bytes=VMEM_LIMIT_BYTES),
        name="bias_table",
    )(ext)


def _prompt_visible(qi, kj):
    own = qi // CHUNK + (KEY_BLOCKS - 1) * (Q_BLOCK // CHUNK)
    kc = kj // CHUNK
    return (kc <= own) & (kc >= own - N_PREV_CHUNKS)


IN_STEPS = N_HEADS
IN_GATE_TN = 2 * D_MODEL // IN_STEPS


def _regroup_kernel(wq_ref, wk_ref, wp_ref, wv_ref, wg_ref, mixed_ref, gate_ref):
    for s, src in enumerate((wq_ref, wk_ref, wp_ref, wv_ref)):
        mixed_ref[:, s * HEAD_DIM:(s + 1) * HEAD_DIM] = src[...].astype(BF16)
    gate_ref[...] = wg_ref[...].astype(BF16)


def _regroup_in_weights(w_in):
    depth = w_in.shape[0]

    def head_cols(seg):
        return pl.BlockSpec((None, D_MODEL, HEAD_DIM), lambda l, j: (l, 0, seg * N_HEADS + j))

    gate0 = 4 * ATTN_WIDTH // IN_GATE_TN
    return pl.pallas_call(
        _regroup_kernel,
        grid=(depth, IN_STEPS),
        in_specs=[head_cols(1), head_cols(2), head_cols(0), head_cols(3),
                  pl.BlockSpec((None, D_MODEL, IN_GATE_TN), lambda l, j: (l, 0, gate0 + j))],
        out_specs=(pl.BlockSpec((None, D_MODEL, 4 * HEAD_DIM), lambda l, j: (l, 0, j)),
                   pl.BlockSpec((None, D_MODEL, IN_GATE_TN), lambda l, j: (l, 0, j))),
        out_shape=(jax.ShapeDtypeStruct((depth, D_MODEL, 4 * ATTN_WIDTH), BF16),
                   jax.ShapeDtypeStruct((depth, D_MODEL, 2 * D_MODEL), BF16)),
        compiler_params=_params(("parallel", "parallel")),
        name="regroup_w_in",
    )(w_in, w_in, w_in, w_in, w_in)


def _in_rows(h, wm_ref, wg_ref, bg_ref, qg_ref, kg_ref, p_ref, q_ref, kf_ref, vf_ref, gate_ref,
             k_ref=None, v_ref=None):
    gate_ref[...] = jax.nn.sigmoid(_dot(h, wg_ref[...]) + bg_ref[...])
    z = _dot(h, wm_ref[...])
    q_ref[...] = (_rms(z[:, 0:HEAD_DIM], qg_ref[...]) * Q_SCALE).astype(BF16)
    kn = _rms(z[:, HEAD_DIM:2 * HEAD_DIM], kg_ref[...])
    p_ref[...] = z[:, 2 * HEAD_DIM:3 * HEAD_DIM]
    v = z[:, 3 * HEAD_DIM:4 * HEAD_DIM]
    kf_ref[...] = kn
    vf_ref[...] = v
    if k_ref is not None:
        k_ref[...] = kn.astype(BF16)
        v_ref[...] = v.astype(BF16)


def _in_kernel(x_ref, xs_ref, g1_ref, wm_ref, wg_ref, bg_ref, qg_ref, kg_ref,
               p_ref, q_ref, k_ref, v_ref, kf_ref, vf_ref, gate_ref,
               ps_ref, qs_ref, kfs_ref, vfs_ref, gates_ref, h_scr, hs_scr):
    i = pl.program_id(0)
    j = pl.program_id(1)
    weights = (wm_ref, wg_ref, bg_ref, qg_ref, kg_ref)

    @pl.when(j == 0)
    def _():
        h_scr[...] = _rms(x_ref[...], g1_ref[...]).astype(BF16)

    @pl.when((i == 0) & (j == 0))
    def _():
        hs_scr[...] = _rms(xs_ref[...], g1_ref[...]).astype(BF16)

    _in_rows(h_scr[...], *weights, p_ref, q_ref, kf_ref, vf_ref, gate_ref, k_ref, v_ref)

    @pl.when(i == 0)
    def _():
        _in_rows(hs_scr[...], *weights, ps_ref, qs_ref, kfs_ref, vfs_ref, gates_ref)


def _in_proj(x, xs, g1, w_mixed, w_gate, b_gate, qg, kg, *, layer, tm):
    m = x.shape[0]
    ms = xs.shape[0]
    head_spec = pl.BlockSpec((tm, HEAD_DIM), lambda i, j: (i, j))

    def sample_spec(width):
        return pl.BlockSpec((ms, width), lambda i, j: (0, jnp.where(i == 0, j, IN_STEPS - 1)))

    out_shape = (
        jax.ShapeDtypeStruct((m, POOL_WIDTH), F32),
        jax.ShapeDtypeStruct((m, ATTN_WIDTH), BF16),
        jax.ShapeDtypeStruct((m, ATTN_WIDTH), BF16),
        jax.ShapeDtypeStruct((m, ATTN_WIDTH), BF16),
        jax.ShapeDtypeStruct((m, ATTN_WIDTH), F32),
        jax.ShapeDtypeStruct((m, ATTN_WIDTH), F32),
        jax.ShapeDtypeStruct((m, 2 * D_MODEL), F32),
        jax.ShapeDtypeStruct((ms, POOL_WIDTH), F32),
        jax.ShapeDtypeStruct((ms, ATTN_WIDTH), BF16),
        jax.ShapeDtypeStruct((ms, ATTN_WIDTH), F32),
        jax.ShapeDtypeStruct((ms, ATTN_WIDTH), F32),
        jax.ShapeDtypeStruct((ms, 2 * D_MODEL), F32),
    )
    return pl.pallas_call(
        _in_kernel,
        grid=(m // tm, IN_STEPS),
        in_specs=[
            pl.BlockSpec((tm, D_MODEL), lambda i, j: (i, 0)),
            pl.BlockSpec((ms, D_MODEL), lambda i, j: (0, 0)),
            pl.BlockSpec((None, 1, D_MODEL), lambda i, j: (layer, 0, 0)),
            pl.BlockSpec((None, D_MODEL, 4 * HEAD_DIM), lambda i, j: (layer, 0, j)),
            pl.BlockSpec((None, D_MODEL, IN_GATE_TN), lambda i, j: (layer, 0, j)),
            pl.BlockSpec((None, 1, IN_GATE_TN), lambda i, j: (layer, 0, j)),
            pl.BlockSpec((None, 1, HEAD_DIM), lambda i, j: (layer, 0, 0)),
            pl.BlockSpec((None, 1, HEAD_DIM), lambda i, j: (layer, 0, 0)),
        ],
        out_specs=(head_spec,) * 6 + (pl.BlockSpec((tm, IN_GATE_TN), lambda i, j: (i, j)),)
                  + (sample_spec(HEAD_DIM),) * 4 + (sample_spec(IN_GATE_TN),),
        out_shape=out_shape,
        scratch_shapes=[pltpu.VMEM((tm, D_MODEL), BF16), pltpu.VMEM((ms, D_MODEL), BF16)],
        compiler_params=_params(("arbitrary", "arbitrary")),
        name="in_proj",
    )(x, xs, g1, w_mixed, w_gate, b_gate, qg, kg)


def _attn_prompt_kernel(q_ref, k0_ref, k1_ref, k2_ref, v0_ref, v1_ref, v2_ref, bias_ref, o_ref):
    i = pl.program_id(1)
    k_refs = (k0_ref, k1_ref, k2_ref)
    v_refs = (v0_ref, v1_ref, v2_ref)

    def heads(n_valid):
        blocks = range(KEY_BLOCKS - n_valid, KEY_BLOCKS)
        for h in range(N_HEADS):
            sl = slice(h * HEAD_DIM, (h + 1) * HEAD_DIM)
            qh = q_ref[:, sl]
            m = l = o = None
            for d in blocks:
                s = _dot_t(qh, k_refs[d][:, sl]) + bias_ref[h, :, d * Q_BLOCK:(d + 1) * Q_BLOCK]
                m_d = jnp.max(s, axis=-1, keepdims=True)
                if m is None:
                    m = m_d
                    e = jnp.exp2(s - m)
                    l = jnp.sum(e, axis=-1, keepdims=True)
                    o = _dot(e.astype(BF16), v_refs[d][:, sl])
                else:
                    m_new = jnp.maximum(m, m_d)
                    alpha = jnp.exp2(m - m_new)
                    e = jnp.exp2(s - m_new)
                    l = alpha * l + jnp.sum(e, axis=-1, keepdims=True)
                    o = alpha * o + _dot(e.astype(BF16), v_refs[d][:, sl])
                    m = m_new
            o_ref[:, sl] = (o / l).astype(BF16)

    for n_valid in range(1, KEY_BLOCKS):
        pl.when(i == n_valid - 1)(functools.partial(heads, n_valid))
    pl.when(i >= KEY_BLOCKS - 1)(functools.partial(heads, KEY_BLOCKS))


def _attn_prompt(q, k, v, bias, *, layer, n, s):
    nb = s // Q_BLOCK
    q_spec = pl.BlockSpec((Q_BLOCK, ATTN_WIDTH), lambda b, i: (b * nb + i, 0))

    def kv_spec(d):
        return pl.BlockSpec((Q_BLOCK, ATTN_WIDTH),
                            lambda b, i: (b * nb + jnp.maximum(i - (KEY_BLOCKS - 1) + d, 0), 0))

    bias_spec = pl.BlockSpec((N_HEADS, Q_BLOCK, KEY_BLOCKS * Q_BLOCK), lambda b, i: (layer, 0, 0),
                             pipeline_mode=pl.Buffered(1))
    return pl.pallas_call(
        _attn_prompt_kernel,
        grid=(n, nb),
        in_specs=[q_spec] + [kv_spec(d) for d in range(KEY_BLOCKS)] * 2 + [bias_spec],
        out_specs=q_spec,
        out_shape=jax.ShapeDtypeStruct((n * s, ATTN_WIDTH), BF16),
        compiler_params=_params(("parallel", "arbitrary")),
        name="attn_prompt",
    )(q, k, k, k, v, v, v, bias)


def _shift_cache_kernel(ck_ref, cv_ref, ko_ref, vo_ref, *, new_rows):
    for src, dst in ((ck_ref, ko_ref), (cv_ref, vo_ref)):
        keep = src.shape[0] - new_rows
        dst[0:keep, :] = src[new_rows:, :]
        dst[keep:, :] = jnp.zeros((new_rows, HEAD_DIM), F32)


def _shift_caches(cache_k, cache_v, *, t):
    depth, n, past = cache_k.shape[:3]
    rows = past * N_HEADS
    spec = pl.BlockSpec((None, None, rows, HEAD_DIM), lambda l, b: (l, b, 0, 0))
    shape = jax.ShapeDtypeStruct((depth, n, rows, HEAD_DIM), F32)
    return pl.pallas_call(
        functools.partial(_shift_cache_kernel, new_rows=t * N_HEADS),
        grid=(depth, n),
        in_specs=[spec, spec],
        out_specs=(spec, spec),
        out_shape=(shape, shape),
        compiler_params=_params(("parallel", "parallel")),
        name="shift_caches",
    )(cache_k.reshape(depth, n, rows, HEAD_DIM), cache_v.reshape(depth, n, rows, HEAD_DIM))


def _attn_sample_kernel(q_ref, kf_ref, vf_ref, ck_ref, cv_ref, bias_ref, k_alias, v_alias,
                        o_ref, ko_ref, vo_ref, kk_scr, vv_scr, *, t, past):
    del k_alias, v_alias
    for new_ref, cache_ref, out_ref, scr in ((kf_ref, ck_ref, ko_ref, kk_scr),
                                             (vf_ref, cv_ref, vo_ref, vv_scr)):
        for h in range(N_HEADS):
            sl = slice(h * HEAD_DIM, (h + 1) * HEAD_DIM)
            out_ref[pl.ds(h, t, stride=N_HEADS), :] = new_ref[:, sl]
            scr[0:past, sl] = cache_ref[pl.ds(h, past, stride=N_HEADS), :].astype(BF16)
        scr[past:past + t, :] = new_ref[...].astype(BF16)
        scr[past + t:, :] = jnp.zeros((SAMPLE_KEYS_PAD - past - t, ATTN_WIDTH), BF16)
    for h in range(N_HEADS):
        sl = slice(h * HEAD_DIM, (h + 1) * HEAD_DIM)
        s = _dot_t(q_ref[:, sl], kk_scr[:, sl]) + bias_ref[h]
        m = jnp.max(s, axis=-1, keepdims=True)
        e = jnp.exp2(s - m)
        l = jnp.sum(e, axis=-1, keepdims=True)
        o = _dot(e.astype(BF16), vv_scr[:, sl])
        o_ref[:, sl] = (o / l).astype(BF16)


def _attn_sample(q, kf, vf, cache_k, cache_v, bias, k_upd, v_upd, *, layer, n, t):
    depth, _, past = cache_k.shape[:3]
    rows = past * N_HEADS
    new_rows = t * N_HEADS
    row_spec = pl.BlockSpec((t, ATTN_WIDTH), lambda b: (b, 0))
    cache_spec = pl.BlockSpec((None, None, rows, HEAD_DIM), lambda b: (layer, b, 0, 0))
    new_spec = pl.BlockSpec((None, None, new_rows, HEAD_DIM), lambda b: (layer, b, rows // new_rows - 1, 0))
    bias_spec = pl.BlockSpec((N_HEADS, t, SAMPLE_KEYS_PAD), lambda b: (layer, 0, 0),
                             pipeline_mode=pl.Buffered(1))
    any_spec = pl.BlockSpec(memory_space=pl.ANY)
    upd = jax.ShapeDtypeStruct(k_upd.shape, F32)
    return pl.pallas_call(
        functools.partial(_attn_sample_kernel, t=t, past=past),
        grid=(n,),
        in_specs=[row_spec, row_spec, row_spec, cache_spec, cache_spec, bias_spec, any_spec, any_spec],
        out_specs=(row_spec, new_spec, new_spec),
        out_shape=(jax.ShapeDtypeStruct((n * t, ATTN_WIDTH), BF16), upd, upd),
        input_output_aliases={6: 1, 7: 2},
        scratch_shapes=[pltpu.VMEM((SAMPLE_KEYS_PAD, ATTN_WIDTH), BF16),
                        pltpu.VMEM((SAMPLE_KEYS_PAD, ATTN_WIDTH), BF16)],
        compiler_params=_params(("arbitrary",)),
        name="attn_sample",
    )(q, kf, vf, cache_k.reshape(depth, n, rows, HEAD_DIM), cache_v.reshape(depth, n, rows, HEAD_DIM),
      bias, k_upd, v_upd)


def _merge_kernel(x_ref, p_ref, hist_ref, a_ref, gate_ref, pw_ref, ps_ref, wa_ref, wb_ref, wo_ref,
                  o_ref, xp_scr, stage_scr, d_scr, *, nseq, t, blocks_per_seq, pos0):
    i = pl.program_id(0)
    row = lax.broadcasted_iota(jnp.int32, (t, 1), 0)
    if blocks_per_seq is None:
        pos = pos0 + row
    else:
        pos = (i % blocks_per_seq) * t + row
    gated_b = gate_ref[:, D_MODEL:] * _dot(a_ref[...], wb_ref[...])

    t0 = POOL_PAD + HIST_ROWS
    xp_scr[0:POOL_PAD, :] = jnp.zeros((POOL_PAD, POOL_WIDTH), F32)
    stage_scr[:, 0:POOL_PAD, :] = jnp.zeros((2, POOL_PAD, POOL_GROUP), F32)
    for s in range(nseq):
        rows = slice(s * t, (s + 1) * t)
        hist = hist_ref[s * HIST_ROWS:(s + 1) * HIST_ROWS, :]
        if blocks_per_seq is not None:
            hist = jnp.where(i % blocks_per_seq == 0, 0.0, hist)
        xp_scr[POOL_PAD:t0, :] = hist
        xp_scr[t0:, :] = p_ref[rows, :]
        for gi, w in enumerate(POOL_WINDOWS):
            cols = slice(gi * POOL_GROUP, (gi + 1) * POOL_GROUP)
            src = xp_scr.at[:, cols]
            shift = 1
            while 2 * shift < w:
                dst = stage_scr.at[(shift.bit_length() - 1) % 2]
                dst[POOL_PAD:, :] = src[POOL_PAD:, :] + src[POOL_PAD - shift:t0 + t - shift, :]
                src = dst
                shift *= 2
            acc = src[t0:, :] + src[t0 - shift:t0 + t - shift, :]
            inv_cnt = 1.0 / jnp.minimum(pos + 1, w).astype(F32)
            d_scr[rows, cols] = (acc * inv_cnt - p_ref[rows, cols]).astype(BF16)
    pooled = jnp.concatenate(
        [_dot(d_scr[:, gi * POOL_GROUP:(gi + 1) * POOL_GROUP], pw_ref[gi]) for gi in range(len(POOL_WINDOWS))],
        axis=-1) * ps_ref[...]
    branch_a = _dot(pooled.astype(BF16), wa_ref[...])
    merged = gate_ref[:, :D_MODEL] * branch_a + gated_b
    o_ref[...] = x_ref[...] + _dot(merged.astype(BF16), wo_ref[...])


def _merge(x, p, hist, attn, gates, pool_w, pool_scale, wa, wb, wo, *, layer, tm, nseq, t, blocks_per_seq, pos0,
           hist_spec):
    m = x.shape[0]

    def row_spec(width):
        return pl.BlockSpec((tm, width), lambda i: (i, 0))

    return pl.pallas_call(
        functools.partial(_merge_kernel, nseq=nseq, t=t, blocks_per_seq=blocks_per_seq, pos0=pos0),
        grid=(m // tm,),
        in_specs=[row_spec(D_MODEL), row_spec(POOL_WIDTH), hist_spec, row_spec(ATTN_WIDTH),
                  row_spec(2 * D_MODEL),
                  _layer_spec(pool_w.shape[1:], layer), _layer_spec(pool_scale.shape[1:], layer),
                  _layer_spec(wa.shape[1:], layer), _layer_spec(wb.shape[1:], layer),
                  _layer_spec(wo.shape[1:], layer)],
        out_specs=row_spec(D_MODEL),
        out_shape=jax.ShapeDtypeStruct((m, D_MODEL), F32),
        scratch_shapes=[pltpu.VMEM((POOL_PAD + HIST_ROWS + t, POOL_WIDTH), F32),
                        pltpu.VMEM((2, POOL_PAD + HIST_ROWS + t, POOL_GROUP), F32),
                        pltpu.VMEM((tm, POOL_WIDTH), BF16)],
        compiler_params=_params(("arbitrary",)),
        name="merge",
    )(x, p, hist, attn, gates, pool_w, pool_scale, wa, wb, wo)


FFN_TF = 512


def _ffn_kernel(x_ref, xs_ref, g2_ref, wu_ref, wd_ref, o_ref, os_ref, h_scr, hs_scr):
    i = pl.program_id(0)
    j = pl.program_id(1)

    def start(src_ref, acc_ref, scr):
        x = src_ref[...]
        scr[...] = _rms(x, g2_ref[...]).astype(BF16)
        acc_ref[...] = x

    def step(acc_ref, scr):
        u = jnp.square(jnp.maximum(_dot(scr[...], wu_ref[...].astype(BF16)), 0.0))
        acc_ref[...] += _dot(u.astype(BF16), wd_ref[...].astype(BF16))

    pl.when(j == 0)(functools.partial(start, x_ref, o_ref, h_scr))
    pl.when((i == 0) & (j == 0))(functools.partial(start, xs_ref, os_ref, hs_scr))
    step(o_ref, h_scr)
    pl.when(i == 0)(functools.partial(step, os_ref, hs_scr))


def _ffn(x, xs, g2, wu, wd, *, layer, tm):
    m = x.shape[0]
    ms = xs.shape[0]
    sample_spec = pl.BlockSpec((ms, D_MODEL), lambda i, j: (0, 0))
    return pl.pallas_call(
        _ffn_kernel,
        grid=(m // tm, D_FF // FFN_TF),
        in_specs=[pl.BlockSpec((tm, D_MODEL), lambda i, j: (i, 0)),
                  sample_spec,
                  pl.BlockSpec((None, 1, D_MODEL), lambda i, j: (layer, 0, 0)),
                  pl.BlockSpec((None, D_MODEL, FFN_TF), lambda i, j: (layer, 0, j)),
                  pl.BlockSpec((None, FFN_TF, D_MODEL), lambda i, j: (layer, j, 0))],
        out_specs=(pl.BlockSpec((tm, D_MODEL), lambda i, j: (i, 0)), sample_spec),
        out_shape=(jax.ShapeDtypeStruct((m, D_MODEL), F32), jax.ShapeDtypeStruct((ms, D_MODEL), F32)),
        scratch_shapes=[pltpu.VMEM((tm, D_MODEL), BF16), pltpu.VMEM((ms, D_MODEL), BF16)],
        compiler_params=_params(("arbitrary", "arbitrary")),
        name="ffn",
    )(x, xs, g2, wu, wd)


PROMPT_TM = 1024
MERGE_TM = 256


def kernel(x_prompt, x_sample, state_pool, cache_k, cache_v, norm1_g, w_in, b_gate, pool_w, pool_scale,
           q_norm_g, k_norm_g, rel_bias, w_branch_a, w_branch_b, w_out, norm2_g, w_up, w_down):
    nb, seq, _ = x_prompt.shape
    ns, t, _ = x_sample.shape
    depth = w_in.shape[0]
    past = cache_k.shape[2]
    assert seq % PROMPT_TM == 0 and seq % MERGE_TM == 0 and seq % Q_BLOCK == 0 and seq >= BAND_PAST
    assert t >= POOL_HIST and t % HIST_ROWS == 0 and past + t <= SAMPLE_KEYS_PAD and past <= BAND_PAST
    assert past % t == 0 and past > t

    xp = x_prompt.reshape(nb * seq, D_MODEL)
    xs = x_sample.reshape(ns * t, D_MODEL)
    blocks_per_seq = seq // MERGE_TM
    prompt_hist_spec = pl.BlockSpec(
        (HIST_ROWS, POOL_WIDTH), lambda i: (jnp.maximum(i * (MERGE_TM // HIST_ROWS) - 1, 0), 0))
    sample_hist_spec = pl.BlockSpec((ns * HIST_ROWS, POOL_WIDTH), lambda i: (0, 0))

    g1 = norm1_g.reshape(depth, 1, D_MODEL)
    g2 = norm2_g.reshape(depth, 1, D_MODEL)
    bg = b_gate.reshape(depth, 1, 2 * D_MODEL)
    qg = q_norm_g.reshape(depth, 1, HEAD_DIM)
    kg = k_norm_g.reshape(depth, 1, HEAD_DIM)
    ps = pool_scale.reshape(depth, 1, POOL_WIDTH)
    w_mixed, w_gate = _regroup_in_weights(w_in)
    pw = pool_w.astype(BF16)
    wa = w_branch_a.astype(BF16)
    wb = w_branch_b.astype(BF16)
    wo = w_out.astype(BF16)
    hist_s = jnp.pad(state_pool, ((0, 0), (0, 0), (HIST_ROWS - POOL_HIST, 0), (0, 0)))
    hist_s = hist_s.reshape(depth, ns * HIST_ROWS, POOL_WIDTH)

    bias_p = _bias_table(rel_bias, q_rows=Q_BLOCK, keys=KEY_BLOCKS * Q_BLOCK,
                         offset=(KEY_BLOCKS - 1) * Q_BLOCK, visible_fn=_prompt_visible)
    bias_s = _bias_table(rel_bias, q_rows=t, keys=SAMPLE_KEYS_PAD, offset=past,
                         visible_fn=lambda qi, kj: kj < past + t)

    k_upd, v_upd = _shift_caches(cache_k, cache_v, t=t)

    pool_p, kp_l, vp_l, pool_s = [], [], [], []
    for l in range(depth):
        (p, q, k, v, kf, vf, gates,
         p_s, q_s, kf_s, vf_s, gates_s) = _in_proj(xp, xs, g1, w_mixed, w_gate, bg, qg, kg, layer=l, tm=PROMPT_TM)

        attn = _attn_prompt(q, k, v, bias_p, layer=l, n=nb, s=seq)
        pool_p.append(p.reshape(nb, seq, POOL_WIDTH)[:, seq - POOL_HIST:])
        kp_l.append(kf.reshape(nb, seq, N_HEADS, HEAD_DIM)[:, seq - BAND_PAST:])
        vp_l.append(vf.reshape(nb, seq, N_HEADS, HEAD_DIM)[:, seq - BAND_PAST:])
        xp = _merge(xp, p, p, attn, gates, pw, ps, wa, wb, wo, layer=l, tm=MERGE_TM, nseq=1, t=MERGE_TM,
                    blocks_per_seq=blocks_per_seq, pos0=None, hist_spec=prompt_hist_spec)

        attn_s, k_upd, v_upd = _attn_sample(q_s, kf_s, vf_s, cache_k, cache_v, bias_s, k_upd, v_upd,
                                            layer=l, n=ns, t=t)
        pool_s.append(p_s.reshape(ns, t, POOL_WIDTH)[:, t - POOL_HIST:])
        xs = _merge(xs, p_s, hist_s[l], attn_s, gates_s, pw, ps, wa, wb, wo, layer=l, tm=ns * t, nseq=ns, t=t,
                    blocks_per_seq=None, pos0=PAST_LEN, hist_spec=sample_hist_spec)

        xp, xs = _ffn(xp, xs, g2, w_up, w_down, layer=l, tm=PROMPT_TM)

    return (xp.reshape(nb, seq, D_MODEL), xs.reshape(ns, t, D_MODEL),
            jnp.stack(pool_p), jnp.stack(kp_l), jnp.stack(vp_l),
            jnp.stack(pool_s), k_upd.reshape(cache_k.shape), v_upd.reshape(cache_v.shape))
```

```python
import functools

import numpy as np

import jax
import jax.numpy as jnp
from jax import lax
from jax.experimental import pallas as pl
from jax.experimental.pallas import tpu as pltpu

D_MODEL = 2048
PAST_LEN = 1024
CHUNK = 64
N_PREV_CHUNKS = 8
BAND_PAST = N_PREV_CHUNKS * CHUNK
POOL_WIDTH = D_MODEL // 2
POOL_WINDOWS = (2, 4, 8, 16)
POOL_GROUP = POOL_WIDTH // len(POOL_WINDOWS)
POOL_HIST = max(POOL_WINDOWS) - 1
HEAD_DIM = 128
ATTN_WIDTH = D_MODEL // 2
N_HEADS = ATTN_WIDTH // HEAD_DIM
MAX_REL = 128
D_FF = 4 * D_MODEL
EPS = 1e-6
ATTN_SCALE = HEAD_DIM ** -0.5
LOG2E = 1.4426950408889634
Q_SCALE = ATTN_SCALE * LOG2E
NEG_INF = -1e30
IN_WIDTH = POOL_WIDTH + 3 * ATTN_WIDTH + 2 * D_MODEL

BF16 = jnp.bfloat16
F32 = jnp.float32

VMEM_LIMIT_BYTES = 63 * 1024 * 1024
LANES = 128

HIST_ROWS = 16
POOL_PAD = 8
Q_BLOCK = 4 * CHUNK
KEY_BLOCKS = 3
SAMPLE_KEYS_PAD = 640


def _rms(xf, g):
    ms = jnp.mean(xf * xf, axis=-1, keepdims=True)
    return xf * lax.rsqrt(ms + EPS) * g


def _dot(a, b):
    return jnp.dot(a, b, preferred_element_type=F32)


def _dot_t(a, b):
    return lax.dot_general(a, b, (((1,), (1,)), ((), ())), preferred_element_type=F32)


def _params(semantics):
    return pltpu.CompilerParams(dimension_semantics=semantics, vmem_limit_bytes=VMEM_LIMIT_BYTES)


def _layer_spec(shape, layer):
    nd = len(shape)
    return pl.BlockSpec((None,) + tuple(shape), lambda *_: (layer,) + (0,) * nd,
                        pipeline_mode=pl.Buffered(1))


def _bias_table_kernel(ext_ref, o_ref, *, q_rows, keys, visible_fn):
    width = ext_ref.shape[-1]
    qi = lax.broadcasted_iota(jnp.int32, (q_rows, keys), 0)
    kj = lax.broadcasted_iota(jnp.int32, (q_rows, keys), 1)
    visible = visible_fn(qi, kj)
    for g in range(ext_ref.shape[0]):
        row = jnp.broadcast_to(ext_ref[g:g + 1, :], (q_rows, width))
        skew = pltpu.roll(row, 0, 1, stride=1, stride_axis=0)
        o_ref[g] = jnp.where(visible, skew[:, :keys] * LOG2E, NEG_INF)


def _bias_table(rel_bias, *, q_rows, keys, offset, visible_fn):
    groups = rel_bias.shape[0] * rel_bias.shape[1]
    width = -(-(keys + q_rows - 1) // LANES) * LANES
    idx = np.zeros((width,), np.int32)
    idx[:keys] = np.clip(offset - np.arange(keys), -MAX_REL, MAX_REL) + MAX_REL
    back = np.arange(1, q_rows)
    idx[width - back] = np.clip(offset + back, -MAX_REL, MAX_REL) + MAX_REL
    ext = rel_bias.reshape(groups, 2 * MAX_REL + 1)[:, idx].astype(F32)
    return pl.pallas_call(
        functools.partial(_bias_table_kernel, q_rows=q_rows, keys=keys, visible_fn=visible_fn),
        out_shape=jax.ShapeDtypeStruct((groups, q_rows, keys), F32),
        compiler_params=pltpu.CompilerParams(vmem_limit_bytes=VMEM_LIMIT_BYTES),
        name="bias_table",
    )(ext)


def _prompt_visible(qi, kj):
    own = qi // CHUNK + (KEY_BLOCKS - 1) * (Q_BLOCK // CHUNK)
    kc = kj // CHUNK
    return (kc <= own) & (kc >= own - N_PREV_CHUNKS)


IN_HEADS = 2
IN_STEPS = N_HEADS // IN_HEADS
IN_SEG_TN = IN_HEADS * HEAD_DIM
IN_GATE_TN = 2 * D_MODEL // IN_STEPS


def _regroup_kernel(wq_ref, wk_ref, wp_ref, wv_ref, wg_ref, mixed_ref, gate_ref):
    for s, src in enumerate((wq_ref, wk_ref, wp_ref, wv_ref)):
        mixed_ref[:, s * IN_SEG_TN:(s + 1) * IN_SEG_TN] = src[...].astype(BF16)
    gate_ref[...] = wg_ref[...].astype(BF16)


def _regroup_in_weights(w_in):
    depth = w_in.shape[0]

    def head_cols(seg):
        return pl.BlockSpec((None, D_MODEL, IN_SEG_TN), lambda l, j: (l, 0, seg * IN_STEPS + j))

    gate0 = 4 * ATTN_WIDTH // IN_GATE_TN
    return pl.pallas_call(
        _regroup_kernel,
        grid=(depth, IN_STEPS),
        in_specs=[head_cols(1), head_cols(2), head_cols(0), head_cols(3),
                  pl.BlockSpec((None, D_MODEL, IN_GATE_TN), lambda l, j: (l, 0, gate0 + j))],
        out_specs=(pl.BlockSpec((None, D_MODEL, 4 * IN_SEG_TN), lambda l, j: (l, 0, j)),
                   pl.BlockSpec((None, D_MODEL, IN_GATE_TN), lambda l, j: (l, 0, j))),
        out_shape=(jax.ShapeDtypeStruct((depth, D_MODEL, 4 * ATTN_WIDTH), BF16),
                   jax.ShapeDtypeStruct((depth, D_MODEL, 2 * D_MODEL), BF16)),
        compiler_params=_params(("parallel", "parallel")),
        name="regroup_w_in",
    )(w_in, w_in, w_in, w_in, w_in)


def _in_rows(h, wm_ref, wg_ref, bg_ref, qg_ref, kg_ref, p_ref, q_ref, kf_ref, vf_ref, gate_ref,
             k_ref=None, v_ref=None):
    gate_ref[...] = jax.nn.sigmoid(_dot(h, wg_ref[...]) + bg_ref[...])
    z = _dot(h, wm_ref[...])
    for hd in range(IN_HEADS):
        sl = slice(hd * HEAD_DIM, (hd + 1) * HEAD_DIM)
        q_ref[:, sl] = (_rms(z[:, sl], qg_ref[...]) * Q_SCALE).astype(BF16)
        kn = _rms(z[:, IN_SEG_TN + hd * HEAD_DIM:IN_SEG_TN + (hd + 1) * HEAD_DIM], kg_ref[...])
        kf_ref[:, sl] = kn
        if k_ref is not None:
            k_ref[:, sl] = kn.astype(BF16)
    p_ref[...] = z[:, 2 * IN_SEG_TN:3 * IN_SEG_TN]
    v = z[:, 3 * IN_SEG_TN:4 * IN_SEG_TN]
    vf_ref[...] = v
    if v_ref is not None:
        v_ref[...] = v.astype(BF16)


def _in_kernel(x_ref, xs_ref, g1_ref, wm_ref, wg_ref, bg_ref, qg_ref, kg_ref,
               p_ref, q_ref, k_ref, v_ref, kf_ref, vf_ref, gate_ref,
               ps_ref, qs_ref, kfs_ref, vfs_ref, gates_ref, h_scr, hs_scr):
    i = pl.program_id(0)
    j = pl.program_id(1)
    weights = (wm_ref, wg_ref, bg_ref, qg_ref, kg_ref)

    @pl.when(j == 0)
    def _():
        h_scr[...] = _rms(x_ref[...], g1_ref[...]).astype(BF16)

    @pl.when((i == 0) & (j == 0))
    def _():
        hs_scr[...] = _rms(xs_ref[...], g1_ref[...]).astype(BF16)

    _in_rows(h_scr[...], *weights, p_ref, q_ref, kf_ref, vf_ref, gate_ref, k_ref, v_ref)

    @pl.when(i == 0)
    def _():
        _in_rows(hs_scr[...], *weights, ps_ref, qs_ref, kfs_ref, vfs_ref, gates_ref)


def _in_proj(x, xs, g1, w_mixed, w_gate, b_gate, qg, kg, *, layer, tm):
    m = x.shape[0]
    ms = xs.shape[0]
    head_spec = pl.BlockSpec((tm, IN_SEG_TN), lambda i, j: (i, j))

    def sample_spec(width):
        return pl.BlockSpec((ms, width), lambda i, j: (0, jnp.where(i == 0, j, IN_STEPS - 1)))

    out_shape = (
        jax.ShapeDtypeStruct((m, POOL_WIDTH), F32),
        jax.ShapeDtypeStruct((m, ATTN_WIDTH), BF16),
        jax.ShapeDtypeStruct((m, ATTN_WIDTH), BF16),
        jax.ShapeDtypeStruct((m, ATTN_WIDTH), BF16),
        jax.ShapeDtypeStruct((m, ATTN_WIDTH), F32),
        jax.ShapeDtypeStruct((m, ATTN_WIDTH), F32),
        jax.ShapeDtypeStruct((m, 2 * D_MODEL), F32),
        jax.ShapeDtypeStruct((ms, POOL_WIDTH), F32),
        jax.ShapeDtypeStruct((ms, ATTN_WIDTH), BF16),
        jax.ShapeDtypeStruct((ms, ATTN_WIDTH), F32),
        jax.ShapeDtypeStruct((ms, ATTN_WIDTH), F32),
        jax.ShapeDtypeStruct((ms, 2 * D_MODEL), F32),
    )
    return pl.pallas_call(
        _in_kernel,
        grid=(m // tm, IN_STEPS),
        in_specs=[
            pl.BlockSpec((tm, D_MODEL), lambda i, j: (i, 0)),
            pl.BlockSpec((ms, D_MODEL), lambda i, j: (0, 0)),
            pl.BlockSpec((None, 1, D_MODEL), lambda i, j: (layer, 0, 0)),
            pl.BlockSpec((None, D_MODEL, 4 * IN_SEG_TN), lambda i, j: (layer, 0, j)),
            pl.BlockSpec((None, D_MODEL, IN_GATE_TN), lambda i, j: (layer, 0, j)),
            pl.BlockSpec((None, 1, IN_GATE_TN), lambda i, j: (layer, 0, j)),
            pl.BlockSpec((None, 1, HEAD_DIM), lambda i, j: (layer, 0, 0)),
            pl.BlockSpec((None, 1, HEAD_DIM), lambda i, j: (layer, 0, 0)),
        ],
        out_specs=(head_spec,) * 6 + (pl.BlockSpec((tm, IN_GATE_TN), lambda i, j: (i, j)),)
                  + (sample_spec(IN_SEG_TN),) * 4 + (sample_spec(IN_GATE_TN),),
        out_shape=out_shape,
        scratch_shapes=[pltpu.VMEM((tm, D_MODEL), BF16), pltpu.VMEM((ms, D_MODEL), BF16)],
        compiler_params=_params(("arbitrary", "arbitrary")),
        name="in_proj",
    )(x, xs, g1, w_mixed, w_gate, b_gate, qg, kg)


def _attn_prompt_kernel(q_ref, k0_ref, k1_ref, k2_ref, v0_ref, v1_ref, v2_ref, bias_ref, o_ref):
    i = pl.program_id(1)
    k_refs = (k0_ref, k1_ref, k2_ref)
    v_refs = (v0_ref, v1_ref, v2_ref)

    def heads(n_valid):
        blocks = range(KEY_BLOCKS - n_valid, KEY_BLOCKS)
        for h in range(N_HEADS):
            sl = slice(h * HEAD_DIM, (h + 1) * HEAD_DIM)
            qh = q_ref[:, sl]
            m = l = o = None
            for d in blocks:
                s = _dot_t(qh, k_refs[d][:, sl]) + bias_ref[h, :, d * Q_BLOCK:(d + 1) * Q_BLOCK]
                m_d = jnp.max(s, axis=-1, keepdims=True)
                if m is None:
                    m = m_d
                    e = jnp.exp2(s - m)
                    l = jnp.sum(e, axis=-1, keepdims=True)
                    o = _dot(e.astype(BF16), v_refs[d][:, sl])
                else:
                    m_new = jnp.maximum(m, m_d)
                    alpha = jnp.exp2(m - m_new)
                    e = jnp.exp2(s - m_new)
                    l = alpha * l + jnp.sum(e, axis=-1, keepdims=True)
                    o = alpha * o + _dot(e.astype(BF16), v_refs[d][:, sl])
                    m = m_new
            o_ref[:, sl] = (o / l).astype(BF16)

    for n_valid in range(1, KEY_BLOCKS):
        pl.when(i == n_valid - 1)(functools.partial(heads, n_valid))
    pl.when(i >= KEY_BLOCKS - 1)(functools.partial(heads, KEY_BLOCKS))


def _attn_prompt(q, k, v, bias, *, layer, n, s):
    nb = s // Q_BLOCK
    q_spec = pl.BlockSpec((Q_BLOCK, ATTN_WIDTH), lambda b, i: (b * nb + i, 0))

    def kv_spec(d):
        return pl.BlockSpec((Q_BLOCK, ATTN_WIDTH),
                            lambda b, i: (b * nb + jnp.maximum(i - (KEY_BLOCKS - 1) + d, 0), 0))

    bias_spec = pl.BlockSpec((N_HEADS, Q_BLOCK, KEY_BLOCKS * Q_BLOCK), lambda b, i: (layer, 0, 0),
                             pipeline_mode=pl.Buffered(1))
    return pl.pallas_call(
        _attn_prompt_kernel,
        grid=(n, nb),
        in_specs=[q_spec] + [kv_spec(d) for d in range(KEY_BLOCKS)] * 2 + [bias_spec],
        out_specs=q_spec,
        out_shape=jax.ShapeDtypeStruct((n * s, ATTN_WIDTH), BF16),
        compiler_params=_params(("parallel", "arbitrary")),
        name="attn_prompt",
    )(q, k, k, k, v, v, v, bias)


def _shift_cache_kernel(ck_ref, cv_ref, ko_ref, vo_ref, *, new_rows):
    for src, dst in ((ck_ref, ko_ref), (cv_ref, vo_ref)):
        keep = src.shape[0] - new_rows
        dst[0:keep, :] = src[new_rows:, :]
        dst[keep:, :] = jnp.zeros((new_rows, HEAD_DIM), F32)


def _shift_caches(cache_k, cache_v, *, t):
    depth, n, past = cache_k.shape[:3]
    rows = past * N_HEADS
    spec = pl.BlockSpec((None, None, rows, HEAD_DIM), lambda l, b: (l, b, 0, 0))
    shape = jax.ShapeDtypeStruct((depth, n, rows, HEAD_DIM), F32)
    return pl.pallas_call(
        functools.partial(_shift_cache_kernel, new_rows=t * N_HEADS),
        grid=(depth, n),
        in_specs=[spec, spec],
        out_specs=(spec, spec),
        out_shape=(shape, shape),
        compiler_params=_params(("parallel", "parallel")),
        name="shift_caches",
    )(cache_k.reshape(depth, n, rows, HEAD_DIM), cache_v.reshape(depth, n, rows, HEAD_DIM))


def _attn_sample_kernel(q_ref, kf_ref, vf_ref, ck_ref, cv_ref, bias_ref, k_alias, v_alias,
                        o_ref, ko_ref, vo_ref, kk_scr, vv_scr, *, t, past):
    del k_alias, v_alias
    for new_ref, cache_ref, out_ref, scr in ((kf_ref, ck_ref, ko_ref, kk_scr),
                                             (vf_ref, cv_ref, vo_ref, vv_scr)):
        for h in range(N_HEADS):
            sl = slice(h * HEAD_DIM, (h + 1) * HEAD_DIM)
            out_ref[pl.ds(h, t, stride=N_HEADS), :] = new_ref[:, sl]
            scr[0:past, sl] = cache_ref[pl.ds(h, past, stride=N_HEADS), :].astype(BF16)
        scr[past:past + t, :] = new_ref[...].astype(BF16)
        scr[past + t:, :] = jnp.zeros((SAMPLE_KEYS_PAD - past - t, ATTN_WIDTH), BF16)
    for h in range(N_HEADS):
        sl = slice(h * HEAD_DIM, (h + 1) * HEAD_DIM)
        s = _dot_t(q_ref[:, sl], kk_scr[:, sl]) + bias_ref[h]
        m = jnp.max(s, axis=-1, keepdims=True)
        e = jnp.exp2(s - m)
        l = jnp.sum(e, axis=-1, keepdims=True)
        o = _dot(e.astype(BF16), vv_scr[:, sl])
        o_ref[:, sl] = (o / l).astype(BF16)


def _attn_sample(q, kf, vf, cache_k, cache_v, bias, k_upd, v_upd, *, layer, n, t):
    depth, _, past = cache_k.shape[:3]
    rows = past * N_HEADS
    new_rows = t * N_HEADS
    row_spec = pl.BlockSpec((t, ATTN_WIDTH), lambda b: (b, 0))
    cache_spec = pl.BlockSpec((None, None, rows, HEAD_DIM), lambda b: (layer, b, 0, 0))
    new_spec = pl.BlockSpec((None, None, new_rows, HEAD_DIM), lambda b: (layer, b, rows // new_rows - 1, 0))
    bias_spec = pl.BlockSpec((N_HEADS, t, SAMPLE_KEYS_PAD), lambda b: (layer, 0, 0),
                             pipeline_mode=pl.Buffered(1))
    any_spec = pl.BlockSpec(memory_space=pl.ANY)
    upd = jax.ShapeDtypeStruct(k_upd.shape, F32)
    return pl.pallas_call(
        functools.partial(_attn_sample_kernel, t=t, past=past),
        grid=(n,),
        in_specs=[row_spec, row_spec, row_spec, cache_spec, cache_spec, bias_spec, any_spec, any_spec],
        out_specs=(row_spec, new_spec, new_spec),
        out_shape=(jax.ShapeDtypeStruct((n * t, ATTN_WIDTH), BF16), upd, upd),
        input_output_aliases={6: 1, 7: 2},
        scratch_shapes=[pltpu.VMEM((SAMPLE_KEYS_PAD, ATTN_WIDTH), BF16),
                        pltpu.VMEM((SAMPLE_KEYS_PAD, ATTN_WIDTH), BF16)],
        compiler_params=_params(("arbitrary",)),
        name="attn_sample",
    )(q, kf, vf, cache_k.reshape(depth, n, rows, HEAD_DIM), cache_v.reshape(depth, n, rows, HEAD_DIM),
      bias, k_upd, v_upd)


def _merge_kernel(x_ref, p_ref, hist_ref, a_ref, gate_ref, pw_ref, ps_ref, wa_ref, wb_ref, wo_ref,
                  o_ref, xp_scr, stage_scr, d_scr, *, nseq, t, blocks_per_seq, pos0):
    i = pl.program_id(0)
    row = lax.broadcasted_iota(jnp.int32, (t, 1), 0)
    if blocks_per_seq is None:
        pos = pos0 + row
    else:
        pos = (i % blocks_per_seq) * t + row
    gated_b = gate_ref[:, D_MODEL:] * _dot(a_ref[...], wb_ref[...])

    t0 = POOL_PAD + HIST_ROWS
    xp_scr[0:POOL_PAD, :] = jnp.zeros((POOL_PAD, POOL_WIDTH), F32)
    stage_scr[:, 0:POOL_PAD, :] = jnp.zeros((2, POOL_PAD, POOL_GROUP), F32)
    for s in range(nseq):
        rows = slice(s * t, (s + 1) * t)
        hist = hist_ref[s * HIST_ROWS:(s + 1) * HIST_ROWS, :]
        if blocks_per_seq is not None:
            hist = jnp.where(i % blocks_per_seq == 0, 0.0, hist)
        xp_scr[POOL_PAD:t0, :] = hist
        xp_scr[t0:, :] = p_ref[rows, :]
        for gi, w in enumerate(POOL_WINDOWS):
            cols = slice(gi * POOL_GROUP, (gi + 1) * POOL_GROUP)
            src = xp_scr.at[:, cols]
            shift = 1
            while 2 * shift < w:
                dst = stage_scr.at[(shift.bit_length() - 1) % 2]
                dst[POOL_PAD:, :] = src[POOL_PAD:, :] + src[POOL_PAD - shift:t0 + t - shift, :]
                src = dst
                shift *= 2
            acc = src[t0:, :] + src[t0 - shift:t0 + t - shift, :]
            inv_cnt = 1.0 / jnp.minimum(pos + 1, w).astype(F32)
            d_scr[rows, cols] = (acc * inv_cnt - p_ref[rows, cols]).astype(BF16)
    pooled = jnp.concatenate(
        [_dot(d_scr[:, gi * POOL_GROUP:(gi + 1) * POOL_GROUP], pw_ref[gi]) for gi in range(len(POOL_WINDOWS))],
        axis=-1) * ps_ref[...]
    branch_a = _dot(pooled.astype(BF16), wa_ref[...])
    merged = gate_ref[:, :D_MODEL] * branch_a + gated_b
    o_ref[...] = x_ref[...] + _dot(merged.astype(BF16), wo_ref[...])


def _merge(x, p, hist, attn, gates, pool_w, pool_scale, wa, wb, wo, *, layer, tm, nseq, t, blocks_per_seq, pos0,
           hist_spec):
    m = x.shape[0]

    def row_spec(width):
        return pl.BlockSpec((tm, width), lambda i: (i, 0))

    return pl.pallas_call(
        functools.partial(_merge_kernel, nseq=nseq, t=t, blocks_per_seq=blocks_per_seq, pos0=pos0),
        grid=(m // tm,),
        in_specs=[row_spec(D_MODEL), row_spec(POOL_WIDTH), hist_spec, row_spec(ATTN_WIDTH),
                  row_spec(2 * D_MODEL),
                  _layer_spec(pool_w.shape[1:], layer), _layer_spec(pool_scale.shape[1:], layer),
                  _layer_spec(wa.shape[1:], layer), _layer_spec(wb.shape[1:], layer),
                  _layer_spec(wo.shape[1:], layer)],
        out_specs=row_spec(D_MODEL),
        out_shape=jax.ShapeDtypeStruct((m, D_MODEL), F32),
        scratch_shapes=[pltpu.VMEM((POOL_PAD + HIST_ROWS + t, POOL_WIDTH), F32),
                        pltpu.VMEM((2, POOL_PAD + HIST_ROWS + t, POOL_GROUP), F32),
                        pltpu.VMEM((tm, POOL_WIDTH), BF16)],
        compiler_params=_params(("arbitrary",)),
        name="merge",
    )(x, p, hist, attn, gates, pool_w, pool_scale, wa, wb, wo)


FFN_TF = 512


def _ffn_kernel(x_ref, xs_ref, g2_ref, wu_ref, wd_ref, o_ref, os_ref, h_scr, hs_scr):
    i = pl.program_id(0)
    j = pl.program_id(1)

    def start(src_ref, acc_ref, scr):
        x = src_ref[...]
        scr[...] = _rms(x, g2_ref[...]).astype(BF16)
        acc_ref[...] = x

    def step(acc_ref, scr):
        u = jnp.square(jnp.maximum(_dot(scr[...], wu_ref[...].astype(BF16)), 0.0))
        acc_ref[...] += _dot(u.astype(BF16), wd_ref[...].astype(BF16))

    pl.when(j == 0)(functools.partial(start, x_ref, o_ref, h_scr))
    pl.when((i == 0) & (j == 0))(functools.partial(start, xs_ref, os_ref, hs_scr))
    step(o_ref, h_scr)
    pl.when(i == 0)(functools.partial(step, os_ref, hs_scr))


def _ffn(x, xs, g2, wu, wd, *, layer, tm):
    m = x.shape[0]
    ms = xs.shape[0]
    sample_spec = pl.BlockSpec((ms, D_MODEL), lambda i, j: (0, 0))
    return pl.pallas_call(
        _ffn_kernel,
        grid=(m // tm, D_FF // FFN_TF),
        in_specs=[pl.BlockSpec((tm, D_MODEL), lambda i, j: (i, 0)),
                  sample_spec,
                  pl.BlockSpec((None, 1, D_MODEL), lambda i, j: (layer, 0, 0)),
                  pl.BlockSpec((None, D_MODEL, FFN_TF), lambda i, j: (layer, 0, j)),
                  pl.BlockSpec((None, FFN_TF, D_MODEL), lambda i, j: (layer, j, 0))],
        out_specs=(pl.BlockSpec((tm, D_MODEL), lambda i, j: (i, 0)), sample_spec),
        out_shape=(jax.ShapeDtypeStruct((m, D_MODEL), F32), jax.ShapeDtypeStruct((ms, D_MODEL), F32)),
        scratch_shapes=[pltpu.VMEM((tm, D_MODEL), BF16), pltpu.VMEM((ms, D_MODEL), BF16)],
        compiler_params=_params(("arbitrary", "arbitrary")),
        name="ffn",
    )(x, xs, g2, wu, wd)


PROMPT_TM = 1024
MERGE_TM = 256


def kernel(x_prompt, x_sample, state_pool, cache_k, cache_v, norm1_g, w_in, b_gate, pool_w, pool_scale,
           q_norm_g, k_norm_g, rel_bias, w_branch_a, w_branch_b, w_out, norm2_g, w_up, w_down):
    nb, seq, _ = x_prompt.shape
    ns, t, _ = x_sample.shape
    depth = w_in.shape[0]
    past = cache_k.shape[2]
    assert seq % PROMPT_TM == 0 and seq % MERGE_TM == 0 and seq % Q_BLOCK == 0 and seq >= BAND_PAST
    assert t >= POOL_HIST and t % HIST_ROWS == 0 and past + t <= SAMPLE_KEYS_PAD and past <= BAND_PAST
    assert past % t == 0 and past > t

    xp = x_prompt.reshape(nb * seq, D_MODEL)
    xs = x_sample.reshape(ns * t, D_MODEL)
    blocks_per_seq = seq // MERGE_TM
    prompt_hist_spec = pl.BlockSpec(
        (HIST_ROWS, POOL_WIDTH), lambda i: (jnp.maximum(i * (MERGE_TM // HIST_ROWS) - 1, 0), 0))
    sample_hist_spec = pl.BlockSpec((ns * HIST_ROWS, POOL_WIDTH), lambda i: (0, 0))

    g1 = norm1_g.reshape(depth, 1, D_MODEL)
    g2 = norm2_g.reshape(depth, 1, D_MODEL)
    bg = b_gate.reshape(depth, 1, 2 * D_MODEL)
    qg = q_norm_g.reshape(depth, 1, HEAD_DIM)
    kg = k_norm_g.reshape(depth, 1, HEAD_DIM)
    ps = pool_scale.reshape(depth, 1, POOL_WIDTH)
    w_mixed, w_gate = _regroup_in_weights(w_in)
    pw = pool_w.astype(BF16)
    wa = w_branch_a.astype(BF16)
    wb = w_branch_b.astype(BF16)
    wo = w_out.astype(BF16)
    hist_s = jnp.pad(state_pool, ((0, 0), (0, 0), (HIST_ROWS - POOL_HIST, 0), (0, 0)))
    hist_s = hist_s.reshape(depth, ns * HIST_ROWS, POOL_WIDTH)

    bias_p = _bias_table(rel_bias, q_rows=Q_BLOCK, keys=KEY_BLOCKS * Q_BLOCK,
                         offset=(KEY_BLOCKS - 1) * Q_BLOCK, visible_fn=_prompt_visible)
    bias_s = _bias_table(rel_bias, q_rows=t, keys=SAMPLE_KEYS_PAD, offset=past,
                         visible_fn=lambda qi, kj: kj < past + t)

    k_upd, v_upd = _shift_caches(cache_k, cache_v, t=t)

    pool_p, kp_l, vp_l, pool_s = [], [], [], []
    for l in range(depth):
        (p, q, k, v, kf, vf, gates,
         p_s, q_s, kf_s, vf_s, gates_s) = _in_proj(xp, xs, g1, w_mixed, w_gate, bg, qg, kg, layer=l, tm=PROMPT_TM)

        attn = _attn_prompt(q, k, v, bias_p, layer=l, n=nb, s=seq)
        pool_p.append(p.reshape(nb, seq, POOL_WIDTH)[:, seq - POOL_HIST:])
        kp_l.append(kf.reshape(nb, seq, N_HEADS, HEAD_DIM)[:, seq - BAND_PAST:])
        vp_l.append(vf.reshape(nb, seq, N_HEADS, HEAD_DIM)[:, seq - BAND_PAST:])
        xp = _merge(xp, p, p, attn, gates, pw, ps, wa, wb, wo, layer=l, tm=MERGE_TM, nseq=1, t=MERGE_TM,
                    blocks_per_seq=blocks_per_seq, pos0=None, hist_spec=prompt_hist_spec)

        attn_s, k_upd, v_upd = _attn_sample(q_s, kf_s, vf_s, cache_k, cache_v, bias_s, k_upd, v_upd,
                                            layer=l, n=ns, t=t)
        pool_s.append(p_s.reshape(ns, t, POOL_WIDTH)[:, t - POOL_HIST:])
        xs = _merge(xs, p_s, hist_s[l], attn_s, gates_s, pw, ps, wa, wb, wo, layer=l, tm=ns * t, nseq=ns, t=t,
                    blocks_per_seq=None, pos0=PAST_LEN, hist_spec=sample_hist_spec)

        xp, xs = _ffn(xp, xs, g2, w_up, w_down, layer=l, tm=PROMPT_TM)

    return (xp.reshape(nb, seq, D_MODEL), xs.reshape(ns, t, D_MODEL),
            jnp.stack(pool_p), jnp.stack(kp_l), jnp.stack(vp_l),
            jnp.stack(pool_s), k_upd.reshape(cache_k.shape), v_upd.reshape(cache_v.shape))
```

```python
import functools

import numpy as np

import jax
import jax.numpy as jnp
from jax import lax
from jax.experimental import pallas as pl
from jax.experimental.pallas import tpu as pltpu

D_MODEL = 2048
PAST_LEN = 1024
CHUNK = 64
N_PREV_CHUNKS = 8
BAND_PAST = N_PREV_CHUNKS * CHUNK
POOL_WIDTH = D_MODEL // 2
POOL_WINDOWS = (2, 4, 8, 16)
POOL_GROUP = POOL_WIDTH // len(POOL_WINDOWS)
POOL_HIST = max(POOL_WINDOWS) - 1
HEAD_DIM = 128
ATTN_WIDTH = D_MODEL // 2
N_HEADS = ATTN_WIDTH // HEAD_DIM
MAX_REL = 128
D_FF = 4 * D_MODEL
EPS = 1e-6
ATTN_SCALE = HEAD_DIM ** -0.5
LOG2E = 1.4426950408889634
Q_SCALE = ATTN_SCALE * LOG2E
NEG_INF = -1e30
IN_WIDTH = POOL_WIDTH + 3 * ATTN_WIDTH + 2 * D_MODEL

BF16 = jnp.bfloat16
F32 = jnp.float32

VMEM_LIMIT_BYTES = 63 * 1024 * 1024
LANES = 128

HIST_ROWS = 16
POOL_PAD = 8
Q_BLOCK = 4 * CHUNK
KEY_BLOCKS = 3
SAMPLE_KEYS_PAD = 640


def _rms(xf, g):
    ms = jnp.mean(xf * xf, axis=-1, keepdims=True)
    return xf * lax.rsqrt(ms + EPS) * g


def _dot(a, b):
    return jnp.dot(a, b, preferred_element_type=F32)


def _dot_t(a, b):
    return lax.dot_general(a, b, (((1,), (1,)), ((), ())), preferred_element_type=F32)


def _params(semantics):
    return pltpu.CompilerParams(dimension_semantics=semantics, vmem_limit_bytes=VMEM_LIMIT_BYTES)


def _layer_spec(shape, layer):
    nd = len(shape)
    return pl.BlockSpec((None,) + tuple(shape), lambda *_: (layer,) + (0,) * nd,
                        pipeline_mode=pl.Buffered(1))


def _bias_table_kernel(ext_ref, o_ref, *, q_rows, keys, visible_fn):
    width = ext_ref.shape[-1]
    qi = lax.broadcasted_iota(jnp.int32, (q_rows, keys), 0)
    kj = lax.broadcasted_iota(jnp.int32, (q_rows, keys), 1)
    visible = visible_fn(qi, kj)
    for g in range(ext_ref.shape[0]):
        row = jnp.broadcast_to(ext_ref[g:g + 1, :], (q_rows, width))
        skew = pltpu.roll(row, 0, 1, stride=1, stride_axis=0)
        o_ref[g] = jnp.where(visible, skew[:, :keys] * LOG2E, NEG_INF)


def _bias_table(rel_bias, *, q_rows, keys, offset, visible_fn):
    groups = rel_bias.shape[0] * rel_bias.shape[1]
    width = -(-(keys + q_rows - 1) // LANES) * LANES
    idx = np.zeros((width,), np.int32)
    idx[:keys] = np.clip(offset - np.arange(keys), -MAX_REL, MAX_REL) + MAX_REL
    back = np.arange(1, q_rows)
    idx[width - back] = np.clip(offset + back, -MAX_REL, MAX_REL) + MAX_REL
    ext = rel_bias.reshape(groups, 2 * MAX_REL + 1)[:, idx].astype(F32)
    return pl.pallas_call(
        functools.partial(_bias_table_kernel, q_rows=q_rows, keys=keys, visible_fn=visible_fn),
        out_shape=jax.ShapeDtypeStruct((groups, q_rows, keys), F32),
        compiler_params=pltpu.CompilerParams(vmem_limit_bytes=VMEM_LIMIT_BYTES),
        name="bias_table",
    )(ext)


def _prompt_visible(qi, kj):
    own = qi // CHUNK + (KEY_BLOCKS - 1) * (Q_BLOCK // CHUNK)
    kc = kj // CHUNK
    return (kc <= own) & (kc >= own - N_PREV_CHUNKS)


IN_HEADS = 2
IN_STEPS = N_HEADS // IN_HEADS
IN_SEG_TN = IN_HEADS * HEAD_DIM
IN_GATE_TN = 2 * D_MODEL // IN_STEPS


def _regroup_kernel(wq_ref, wk_ref, wp_ref, wv_ref, wg_ref, mixed_ref, gate_ref):
    for s, src in enumerate((wq_ref, wk_ref, wp_ref, wv_ref)):
        mixed_ref[:, s * IN_SEG_TN:(s + 1) * IN_SEG_TN] = src[...].astype(BF16)
    gate_ref[...] = wg_ref[...].astype(BF16)


def _regroup_in_weights(w_in):
    depth = w_in.shape[0]

    def head_cols(seg):
        return pl.BlockSpec((None, D_MODEL, IN_SEG_TN), lambda l, j: (l, 0, seg * IN_STEPS + j))

    gate0 = 4 * ATTN_WIDTH // IN_GATE_TN
    return pl.pallas_call(
        _regroup_kernel,
        grid=(depth, IN_STEPS),
        in_specs=[head_cols(1), head_cols(2), head_cols(0), head_cols(3),
                  pl.BlockSpec((None, D_MODEL, IN_GATE_TN), lambda l, j: (l, 0, gate0 + j))],
        out_specs=(pl.BlockSpec((None, D_MODEL, 4 * IN_SEG_TN), lambda l, j: (l, 0, j)),
                   pl.BlockSpec((None, D_MODEL, IN_GATE_TN), lambda l, j: (l, 0, j))),
        out_shape=(jax.ShapeDtypeStruct((depth, D_MODEL, 4 * ATTN_WIDTH), BF16),
                   jax.ShapeDtypeStruct((depth, D_MODEL, 2 * D_MODEL), BF16)),
        compiler_params=_params(("parallel", "parallel")),
        name="regroup_w_in",
    )(w_in, w_in, w_in, w_in, w_in)


def _in_rows(h, wm_ref, wg_ref, bg_ref, qg_ref, kg_ref, p_ref, q_ref, kf_ref, vf_ref, gate_ref,
             k_ref=None, v_ref=None):
    gate_ref[...] = jax.nn.sigmoid(_dot(h, wg_ref[...]) + bg_ref[...])
    z = _dot(h, wm_ref[...])
    for hd in range(IN_HEADS):
        sl = slice(hd * HEAD_DIM, (hd + 1) * HEAD_DIM)
        q_ref[:, sl] = (_rms(z[:, sl], qg_ref[...]) * Q_SCALE).astype(BF16)
        kn = _rms(z[:, IN_SEG_TN + hd * HEAD_DIM:IN_SEG_TN + (hd + 1) * HEAD_DIM], kg_ref[...])
        kf_ref[:, sl] = kn
        if k_ref is not None:
            k_ref[:, sl] = kn.astype(BF16)
    p_ref[...] = z[:, 2 * IN_SEG_TN:3 * IN_SEG_TN]
    v = z[:, 3 * IN_SEG_TN:4 * IN_SEG_TN]
    vf_ref[...] = v
    if v_ref is not None:
        v_ref[...] = v.astype(BF16)


def _in_kernel(x_ref, xs_ref, g1_ref, wm_ref, wg_ref, bg_ref, qg_ref, kg_ref,
               p_ref, q_ref, k_ref, v_ref, kf_ref, vf_ref, gate_ref,
               ps_ref, qs_ref, kfs_ref, vfs_ref, gates_ref, h_scr, hs_scr):
    i = pl.program_id(0)
    j = pl.program_id(1)
    weights = (wm_ref, wg_ref, bg_ref, qg_ref, kg_ref)

    @pl.when(j == 0)
    def _():
        h_scr[...] = _rms(x_ref[...], g1_ref[...]).astype(BF16)

    @pl.when((i == 0) & (j == 0))
    def _():
        hs_scr[...] = _rms(xs_ref[...], g1_ref[...]).astype(BF16)

    _in_rows(h_scr[...], *weights, p_ref, q_ref, kf_ref, vf_ref, gate_ref, k_ref, v_ref)

    @pl.when(i == 0)
    def _():
        _in_rows(hs_scr[...], *weights, ps_ref, qs_ref, kfs_ref, vfs_ref, gates_ref)


def _in_proj(x, xs, g1, w_mixed, w_gate, b_gate, qg, kg, *, layer, tm):
    m = x.shape[0]
    ms = xs.shape[0]
    head_spec = pl.BlockSpec((tm, IN_SEG_TN), lambda i, j: (i, j))

    def sample_spec(width):
        return pl.BlockSpec((ms, width), lambda i, j: (0, jnp.where(i == 0, j, IN_STEPS - 1)))

    out_shape = (
        jax.ShapeDtypeStruct((m, POOL_WIDTH), F32),
        jax.ShapeDtypeStruct((m, ATTN_WIDTH), BF16),
        jax.ShapeDtypeStruct((m, ATTN_WIDTH), BF16),
        jax.ShapeDtypeStruct((m, ATTN_WIDTH), BF16),
        jax.ShapeDtypeStruct((m, ATTN_WIDTH), F32),
        jax.ShapeDtypeStruct((m, ATTN_WIDTH), F32),
        jax.ShapeDtypeStruct((m, 2 * D_MODEL), F32),
        jax.ShapeDtypeStruct((ms, POOL_WIDTH), F32),
        jax.ShapeDtypeStruct((ms, ATTN_WIDTH), BF16),
        jax.ShapeDtypeStruct((ms, ATTN_WIDTH), F32),
        jax.ShapeDtypeStruct((ms, ATTN_WIDTH), F32),
        jax.ShapeDtypeStruct((ms, 2 * D_MODEL), F32),
    )
    return pl.pallas_call(
        _in_kernel,
        grid=(m // tm, IN_STEPS),
        in_specs=[
            pl.BlockSpec((tm, D_MODEL), lambda i, j: (i, 0)),
            pl.BlockSpec((ms, D_MODEL), lambda i, j: (0, 0)),
            pl.BlockSpec((None, 1, D_MODEL), lambda i, j: (layer, 0, 0)),
            pl.BlockSpec((None, D_MODEL, 4 * IN_SEG_TN), lambda i, j: (layer, 0, j)),
            pl.BlockSpec((None, D_MODEL, IN_GATE_TN), lambda i, j: (layer, 0, j)),
            pl.BlockSpec((None, 1, IN_GATE_TN), lambda i, j: (layer, 0, j)),
            pl.BlockSpec((None, 1, HEAD_DIM), lambda i, j: (layer, 0, 0)),
            pl.BlockSpec((None, 1, HEAD_DIM), lambda i, j: (layer, 0, 0)),
        ],
        out_specs=(head_spec,) * 6 + (pl.BlockSpec((tm, IN_GATE_TN), lambda i, j: (i, j)),)
                  + (sample_spec(IN_SEG_TN),) * 4 + (sample_spec(IN_GATE_TN),),
        out_shape=out_shape,
        scratch_shapes=[pltpu.VMEM((tm, D_MODEL), BF16), pltpu.VMEM((ms, D_MODEL), BF16)],
        compiler_params=_params(("arbitrary", "arbitrary")),
        name="in_proj",
    )(x, xs, g1, w_mixed, w_gate, b_gate, qg, kg)


def _attn_prompt_kernel(q_ref, k0_ref, k1_ref, k2_ref, v0_ref, v1_ref, v2_ref, bias_ref, o_ref):
    i = pl.program_id(1)
    k_refs = (k0_ref, k1_ref, k2_ref)
    v_refs = (v0_ref, v1_ref, v2_ref)

    def heads(n_valid):
        blocks = range(KEY_BLOCKS - n_valid, KEY_BLOCKS)
        for h in range(N_HEADS):
            sl = slice(h * HEAD_DIM, (h + 1) * HEAD_DIM)
            qh = q_ref[:, sl]
            m = l = o = None
            for d in blocks:
                s = _dot_t(qh, k_refs[d][:, sl]) + bias_ref[h, :, d * Q_BLOCK:(d + 1) * Q_BLOCK]
                m_d = jnp.max(s, axis=-1, keepdims=True)
                if m is None:
                    m = m_d
                    e = jnp.exp2(s - m)
                    l = jnp.sum(e, axis=-1, keepdims=True)
                    o = _dot(e.astype(BF16), v_refs[d][:, sl])
                else:
                    m_new = jnp.maximum(m, m_d)
                    alpha = jnp.exp2(m - m_new)
                    e = jnp.exp2(s - m_new)
                    l = alpha * l + jnp.sum(e, axis=-1, keepdims=True)
                    o = alpha * o + _dot(e.astype(BF16), v_refs[d][:, sl])
                    m = m_new
            o_ref[:, sl] = (o / l).astype(BF16)

    for n_valid in range(1, KEY_BLOCKS):
        pl.when(i == n_valid - 1)(functools.partial(heads, n_valid))
    pl.when(i >= KEY_BLOCKS - 1)(functools.partial(heads, KEY_BLOCKS))


def _attn_prompt(q, k, v, bias, *, layer, n, s):
    nb = s // Q_BLOCK
    q_spec = pl.BlockSpec((Q_BLOCK, ATTN_WIDTH), lambda b, i: (b * nb + i, 0))

    def kv_spec(d):
        return pl.BlockSpec((Q_BLOCK, ATTN_WIDTH),
                            lambda b, i: (b * nb + jnp.maximum(i - (KEY_BLOCKS - 1) + d, 0), 0))

    bias_spec = pl.BlockSpec((N_HEADS, Q_BLOCK, KEY_BLOCKS * Q_BLOCK), lambda b, i: (layer, 0, 0),
                             pipeline_mode=pl.Buffered(1))
    return pl.pallas_call(
        _attn_prompt_kernel,
        grid=(n, nb),
        in_specs=[q_spec] + [kv_spec(d) for d in range(KEY_BLOCKS)] * 2 + [bias_spec],
        out_specs=q_spec,
        out_shape=jax.ShapeDtypeStruct((n * s, ATTN_WIDTH), BF16),
        compiler_params=_params(("parallel", "arbitrary")),
        name="attn_prompt",
    )(q, k, k, k, v, v, v, bias)


def _shift_cache_block(src_ref, dst_ref, *, new_rows):
    keep = src_ref.shape[0] - new_rows
    dst_ref[0:keep, :] = src_ref[new_rows:, :]
    dst_ref[keep:, :] = jnp.zeros((new_rows, HEAD_DIM), F32)


def _attn_sample_kernel(q_ref, kf_ref, vf_ref, ck_ref, cv_ref, bias_ref, k_alias, v_alias,
                        o_ref, ko_ref, vo_ref, kk_scr, vv_scr, *, t, past):
    del k_alias, v_alias
    for new_ref, cache_ref, out_ref, scr in ((kf_ref, ck_ref, ko_ref, kk_scr),
                                             (vf_ref, cv_ref, vo_ref, vv_scr)):
        for h in range(N_HEADS):
            sl = slice(h * HEAD_DIM, (h + 1) * HEAD_DIM)
            out_ref[pl.ds(h, t, stride=N_HEADS), :] = new_ref[:, sl]
            scr[0:past, sl] = cache_ref[pl.ds(h, past, stride=N_HEADS), :].astype(BF16)
        scr[past:past + t, :] = new_ref[...].astype(BF16)
        scr[past + t:, :] = jnp.zeros((SAMPLE_KEYS_PAD - past - t, ATTN_WIDTH), BF16)
    for h in range(N_HEADS):
        sl = slice(h * HEAD_DIM, (h + 1) * HEAD_DIM)
        s = _dot_t(q_ref[:, sl], kk_scr[:, sl]) + bias_ref[h]
        m = jnp.max(s, axis=-1, keepdims=True)
        e = jnp.exp2(s - m)
        l = jnp.sum(e, axis=-1, keepdims=True)
        o = _dot(e.astype(BF16), vv_scr[:, sl])
        o_ref[:, sl] = (o / l).astype(BF16)


def _attn_sample(q, kf, vf, cache_k, cache_v, bias, k_upd, v_upd, *, layer, n, t):
    depth, _, past = cache_k.shape[:3]
    rows = past * N_HEADS
    new_rows = t * N_HEADS
    row_spec = pl.BlockSpec((t, ATTN_WIDTH), lambda b: (b, 0))
    cache_spec = pl.BlockSpec((None, None, rows, HEAD_DIM), lambda b: (layer, b, 0, 0))
    new_spec = pl.BlockSpec((None, None, new_rows, HEAD_DIM), lambda b: (layer, b, rows // new_rows - 1, 0))
    bias_spec = pl.BlockSpec((N_HEADS, t, SAMPLE_KEYS_PAD), lambda b: (layer, 0, 0),
                             pipeline_mode=pl.Buffered(1))
    any_spec = pl.BlockSpec(memory_space=pl.ANY)
    upd = jax.ShapeDtypeStruct(k_upd.shape, F32)
    return pl.pallas_call(
        functools.partial(_attn_sample_kernel, t=t, past=past),
        grid=(n,),
        in_specs=[row_spec, row_spec, row_spec, cache_spec, cache_spec, bias_spec, any_spec, any_spec],
        out_specs=(row_spec, new_spec, new_spec),
        out_shape=(jax.ShapeDtypeStruct((n * t, ATTN_WIDTH), BF16), upd, upd),
        input_output_aliases={6: 1, 7: 2},
        scratch_shapes=[pltpu.VMEM((SAMPLE_KEYS_PAD, ATTN_WIDTH), BF16),
                        pltpu.VMEM((SAMPLE_KEYS_PAD, ATTN_WIDTH), BF16)],
        compiler_params=_params(("arbitrary",)),
        name="attn_sample",
    )(q, kf, vf, cache_k.reshape(depth, n, rows, HEAD_DIM), cache_v.reshape(depth, n, rows, HEAD_DIM),
      bias, k_upd, v_upd)


def _merge_kernel(*refs, nseq, t, blocks_per_seq, pos0, cache_blocks, new_rows):
    x_ref, p_ref, hist_ref, a_ref, gate_ref, pw_ref, ps_ref, wa_ref, wb_ref, wo_ref = refs[:10]
    i = pl.program_id(0)
    if cache_blocks:
        ck_ref, cv_ref, o_ref, ko_ref, vo_ref, xp_scr, stage_scr, d_scr = refs[10:]
        pl.when(i < cache_blocks)(
            functools.partial(_shift_cache_block, ck_ref, ko_ref, new_rows=new_rows))
        pl.when((i >= cache_blocks) & (i < 2 * cache_blocks))(
            functools.partial(_shift_cache_block, cv_ref, vo_ref, new_rows=new_rows))
    else:
        o_ref, xp_scr, stage_scr, d_scr = refs[10:]
    row = lax.broadcasted_iota(jnp.int32, (t, 1), 0)
    if blocks_per_seq is None:
        pos = pos0 + row
    else:
        pos = (i % blocks_per_seq) * t + row
    gated_b = gate_ref[:, D_MODEL:] * _dot(a_ref[...], wb_ref[...])

    t0 = POOL_PAD + HIST_ROWS
    xp_scr[0:POOL_PAD, :] = jnp.zeros((POOL_PAD, POOL_WIDTH), F32)
    stage_scr[:, 0:POOL_PAD, :] = jnp.zeros((2, POOL_PAD, POOL_GROUP), F32)
    for s in range(nseq):
        rows = slice(s * t, (s + 1) * t)
        hist = hist_ref[s * HIST_ROWS:(s + 1) * HIST_ROWS, :]
        if blocks_per_seq is not None:
            hist = jnp.where(i % blocks_per_seq == 0, 0.0, hist)
        xp_scr[POOL_PAD:t0, :] = hist
        xp_scr[t0:, :] = p_ref[rows, :]
        for gi, w in enumerate(POOL_WINDOWS):
            cols = slice(gi * POOL_GROUP, (gi + 1) * POOL_GROUP)
            src = xp_scr.at[:, cols]
            shift = 1
            while 2 * shift < w:
                dst = stage_scr.at[(shift.bit_length() - 1) % 2]
                dst[POOL_PAD:, :] = src[POOL_PAD:, :] + src[POOL_PAD - shift:t0 + t - shift, :]
                src = dst
                shift *= 2
            acc = src[t0:, :] + src[t0 - shift:t0 + t - shift, :]
            inv_cnt = 1.0 / jnp.minimum(pos + 1, w).astype(F32)
            d_scr[rows, cols] = (acc * inv_cnt - p_ref[rows, cols]).astype(BF16)
    pooled = jnp.concatenate(
        [_dot(d_scr[:, gi * POOL_GROUP:(gi + 1) * POOL_GROUP], pw_ref[gi]) for gi in range(len(POOL_WINDOWS))],
        axis=-1) * ps_ref[...]
    branch_a = _dot(pooled.astype(BF16), wa_ref[...])
    merged = gate_ref[:, :D_MODEL] * branch_a + gated_b
    o_ref[...] = x_ref[...] + _dot(merged.astype(BF16), wo_ref[...])


def _merge(x, p, hist, attn, gates, pool_w, pool_scale, wa, wb, wo, *, layer, tm, nseq, t, blocks_per_seq, pos0,
           hist_spec, caches=None, new_frames=0):
    m = x.shape[0]
    steps = m // tm

    def row_spec(width):
        return pl.BlockSpec((tm, width), lambda i: (i, 0))

    operands = [x, p, hist, attn, gates, pool_w, pool_scale, wa, wb, wo]
    in_specs = [row_spec(D_MODEL), row_spec(POOL_WIDTH), hist_spec, row_spec(ATTN_WIDTH),
                row_spec(2 * D_MODEL),
                _layer_spec(pool_w.shape[1:], layer), _layer_spec(pool_scale.shape[1:], layer),
                _layer_spec(wa.shape[1:], layer), _layer_spec(wb.shape[1:], layer),
                _layer_spec(wo.shape[1:], layer)]
    out_specs = [row_spec(D_MODEL)]
    out_shape = [jax.ShapeDtypeStruct((m, D_MODEL), F32)]
    cache_blocks = 0
    if caches is not None:
        depth, n, past = caches[0].shape[:3]
        cache_blocks = depth * n
        assert steps >= 2 * cache_blocks
        rows = past * N_HEADS
        k_spec = pl.BlockSpec((None, rows, HEAD_DIM), lambda i: (jnp.minimum(i, cache_blocks - 1), 0, 0))
        v_spec = pl.BlockSpec((None, rows, HEAD_DIM),
                              lambda i: (jnp.clip(i - cache_blocks, 0, cache_blocks - 1), 0, 0))
        operands += [c.reshape(cache_blocks, rows, HEAD_DIM) for c in caches]
        in_specs += [k_spec, v_spec]
        out_specs += [k_spec, v_spec]
        out_shape += [jax.ShapeDtypeStruct((cache_blocks, rows, HEAD_DIM), F32)] * 2
    outs = pl.pallas_call(
        functools.partial(_merge_kernel, nseq=nseq, t=t, blocks_per_seq=blocks_per_seq, pos0=pos0,
                          cache_blocks=cache_blocks, new_rows=new_frames * N_HEADS),
        grid=(steps,),
        in_specs=in_specs,
        out_specs=tuple(out_specs),
        out_shape=tuple(out_shape),
        scratch_shapes=[pltpu.VMEM((POOL_PAD + HIST_ROWS + t, POOL_WIDTH), F32),
                        pltpu.VMEM((2, POOL_PAD + HIST_ROWS + t, POOL_GROUP), F32),
                        pltpu.VMEM((tm, POOL_WIDTH), BF16)],
        compiler_params=_params(("arbitrary",)),
        name="merge",
    )(*operands)
    if caches is None:
        return outs[0]
    return outs[0], outs[1].reshape(depth, n, rows, HEAD_DIM), outs[2].reshape(depth, n, rows, HEAD_DIM)


FFN_TF = 512


def _ffn_kernel(x_ref, xs_ref, g2_ref, wu_ref, wd_ref, o_ref, os_ref, h_scr, hs_scr):
    i = pl.program_id(0)
    j = pl.program_id(1)

    def start(src_ref, acc_ref, scr):
        x = src_ref[...]
        scr[...] = _rms(x, g2_ref[...]).astype(BF16)
        acc_ref[...] = x

    def step(acc_ref, scr):
        u = jnp.square(jnp.maximum(_dot(scr[...], wu_ref[...].astype(BF16)), 0.0))
        acc_ref[...] += _dot(u.astype(BF16), wd_ref[...].astype(BF16))

    pl.when(j == 0)(functools.partial(start, x_ref, o_ref, h_scr))
    pl.when((i == 0) & (j == 0))(functools.partial(start, xs_ref, os_ref, hs_scr))
    step(o_ref, h_scr)
    pl.when(i == 0)(functools.partial(step, os_ref, hs_scr))


def _ffn(x, xs, g2, wu, wd, *, layer, tm):
    m = x.shape[0]
    ms = xs.shape[0]
    sample_spec = pl.BlockSpec((ms, D_MODEL), lambda i, j: (0, 0))
    return pl.pallas_call(
        _ffn_kernel,
        grid=(m // tm, D_FF // FFN_TF),
        in_specs=[pl.BlockSpec((tm, D_MODEL), lambda i, j: (i, 0)),
                  sample_spec,
                  pl.BlockSpec((None, 1, D_MODEL), lambda i, j: (layer, 0, 0)),
                  pl.BlockSpec((None, D_MODEL, FFN_TF), lambda i, j: (layer, 0, j)),
                  pl.BlockSpec((None, FFN_TF, D_MODEL), lambda i, j: (layer, j, 0))],
        out_specs=(pl.BlockSpec((tm, D_MODEL), lambda i, j: (i, 0)), sample_spec),
        out_shape=(jax.ShapeDtypeStruct((m, D_MODEL), F32), jax.ShapeDtypeStruct((ms, D_MODEL), F32)),
        scratch_shapes=[pltpu.VMEM((tm, D_MODEL), BF16), pltpu.VMEM((ms, D_MODEL), BF16)],
        compiler_params=_params(("arbitrary", "arbitrary")),
        name="ffn",
    )(x, xs, g2, wu, wd)


PROMPT_TM = 1024
MERGE_TM = 256


def kernel(x_prompt, x_sample, state_pool, cache_k, cache_v, norm1_g, w_in, b_gate, pool_w, pool_scale,
           q_norm_g, k_norm_g, rel_bias, w_branch_a, w_branch_b, w_out, norm2_g, w_up, w_down):
    nb, seq, _ = x_prompt.shape
    ns, t, _ = x_sample.shape
    depth = w_in.shape[0]
    past = cache_k.shape[2]
    assert seq % PROMPT_TM == 0 and seq % MERGE_TM == 0 and seq % Q_BLOCK == 0 and seq >= BAND_PAST
    assert t >= POOL_HIST and t % HIST_ROWS == 0 and past + t <= SAMPLE_KEYS_PAD and past <= BAND_PAST
    assert past % t == 0 and past > t

    xp = x_prompt.reshape(nb * seq, D_MODEL)
    xs = x_sample.reshape(ns * t, D_MODEL)
    blocks_per_seq = seq // MERGE_TM
    prompt_hist_spec = pl.BlockSpec(
        (HIST_ROWS, POOL_WIDTH), lambda i: (jnp.maximum(i * (MERGE_TM // HIST_ROWS) - 1, 0), 0))
    sample_hist_spec = pl.BlockSpec((ns * HIST_ROWS, POOL_WIDTH), lambda i: (0, 0))

    g1 = norm1_g.reshape(depth, 1, D_MODEL)
    g2 = norm2_g.reshape(depth, 1, D_MODEL)
    bg = b_gate.reshape(depth, 1, 2 * D_MODEL)
    qg = q_norm_g.reshape(depth, 1, HEAD_DIM)
    kg = k_norm_g.reshape(depth, 1, HEAD_DIM)
    ps = pool_scale.reshape(depth, 1, POOL_WIDTH)
    w_mixed, w_gate = _regroup_in_weights(w_in)
    pw = pool_w.astype(BF16)
    wa = w_branch_a.astype(BF16)
    wb = w_branch_b.astype(BF16)
    wo = w_out.astype(BF16)
    hist_s = jnp.pad(state_pool, ((0, 0), (0, 0), (HIST_ROWS - POOL_HIST, 0), (0, 0)))
    hist_s = hist_s.reshape(depth, ns * HIST_ROWS, POOL_WIDTH)

    bias_p = _bias_table(rel_bias, q_rows=Q_BLOCK, keys=KEY_BLOCKS * Q_BLOCK,
                         offset=(KEY_BLOCKS - 1) * Q_BLOCK, visible_fn=_prompt_visible)
    bias_s = _bias_table(rel_bias, q_rows=t, keys=SAMPLE_KEYS_PAD, offset=past,
                         visible_fn=lambda qi, kj: kj < past + t)

    k_upd = v_upd = None
    pool_p, kp_l, vp_l, pool_s = [], [], [], []
    for l in range(depth):
        (p, q, k, v, kf, vf, gates,
         p_s, q_s, kf_s, vf_s, gates_s) = _in_proj(xp, xs, g1, w_mixed, w_gate, bg, qg, kg, layer=l, tm=PROMPT_TM)

        attn = _attn_prompt(q, k, v, bias_p, layer=l, n=nb, s=seq)
        pool_p.append(p.reshape(nb, seq, POOL_WIDTH)[:, seq - POOL_HIST:])
        kp_l.append(kf.reshape(nb, seq, N_HEADS, HEAD_DIM)[:, seq - BAND_PAST:])
        vp_l.append(vf.reshape(nb, seq, N_HEADS, HEAD_DIM)[:, seq - BAND_PAST:])
        merge_p = functools.partial(_merge, xp, p, p, attn, gates, pw, ps, wa, wb, wo, layer=l, tm=MERGE_TM,
                                    nseq=1, t=MERGE_TM, blocks_per_seq=blocks_per_seq, pos0=None,
                                    hist_spec=prompt_hist_spec)
        if l == 0:
            xp, k_upd, v_upd = merge_p(caches=(cache_k, cache_v), new_frames=t)
        else:
            xp = merge_p()

        attn_s, k_upd, v_upd = _attn_sample(q_s, kf_s, vf_s, cache_k, cache_v, bias_s, k_upd, v_upd,
                                            layer=l, n=ns, t=t)
        pool_s.append(p_s.reshape(ns, t, POOL_WIDTH)[:, t - POOL_HIST:])
        xs = _merge(xs, p_s, hist_s[l], attn_s, gates_s, pw, ps, wa, wb, wo, layer=l, tm=ns * t, nseq=ns, t=t,
                    blocks_per_seq=None, pos0=PAST_LEN, hist_spec=sample_hist_spec)

        xp, xs = _ffn(xp, xs, g2, w_up, w_down, layer=l, tm=PROMPT_TM)

    return (xp.reshape(nb, seq, D_MODEL), xs.reshape(ns, t, D_MODEL),
            jnp.stack(pool_p), jnp.stack(kp_l), jnp.stack(vp_l),
            jnp.stack(pool_s), k_upd.reshape(cache_k.shape), v_upd.reshape(cache_v.shape))
```

```python
import functools

import numpy as np

import jax
import jax.numpy as jnp
from jax import lax
from jax.experimental import pallas as pl
from jax.experimental.pallas import tpu as pltpu

D_MODEL = 2048
PAST_LEN = 1024
CHUNK = 64
N_PREV_CHUNKS = 8
BAND_PAST = N_PREV_CHUNKS * CHUNK
POOL_WIDTH = D_MODEL // 2
POOL_WINDOWS = (2, 4, 8, 16)
POOL_GROUP = POOL_WIDTH // len(POOL_WINDOWS)
POOL_HIST = max(POOL_WINDOWS) - 1
HEAD_DIM = 128
ATTN_WIDTH = D_MODEL // 2
N_HEADS = ATTN_WIDTH // HEAD_DIM
MAX_REL = 128
D_FF = 4 * D_MODEL
EPS = 1e-6
ATTN_SCALE = HEAD_DIM ** -0.5
LOG2E = 1.4426950408889634
Q_SCALE = ATTN_SCALE * LOG2E
NEG_INF = -1e30
IN_WIDTH = POOL_WIDTH + 3 * ATTN_WIDTH + 2 * D_MODEL

BF16 = jnp.bfloat16
F32 = jnp.float32

VMEM_LIMIT_BYTES = 63 * 1024 * 1024
LANES = 128

HIST_ROWS = 16
POOL_PAD = 8
Q_BLOCK = 4 * CHUNK
KEY_BLOCKS = 3
SAMPLE_KEYS_PAD = 640


def _rms(xf, g):
    ms = jnp.mean(xf * xf, axis=-1, keepdims=True)
    return xf * lax.rsqrt(ms + EPS) * g


def _dot(a, b):
    return jnp.dot(a, b, preferred_element_type=F32)


def _dot_t(a, b):
    return lax.dot_general(a, b, (((1,), (1,)), ((), ())), preferred_element_type=F32)


def _params(semantics):
    return pltpu.CompilerParams(dimension_semantics=semantics, vmem_limit_bytes=VMEM_LIMIT_BYTES)


def _layer_spec(shape, layer):
    nd = len(shape)
    return pl.BlockSpec((None,) + tuple(shape), lambda *_: (layer,) + (0,) * nd,
                        pipeline_mode=pl.Buffered(1))


def _bias_table_kernel(ext_ref, o_ref, *, q_rows, keys, visible_fn):
    width = ext_ref.shape[-1]
    qi = lax.broadcasted_iota(jnp.int32, (q_rows, keys), 0)
    kj = lax.broadcasted_iota(jnp.int32, (q_rows, keys), 1)
    visible = visible_fn(qi, kj)
    for g in range(ext_ref.shape[0]):
        row = jnp.broadcast_to(ext_ref[g:g + 1, :], (q_rows, width))
        skew = pltpu.roll(row, 0, 1, stride=1, stride_axis=0)
        o_ref[g] = jnp.where(visible, skew[:, :keys] * LOG2E, NEG_INF)


def _take_static(table, idx):
    pieces = []
    start = 0
    while start < len(idx):
        end = start + 1
        step = int(idx[end] - idx[start]) if end < len(idx) else 0
        if step in (-1, 0, 1):
            while end < len(idx) and int(idx[end] - idx[end - 1]) == step:
                end += 1
        lo, hi = sorted((int(idx[start]), int(idx[end - 1])))
        if step == 0 or end - start == 1:
            pieces.append(jnp.broadcast_to(table[:, lo:lo + 1], (table.shape[0], end - start)))
        else:
            ramp = table[:, lo:hi + 1]
            pieces.append(ramp if step == 1 else ramp[:, ::-1])
        start = end
    return jnp.concatenate(pieces, axis=1)


def _bias_table(rel_bias, *, q_rows, keys, offset, visible_fn):
    groups = rel_bias.shape[0] * rel_bias.shape[1]
    width = -(-(keys + q_rows - 1) // LANES) * LANES
    idx = np.zeros((width,), np.int32)
    idx[:keys] = np.clip(offset - np.arange(keys), -MAX_REL, MAX_REL) + MAX_REL
    back = np.arange(1, q_rows)
    idx[width - back] = np.clip(offset + back, -MAX_REL, MAX_REL) + MAX_REL
    ext = _take_static(rel_bias.reshape(groups, 2 * MAX_REL + 1).astype(F32), idx)
    return pl.pallas_call(
        functools.partial(_bias_table_kernel, q_rows=q_rows, keys=keys, visible_fn=visible_fn),
        out_shape=jax.ShapeDtypeStruct((groups, q_rows, keys), F32),
        compiler_params=pltpu.CompilerParams(vmem_limit_bytes=VMEM_LIMIT_BYTES),
        name="bias_table",
    )(ext)


def _prompt_visible(qi, kj):
    own = qi // CHUNK + (KEY_BLOCKS - 1) * (Q_BLOCK // CHUNK)
    kc = kj // CHUNK
    return (kc <= own) & (kc >= own - N_PREV_CHUNKS)


IN_HEADS = 2
IN_STEPS = N_HEADS // IN_HEADS
IN_SEG_TN = IN_HEADS * HEAD_DIM
IN_GATE_TN = 2 * D_MODEL // IN_STEPS


def _regroup_kernel(wq_ref, wk_ref, wp_ref, wv_ref, wg_ref, mixed_ref, gate_ref):
    for s, src in enumerate((wq_ref, wk_ref, wp_ref, wv_ref)):
        mixed_ref[:, s * IN_SEG_TN:(s + 1) * IN_SEG_TN] = src[...].astype(BF16)
    gate_ref[...] = wg_ref[...].astype(BF16)


def _regroup_in_weights(w_in):
    depth = w_in.shape[0]

    def head_cols(seg):
        return pl.BlockSpec((None, D_MODEL, IN_SEG_TN), lambda l, j: (l, 0, seg * IN_STEPS + j))

    gate0 = 4 * ATTN_WIDTH // IN_GATE_TN
    return pl.pallas_call(
        _regroup_kernel,
        grid=(depth, IN_STEPS),
        in_specs=[head_cols(1), head_cols(2), head_cols(0), head_cols(3),
                  pl.BlockSpec((None, D_MODEL, IN_GATE_TN), lambda l, j: (l, 0, gate0 + j))],
        out_specs=(pl.BlockSpec((None, D_MODEL, 4 * IN_SEG_TN), lambda l, j: (l, 0, j)),
                   pl.BlockSpec((None, D_MODEL, IN_GATE_TN), lambda l, j: (l, 0, j))),
        out_shape=(jax.ShapeDtypeStruct((depth, D_MODEL, 4 * ATTN_WIDTH), BF16),
                   jax.ShapeDtypeStruct((depth, D_MODEL, 2 * D_MODEL), BF16)),
        compiler_params=_params(("parallel", "parallel")),
        name="regroup_w_in",
    )(w_in, w_in, w_in, w_in, w_in)


def _in_rows(h, wm_ref, wg_ref, bg_ref, qg_ref, kg_ref, p_ref, q_ref, kf_ref, vf_ref, gate_ref,
             k_ref=None, v_ref=None):
    gate_ref[...] = jax.nn.sigmoid(_dot(h, wg_ref[...]) + bg_ref[...])
    z = _dot(h, wm_ref[...])
    for hd in range(IN_HEADS):
        sl = slice(hd * HEAD_DIM, (hd + 1) * HEAD_DIM)
        q_ref[:, sl] = (_rms(z[:, sl], qg_ref[...]) * Q_SCALE).astype(BF16)
        kn = _rms(z[:, IN_SEG_TN + hd * HEAD_DIM:IN_SEG_TN + (hd + 1) * HEAD_DIM], kg_ref[...])
        kf_ref[:, sl] = kn
        if k_ref is not None:
            k_ref[:, sl] = kn.astype(BF16)
    p_ref[...] = z[:, 2 * IN_SEG_TN:3 * IN_SEG_TN]
    v = z[:, 3 * IN_SEG_TN:4 * IN_SEG_TN]
    vf_ref[...] = v
    if v_ref is not None:
        v_ref[...] = v.astype(BF16)


def _in_kernel(x_ref, xs_ref, g1_ref, wm_ref, wg_ref, bg_ref, qg_ref, kg_ref,
               p_ref, q_ref, k_ref, v_ref, kf_ref, vf_ref, gate_ref,
               ps_ref, qs_ref, kfs_ref, vfs_ref, gates_ref, h_scr, hs_scr):
    i = pl.program_id(0)
    j = pl.program_id(1)
    weights = (wm_ref, wg_ref, bg_ref, qg_ref, kg_ref)

    @pl.when(j == 0)
    def _():
        h_scr[...] = _rms(x_ref[...], g1_ref[...]).astype(BF16)

    @pl.when((i == 0) & (j == 0))
    def _():
        hs_scr[...] = _rms(xs_ref[...], g1_ref[...]).astype(BF16)

    _in_rows(h_scr[...], *weights, p_ref, q_ref, kf_ref, vf_ref, gate_ref, k_ref, v_ref)

    @pl.when(i == 0)
    def _():
        _in_rows(hs_scr[...], *weights, ps_ref, qs_ref, kfs_ref, vfs_ref, gates_ref)


def _in_proj(x, xs, g1, w_mixed, w_gate, b_gate, qg, kg, *, layer, tm):
    m = x.shape[0]
    ms = xs.shape[0]
    head_spec = pl.BlockSpec((tm, IN_SEG_TN), lambda i, j: (i, j))

    def sample_spec(width):
        return pl.BlockSpec((ms, width), lambda i, j: (0, jnp.where(i == 0, j, IN_STEPS - 1)))

    out_shape = (
        jax.ShapeDtypeStruct((m, POOL_WIDTH), F32),
        jax.ShapeDtypeStruct((m, ATTN_WIDTH), BF16),
        jax.ShapeDtypeStruct((m, ATTN_WIDTH), BF16),
        jax.ShapeDtypeStruct((m, ATTN_WIDTH), BF16),
        jax.ShapeDtypeStruct((m, ATTN_WIDTH), F32),
        jax.ShapeDtypeStruct((m, ATTN_WIDTH), F32),
        jax.ShapeDtypeStruct((m, 2 * D_MODEL), F32),
        jax.ShapeDtypeStruct((ms, POOL_WIDTH), F32),
        jax.ShapeDtypeStruct((ms, ATTN_WIDTH), BF16),
        jax.ShapeDtypeStruct((ms, ATTN_WIDTH), F32),
        jax.ShapeDtypeStruct((ms, ATTN_WIDTH), F32),
        jax.ShapeDtypeStruct((ms, 2 * D_MODEL), F32),
    )
    return pl.pallas_call(
        _in_kernel,
        grid=(m // tm, IN_STEPS),
        in_specs=[
            pl.BlockSpec((tm, D_MODEL), lambda i, j: (i, 0)),
            pl.BlockSpec((ms, D_MODEL), lambda i, j: (0, 0)),
            pl.BlockSpec((None, 1, D_MODEL), lambda i, j: (layer, 0, 0)),
            pl.BlockSpec((None, D_MODEL, 4 * IN_SEG_TN), lambda i, j: (layer, 0, j)),
            pl.BlockSpec((None, D_MODEL, IN_GATE_TN), lambda i, j: (layer, 0, j)),
            pl.BlockSpec((None, 1, IN_GATE_TN), lambda i, j: (layer, 0, j)),
            pl.BlockSpec((None, 1, HEAD_DIM), lambda i, j: (layer, 0, 0)),
            pl.BlockSpec((None, 1, HEAD_DIM), lambda i, j: (layer, 0, 0)),
        ],
        out_specs=(head_spec,) * 6 + (pl.BlockSpec((tm, IN_GATE_TN), lambda i, j: (i, j)),)
                  + (sample_spec(IN_SEG_TN),) * 4 + (sample_spec(IN_GATE_TN),),
        out_shape=out_shape,
        scratch_shapes=[pltpu.VMEM((tm, D_MODEL), BF16), pltpu.VMEM((ms, D_MODEL), BF16)],
        compiler_params=_params(("arbitrary", "arbitrary")),
        name="in_proj",
    )(x, xs, g1, w_mixed, w_gate, b_gate, qg, kg)


def _attn_prompt_kernel(q_ref, k_ref, v_ref, bias_ref, o_ref):
    i = pl.program_id(1)

    def heads(n_valid):
        for h in range(N_HEADS):
            sl = slice(h * HEAD_DIM, (h + 1) * HEAD_DIM)
            qh = q_ref[:, sl]
            m = l = o = None
            for r in range(n_valid):
                rows = slice(r * Q_BLOCK, (r + 1) * Q_BLOCK)
                d = KEY_BLOCKS - n_valid + r
                s = _dot_t(qh, k_ref[rows, sl]) + bias_ref[h, :, d * Q_BLOCK:(d + 1) * Q_BLOCK]
                m_r = jnp.max(s, axis=-1, keepdims=True)
                if m is None:
                    m = m_r
                    e = jnp.exp2(s - m)
                    l = jnp.sum(e, axis=-1, keepdims=True)
                    o = _dot(e.astype(BF16), v_ref[rows, sl])
                else:
                    m_new = jnp.maximum(m, m_r)
                    alpha = jnp.exp2(m - m_new)
                    e = jnp.exp2(s - m_new)
                    l = alpha * l + jnp.sum(e, axis=-1, keepdims=True)
                    o = alpha * o + _dot(e.astype(BF16), v_ref[rows, sl])
                    m = m_new
            o_ref[:, sl] = (o / l).astype(BF16)

    for n_valid in range(1, KEY_BLOCKS):
        pl.when(i == n_valid - 1)(functools.partial(heads, n_valid))
    pl.when(i >= KEY_BLOCKS - 1)(functools.partial(heads, KEY_BLOCKS))


def _attn_prompt(q, k, v, bias, *, layer, n, s):
    nb = s // Q_BLOCK
    q_spec = pl.BlockSpec((Q_BLOCK, ATTN_WIDTH), lambda b, i: (b * nb + i, 0))

    kv_spec = pl.BlockSpec(
        (pl.Element(KEY_BLOCKS * Q_BLOCK), pl.Element(ATTN_WIDTH)),
        lambda b, i: ((b * nb + jnp.maximum(i - (KEY_BLOCKS - 1), 0)) * Q_BLOCK, 0))
    bias_spec = pl.BlockSpec((N_HEADS, Q_BLOCK, KEY_BLOCKS * Q_BLOCK), lambda b, i: (layer, 0, 0),
                             pipeline_mode=pl.Buffered(1))
    return pl.pallas_call(
        _attn_prompt_kernel,
        grid=(n, nb),
        in_specs=[q_spec, kv_spec, kv_spec, bias_spec],
        out_specs=q_spec,
        out_shape=jax.ShapeDtypeStruct((n * s, ATTN_WIDTH), BF16),
        compiler_params=_params(("parallel", "arbitrary")),
        name="attn_prompt",
    )(q, k, v, bias)


def _shift_cache_block(src_ref, dst_ref, *, new_rows):
    keep = src_ref.shape[0] - new_rows
    dst_ref[0:keep, :] = src_ref[new_rows:, :]
    dst_ref[keep:, :] = jnp.zeros((new_rows, HEAD_DIM), F32)


def _attn_sample_kernel(q_ref, kf_ref, vf_ref, ck_ref, cv_ref, bias_ref, k_alias, v_alias,
                        o_ref, ko_ref, vo_ref, kk_scr, vv_scr, *, t, past):
    del k_alias, v_alias
    for new_ref, cache_ref, out_ref, scr in ((kf_ref, ck_ref, ko_ref, kk_scr),
                                             (vf_ref, cv_ref, vo_ref, vv_scr)):
        for h in range(N_HEADS):
            sl = slice(h * HEAD_DIM, (h + 1) * HEAD_DIM)
            out_ref[pl.ds(h, t, stride=N_HEADS), :] = new_ref[:, sl]
            scr[0:past, sl] = cache_ref[pl.ds(h, past, stride=N_HEADS), :].astype(BF16)
        scr[past:past + t, :] = new_ref[...].astype(BF16)
        scr[past + t:, :] = jnp.zeros((SAMPLE_KEYS_PAD - past - t, ATTN_WIDTH), BF16)
    for h in range(N_HEADS):
        sl = slice(h * HEAD_DIM, (h + 1) * HEAD_DIM)
        s = _dot_t(q_ref[:, sl], kk_scr[:, sl]) + bias_ref[h]
        m = jnp.max(s, axis=-1, keepdims=True)
        e = jnp.exp2(s - m)
        l = jnp.sum(e, axis=-1, keepdims=True)
        o = _dot(e.astype(BF16), vv_scr[:, sl])
        o_ref[:, sl] = (o / l).astype(BF16)


def _attn_sample(q, kf, vf, cache_k, cache_v, bias, k_upd, v_upd, *, layer, n, t):
    depth, _, past = cache_k.shape[:3]
    rows = past * N_HEADS
    new_rows = t * N_HEADS
    row_spec = pl.BlockSpec((t, ATTN_WIDTH), lambda b: (b, 0))
    cache_spec = pl.BlockSpec((None, None, rows, HEAD_DIM), lambda b: (layer, b, 0, 0))
    new_spec = pl.BlockSpec((None, None, new_rows, HEAD_DIM), lambda b: (layer, b, rows // new_rows - 1, 0))
    bias_spec = pl.BlockSpec((N_HEADS, t, SAMPLE_KEYS_PAD), lambda b: (layer, 0, 0),
                             pipeline_mode=pl.Buffered(1))
    any_spec = pl.BlockSpec(memory_space=pl.ANY)
    upd = jax.ShapeDtypeStruct(k_upd.shape, F32)
    return pl.pallas_call(
        functools.partial(_attn_sample_kernel, t=t, past=past),
        grid=(n,),
        in_specs=[row_spec, row_spec, row_spec, cache_spec, cache_spec, bias_spec, any_spec, any_spec],
        out_specs=(row_spec, new_spec, new_spec),
        out_shape=(jax.ShapeDtypeStruct((n * t, ATTN_WIDTH), BF16), upd, upd),
        input_output_aliases={6: 1, 7: 2},
        scratch_shapes=[pltpu.VMEM((SAMPLE_KEYS_PAD, ATTN_WIDTH), BF16),
                        pltpu.VMEM((SAMPLE_KEYS_PAD, ATTN_WIDTH), BF16)],
        compiler_params=_params(("arbitrary",)),
        name="attn_sample",
    )(q, kf, vf, cache_k.reshape(depth, n, rows, HEAD_DIM), cache_v.reshape(depth, n, rows, HEAD_DIM),
      bias, k_upd, v_upd)


def _merge_kernel(*refs, nseq, t, blocks_per_seq, pos0, cache_blocks, new_rows):
    x_ref, p_ref, hist_ref, a_ref, gate_ref, pw_ref, ps_ref, wa_ref, wb_ref, wo_ref = refs[:10]
    i = pl.program_id(0)
    if cache_blocks:
        ck_ref, cv_ref, o_ref, ko_ref, vo_ref, xp_scr, stage_scr, d_scr = refs[10:]
        pl.when(i < cache_blocks)(
            functools.partial(_shift_cache_block, ck_ref, ko_ref, new_rows=new_rows))
        pl.when((i >= cache_blocks) & (i < 2 * cache_blocks))(
            functools.partial(_shift_cache_block, cv_ref, vo_ref, new_rows=new_rows))
    else:
        o_ref, xp_scr, stage_scr, d_scr = refs[10:]
    row = lax.broadcasted_iota(jnp.int32, (t, 1), 0)
    if blocks_per_seq is None:
        pos = pos0 + row
    else:
        pos = (i % blocks_per_seq) * t + row
    gated_b = gate_ref[:, D_MODEL:] * _dot(a_ref[...], wb_ref[...])

    t0 = POOL_PAD + HIST_ROWS
    xp_scr[0:POOL_PAD, :] = jnp.zeros((POOL_PAD, POOL_WIDTH), F32)
    stage_scr[:, 0:POOL_PAD, :] = jnp.zeros((2, POOL_PAD, POOL_GROUP), F32)
    for s in range(nseq):
        rows = slice(s * t, (s + 1) * t)
        hist = hist_ref[s * HIST_ROWS:(s + 1) * HIST_ROWS, :]
        if blocks_per_seq is not None:
            hist = jnp.where(i % blocks_per_seq == 0, 0.0, hist)
        xp_scr[POOL_PAD:t0, :] = hist
        xp_scr[t0:, :] = p_ref[rows, :]
        for gi, w in enumerate(POOL_WINDOWS):
            cols = slice(gi * POOL_GROUP, (gi + 1) * POOL_GROUP)
            src = xp_scr.at[:, cols]
            shift = 1
            while 2 * shift < w:
                dst = stage_scr.at[(shift.bit_length() - 1) % 2]
                dst[POOL_PAD:, :] = src[POOL_PAD:, :] + src[POOL_PAD - shift:t0 + t - shift, :]
                src = dst
                shift *= 2
            acc = src[t0:, :] + src[t0 - shift:t0 + t - shift, :]
            inv_cnt = 1.0 / jnp.minimum(pos + 1, w).astype(F32)
            d_scr[rows, cols] = (acc * inv_cnt - p_ref[rows, cols]).astype(BF16)
    pooled = jnp.concatenate(
        [_dot(d_scr[:, gi * POOL_GROUP:(gi + 1) * POOL_GROUP], pw_ref[gi]) for gi in range(len(POOL_WINDOWS))],
        axis=-1) * ps_ref[...]
    branch_a = _dot(pooled.astype(BF16), wa_ref[...])
    merged = gate_ref[:, :D_MODEL] * branch_a + gated_b
    o_ref[...] = x_ref[...] + _dot(merged.astype(BF16), wo_ref[...])


def _merge(x, p, hist, attn, gates, pool_w, pool_scale, wa, wb, wo, *, layer, tm, nseq, t, blocks_per_seq, pos0,
           hist_spec, caches=None, new_frames=0):
    m = x.shape[0]
    steps = m // tm

    def row_spec(width):
        return pl.BlockSpec((tm, width), lambda i: (i, 0))

    operands = [x, p, hist, attn, gates, pool_w, pool_scale, wa, wb, wo]
    in_specs = [row_spec(D_MODEL), row_spec(POOL_WIDTH), hist_spec, row_spec(ATTN_WIDTH),
                row_spec(2 * D_MODEL),
                _layer_spec(pool_w.shape[1:], layer), _layer_spec(pool_scale.shape[1:], layer),
                _layer_spec(wa.shape[1:], layer), _layer_spec(wb.shape[1:], layer),
                _layer_spec(wo.shape[1:], layer)]
    out_specs = [row_spec(D_MODEL)]
    out_shape = [jax.ShapeDtypeStruct((m, D_MODEL), F32)]
    cache_blocks = 0
    if caches is not None:
        depth, n, past = caches[0].shape[:3]
        cache_blocks = depth * n
        assert steps >= 2 * cache_blocks
        rows = past * N_HEADS
        k_spec = pl.BlockSpec((None, rows, HEAD_DIM), lambda i: (jnp.minimum(i, cache_blocks - 1), 0, 0))
        v_spec = pl.BlockSpec((None, rows, HEAD_DIM),
                              lambda i: (jnp.clip(i - cache_blocks, 0, cache_blocks - 1), 0, 0))
        operands += [c.reshape(cache_blocks, rows, HEAD_DIM) for c in caches]
        in_specs += [k_spec, v_spec]
        out_specs += [k_spec, v_spec]
        out_shape += [jax.ShapeDtypeStruct((cache_blocks, rows, HEAD_DIM), F32)] * 2
    outs = pl.pallas_call(
        functools.partial(_merge_kernel, nseq=nseq, t=t, blocks_per_seq=blocks_per_seq, pos0=pos0,
                          cache_blocks=cache_blocks, new_rows=new_frames * N_HEADS),
        grid=(steps,),
        in_specs=in_specs,
        out_specs=tuple(out_specs),
        out_shape=tuple(out_shape),
        scratch_shapes=[pltpu.VMEM((POOL_PAD + HIST_ROWS + t, POOL_WIDTH), F32),
                        pltpu.VMEM((2, POOL_PAD + HIST_ROWS + t, POOL_GROUP), F32),
                        pltpu.VMEM((tm, POOL_WIDTH), BF16)],
        compiler_params=_params(("arbitrary",)),
        name="merge",
    )(*operands)
    if caches is None:
        return outs[0]
    return outs[0], outs[1].reshape(depth, n, rows, HEAD_DIM), outs[2].reshape(depth, n, rows, HEAD_DIM)


FFN_TF = 512


def _ffn_kernel(x_ref, xs_ref, g2_ref, wu_ref, wd_ref, o_ref, os_ref, h_scr, hs_scr):
    i = pl.program_id(0)
    j = pl.program_id(1)

    def start(src_ref, acc_ref, scr):
        x = src_ref[...]
        scr[...] = _rms(x, g2_ref[...]).astype(BF16)
        acc_ref[...] = x

    def step(acc_ref, scr):
        u = jnp.square(jnp.maximum(_dot(scr[...], wu_ref[...].astype(BF16)), 0.0))
        acc_ref[...] += _dot(u.astype(BF16), wd_ref[...].astype(BF16))

    pl.when(j == 0)(functools.partial(start, x_ref, o_ref, h_scr))
    pl.when((i == 0) & (j == 0))(functools.partial(start, xs_ref, os_ref, hs_scr))
    step(o_ref, h_scr)
    pl.when(i == 0)(functools.partial(step, os_ref, hs_scr))


def _ffn(x, xs, g2, wu, wd, *, layer, tm):
    m = x.shape[0]
    ms = xs.shape[0]
    sample_spec = pl.BlockSpec((ms, D_MODEL), lambda i, j: (0, 0))
    return pl.pallas_call(
        _ffn_kernel,
        grid=(m // tm, D_FF // FFN_TF),
        in_specs=[pl.BlockSpec((tm, D_MODEL), lambda i, j: (i, 0)),
                  sample_spec,
                  pl.BlockSpec((None, 1, D_MODEL), lambda i, j: (layer, 0, 0)),
                  pl.BlockSpec((None, D_MODEL, FFN_TF), lambda i, j: (layer, 0, j)),
                  pl.BlockSpec((None, FFN_TF, D_MODEL), lambda i, j: (layer, j, 0))],
        out_specs=(pl.BlockSpec((tm, D_MODEL), lambda i, j: (i, 0)), sample_spec),
        out_shape=(jax.ShapeDtypeStruct((m, D_MODEL), F32), jax.ShapeDtypeStruct((ms, D_MODEL), F32)),
        scratch_shapes=[pltpu.VMEM((tm, D_MODEL), BF16), pltpu.VMEM((ms, D_MODEL), BF16)],
        compiler_params=_params(("arbitrary", "arbitrary")),
        name="ffn",
    )(x, xs, g2, wu, wd)


PROMPT_TM = 1024
MERGE_TM = 256


def kernel(x_prompt, x_sample, state_pool, cache_k, cache_v, norm1_g, w_in, b_gate, pool_w, pool_scale,
           q_norm_g, k_norm_g, rel_bias, w_branch_a, w_branch_b, w_out, norm2_g, w_up, w_down):
    nb, seq, _ = x_prompt.shape
    ns, t, _ = x_sample.shape
    depth = w_in.shape[0]
    past = cache_k.shape[2]
    assert seq % PROMPT_TM == 0 and seq % MERGE_TM == 0 and seq % Q_BLOCK == 0 and seq >= BAND_PAST
    assert t >= POOL_HIST and t % HIST_ROWS == 0 and past + t <= SAMPLE_KEYS_PAD and past <= BAND_PAST
    assert past % t == 0 and past > t

    xp = x_prompt.reshape(nb * seq, D_MODEL)
    xs = x_sample.reshape(ns * t, D_MODEL)
    blocks_per_seq = seq // MERGE_TM
    prompt_hist_spec = pl.BlockSpec(
        (HIST_ROWS, POOL_WIDTH), lambda i: (jnp.maximum(i * (MERGE_TM // HIST_ROWS) - 1, 0), 0))
    sample_hist_spec = pl.BlockSpec((ns * HIST_ROWS, POOL_WIDTH), lambda i: (0, 0))

    g1 = norm1_g.reshape(depth, 1, D_MODEL)
    g2 = norm2_g.reshape(depth, 1, D_MODEL)
    bg = b_gate.reshape(depth, 1, 2 * D_MODEL)
    qg = q_norm_g.reshape(depth, 1, HEAD_DIM)
    kg = k_norm_g.reshape(depth, 1, HEAD_DIM)
    ps = pool_scale.reshape(depth, 1, POOL_WIDTH)
    w_mixed, w_gate = _regroup_in_weights(w_in)
    pw = pool_w.astype(BF16)
    wa = w_branch_a.astype(BF16)
    wb = w_branch_b.astype(BF16)
    wo = w_out.astype(BF16)
    hist_s = jnp.pad(state_pool, ((0, 0), (0, 0), (HIST_ROWS - POOL_HIST, 0), (0, 0)))
    hist_s = hist_s.reshape(depth, ns * HIST_ROWS, POOL_WIDTH)

    bias_p = _bias_table(rel_bias, q_rows=Q_BLOCK, keys=KEY_BLOCKS * Q_BLOCK,
                         offset=(KEY_BLOCKS - 1) * Q_BLOCK, visible_fn=_prompt_visible)
    bias_s = _bias_table(rel_bias, q_rows=t, keys=SAMPLE_KEYS_PAD, offset=past,
                         visible_fn=lambda qi, kj: kj < past + t)

    k_upd = v_upd = None
    pool_p, kp_l, vp_l, pool_s = [], [], [], []
    for l in range(depth):
        (p, q, k, v, kf, vf, gates,
         p_s, q_s, kf_s, vf_s, gates_s) = _in_proj(xp, xs, g1, w_mixed, w_gate, bg, qg, kg, layer=l, tm=PROMPT_TM)

        attn = _attn_prompt(q, k, v, bias_p, layer=l, n=nb, s=seq)
        pool_p.append(p.reshape(nb, seq, POOL_WIDTH)[:, seq - POOL_HIST:])
        kp_l.append(kf.reshape(nb, seq, N_HEADS, HEAD_DIM)[:, seq - BAND_PAST:])
        vp_l.append(vf.reshape(nb, seq, N_HEADS, HEAD_DIM)[:, seq - BAND_PAST:])
        merge_p = functools.partial(_merge, xp, p, p, attn, gates, pw, ps, wa, wb, wo, layer=l, tm=MERGE_TM,
                                    nseq=1, t=MERGE_TM, blocks_per_seq=blocks_per_seq, pos0=None,
                                    hist_spec=prompt_hist_spec)
        if l == 0:
            xp, k_upd, v_upd = merge_p(caches=(cache_k, cache_v), new_frames=t)
        else:
            xp = merge_p()

        attn_s, k_upd, v_upd = _attn_sample(q_s, kf_s, vf_s, cache_k, cache_v, bias_s, k_upd, v_upd,
                                            layer=l, n=ns, t=t)
        pool_s.append(p_s.reshape(ns, t, POOL_WIDTH)[:, t - POOL_HIST:])
        xs = _merge(xs, p_s, hist_s[l], attn_s, gates_s, pw, ps, wa, wb, wo, layer=l, tm=ns * t, nseq=ns, t=t,
                    blocks_per_seq=None, pos0=PAST_LEN, hist_spec=sample_hist_spec)

        xp, xs = _ffn(xp, xs, g2, w_up, w_down, layer=l, tm=PROMPT_TM)

    return (xp.reshape(nb, seq, D_MODEL), xs.reshape(ns, t, D_MODEL),
            jnp.stack(pool_p), jnp.stack(kp_l), jnp.stack(vp_l),
            jnp.stack(pool_s), k_upd.reshape(cache_k.shape), v_upd.reshape(cache_v.shape))
```

```python
import functools

import numpy as np

import jax
import jax.numpy as jnp
from jax import lax
from jax.experimental import pallas as pl
from jax.experimental.pallas import tpu as pltpu

D_MODEL = 2048
PAST_LEN = 1024
CHUNK = 64
N_PREV_CHUNKS = 8
BAND_PAST = N_PREV_CHUNKS * CHUNK
POOL_WIDTH = D_MODEL // 2
POOL_WINDOWS = (2, 4, 8, 16)
POOL_GROUP = POOL_WIDTH // len(POOL_WINDOWS)
POOL_HIST = max(POOL_WINDOWS) - 1
HEAD_DIM = 128
ATTN_WIDTH = D_MODEL // 2
N_HEADS = ATTN_WIDTH // HEAD_DIM
MAX_REL = 128
D_FF = 4 * D_MODEL
EPS = 1e-6
ATTN_SCALE = HEAD_DIM ** -0.5
LOG2E = 1.4426950408889634
Q_SCALE = ATTN_SCALE * LOG2E
NEG_INF = -1e30
IN_WIDTH = POOL_WIDTH + 3 * ATTN_WIDTH + 2 * D_MODEL

BF16 = jnp.bfloat16
F32 = jnp.float32

VMEM_LIMIT_BYTES = 63 * 1024 * 1024
LANES = 128

HIST_ROWS = 16
POOL_PAD = 8
Q_BLOCK = 4 * CHUNK
KEY_BLOCKS = 3
SAMPLE_KEYS_PAD = 640


def _rms(xf, g):
    ms = jnp.mean(xf * xf, axis=-1, keepdims=True)
    return xf * lax.rsqrt(ms + EPS) * g


def _dot(a, b):
    return jnp.dot(a, b, preferred_element_type=F32)


def _dot_t(a, b):
    return lax.dot_general(a, b, (((1,), (1,)), ((), ())), preferred_element_type=F32)


def _params(semantics):
    return pltpu.CompilerParams(dimension_semantics=semantics, vmem_limit_bytes=VMEM_LIMIT_BYTES)


def _layer_spec(shape, layer):
    nd = len(shape)
    return pl.BlockSpec((None,) + tuple(shape), lambda *_: (layer,) + (0,) * nd,
                        pipeline_mode=pl.Buffered(1))


def _bias_table_kernel(ext_ref, o_ref, *, q_rows, keys, visible_fn):
    width = ext_ref.shape[-1]
    qi = lax.broadcasted_iota(jnp.int32, (q_rows, keys), 0)
    kj = lax.broadcasted_iota(jnp.int32, (q_rows, keys), 1)
    visible = visible_fn(qi, kj)
    for g in range(ext_ref.shape[0]):
        row = jnp.broadcast_to(ext_ref[g:g + 1, :], (q_rows, width))
        skew = pltpu.roll(row, 0, 1, stride=1, stride_axis=0)
        o_ref[g] = jnp.where(visible, skew[:, :keys] * LOG2E, NEG_INF)


def _take_static(table, idx):
    pieces = []
    start = 0
    while start < len(idx):
        end = start + 1
        step = int(idx[end] - idx[start]) if end < len(idx) else 0
        if step in (-1, 0, 1):
            while end < len(idx) and int(idx[end] - idx[end - 1]) == step:
                end += 1
        lo, hi = sorted((int(idx[start]), int(idx[end - 1])))
        if step == 0 or end - start == 1:
            pieces.append(jnp.broadcast_to(table[:, lo:lo + 1], (table.shape[0], end - start)))
        else:
            ramp = table[:, lo:hi + 1]
            pieces.append(ramp if step == 1 else ramp[:, ::-1])
        start = end
    return jnp.concatenate(pieces, axis=1)


def _bias_table(rel_bias, *, q_rows, keys, offset, visible_fn):
    groups = rel_bias.shape[0] * rel_bias.shape[1]
    width = -(-(keys + q_rows - 1) // LANES) * LANES
    idx = np.zeros((width,), np.int32)
    idx[:keys] = np.clip(offset - np.arange(keys), -MAX_REL, MAX_REL) + MAX_REL
    back = np.arange(1, q_rows)
    idx[width - back] = np.clip(offset + back, -MAX_REL, MAX_REL) + MAX_REL
    ext = _take_static(rel_bias.reshape(groups, 2 * MAX_REL + 1).astype(F32), idx)
    return pl.pallas_call(
        functools.partial(_bias_table_kernel, q_rows=q_rows, keys=keys, visible_fn=visible_fn),
        out_shape=jax.ShapeDtypeStruct((groups, q_rows, keys), F32),
        compiler_params=pltpu.CompilerParams(vmem_limit_bytes=VMEM_LIMIT_BYTES),
        name="bias_table",
    )(ext)


def _prompt_visible(qi, kj):
    own = qi // CHUNK + (KEY_BLOCKS - 1) * (Q_BLOCK // CHUNK)
    kc = kj // CHUNK
    return (kc <= own) & (kc >= own - N_PREV_CHUNKS)


IN_HEADS = 2
IN_STEPS = N_HEADS // IN_HEADS
IN_SEG_TN = IN_HEADS * HEAD_DIM
IN_GATE_TN = 2 * D_MODEL // IN_STEPS


IN_SEG_ORDER = (1, 2, 0, 3)
IN_SLABS = 4 * ATTN_WIDTH // HEAD_DIM


def _regroup_kernel(wq_ref, wk_ref, wp_ref, wv_ref, wg_ref, mixed_ref, gate_ref):
    for s, src in enumerate((wq_ref, wk_ref, wp_ref, wv_ref)):
        mixed_ref[:, s * IN_SEG_TN:(s + 1) * IN_SEG_TN] = src[...].astype(BF16)
    gate_ref[...] = wg_ref[...].astype(BF16)


def _regroup_in_weights(w_in, layer):
    def head_cols(seg):
        return pl.BlockSpec((None, D_MODEL, IN_SEG_TN), lambda j: (layer, 0, seg * IN_STEPS + j))

    gate0 = 4 * ATTN_WIDTH // IN_GATE_TN
    return pl.pallas_call(
        _regroup_kernel,
        grid=(IN_STEPS,),
        in_specs=[head_cols(seg) for seg in IN_SEG_ORDER]
                 + [pl.BlockSpec((None, D_MODEL, IN_GATE_TN), lambda j: (layer, 0, gate0 + j))],
        out_specs=(pl.BlockSpec((None, D_MODEL, 4 * IN_SEG_TN), lambda j: (0, 0, j)),
                   pl.BlockSpec((None, D_MODEL, IN_GATE_TN), lambda j: (0, 0, j))),
        out_shape=(jax.ShapeDtypeStruct((1, D_MODEL, 4 * ATTN_WIDTH), BF16),
                   jax.ShapeDtypeStruct((1, D_MODEL, 2 * D_MODEL), BF16)),
        compiler_params=_params(("parallel",)),
        name="regroup_w_in",
    )(w_in, w_in, w_in, w_in, w_in)


def _regroup_slab_specs(layer, step_fn):
    def mixed_src(*ids):
        s = step_fn(*ids)
        group, within = s // (4 * IN_HEADS), s % (4 * IN_HEADS)
        seg, head = within // IN_HEADS, group * IN_HEADS + within % IN_HEADS
        src_seg = sum(jnp.where(seg == pos, src, 0) for pos, src in enumerate(IN_SEG_ORDER))
        return (layer, 0, src_seg * N_HEADS + head)

    slab = (None, D_MODEL, HEAD_DIM)
    in_specs = [pl.BlockSpec(slab, mixed_src),
                pl.BlockSpec(slab, lambda *ids: (layer, 0, IN_SLABS + step_fn(*ids)))]
    out_specs = [pl.BlockSpec(slab, lambda *ids: (0, 0, step_fn(*ids)))] * 2
    return in_specs, out_specs


def _in_rows(h, wm_ref, wg_ref, bg_ref, qg_ref, kg_ref, p_ref, q_ref, kf_ref, vf_ref, gate_ref,
             k_ref=None, v_ref=None):
    gate_ref[...] = jax.nn.sigmoid(_dot(h, wg_ref[...]) + bg_ref[...])
    z = _dot(h, wm_ref[...])
    for hd in range(IN_HEADS):
        sl = slice(hd * HEAD_DIM, (hd + 1) * HEAD_DIM)
        q_ref[:, sl] = (_rms(z[:, sl], qg_ref[...]) * Q_SCALE).astype(BF16)
        kn = _rms(z[:, IN_SEG_TN + hd * HEAD_DIM:IN_SEG_TN + (hd + 1) * HEAD_DIM], kg_ref[...])
        kf_ref[:, sl] = kn
        if k_ref is not None:
            k_ref[:, sl] = kn.astype(BF16)
    p_ref[...] = z[:, 2 * IN_SEG_TN:3 * IN_SEG_TN]
    v = z[:, 3 * IN_SEG_TN:4 * IN_SEG_TN]
    vf_ref[...] = v
    if v_ref is not None:
        v_ref[...] = v.astype(BF16)


def _in_kernel(x_ref, xs_ref, g1_ref, wm_ref, wg_ref, bg_ref, qg_ref, kg_ref,
               p_ref, q_ref, k_ref, v_ref, kf_ref, vf_ref, gate_ref,
               ps_ref, qs_ref, kfs_ref, vfs_ref, gates_ref, h_scr, hs_scr):
    i = pl.program_id(0)
    j = pl.program_id(1)
    weights = (wm_ref, wg_ref, bg_ref, qg_ref, kg_ref)

    @pl.when(j == 0)
    def _():
        h_scr[...] = _rms(x_ref[...], g1_ref[...]).astype(BF16)

    @pl.when((i == 0) & (j == 0))
    def _():
        hs_scr[...] = _rms(xs_ref[...], g1_ref[...]).astype(BF16)

    _in_rows(h_scr[...], *weights, p_ref, q_ref, kf_ref, vf_ref, gate_ref, k_ref, v_ref)

    @pl.when(i == 0)
    def _():
        _in_rows(hs_scr[...], *weights, ps_ref, qs_ref, kfs_ref, vfs_ref, gates_ref)


def _in_proj(x, xs, g1, w_mixed, w_gate, b_gate, qg, kg, *, layer, tm):
    m = x.shape[0]
    ms = xs.shape[0]
    head_spec = pl.BlockSpec((tm, IN_SEG_TN), lambda i, j: (i, j))

    def sample_spec(width):
        return pl.BlockSpec((ms, width), lambda i, j: (0, jnp.where(i == 0, j, IN_STEPS - 1)))

    out_shape = (
        jax.ShapeDtypeStruct((m, POOL_WIDTH), F32),
        jax.ShapeDtypeStruct((m, ATTN_WIDTH), BF16),
        jax.ShapeDtypeStruct((m, ATTN_WIDTH), BF16),
        jax.ShapeDtypeStruct((m, ATTN_WIDTH), BF16),
        jax.ShapeDtypeStruct((m, ATTN_WIDTH), F32),
        jax.ShapeDtypeStruct((m, ATTN_WIDTH), F32),
        jax.ShapeDtypeStruct((m, 2 * D_MODEL), F32),
        jax.ShapeDtypeStruct((ms, POOL_WIDTH), F32),
        jax.ShapeDtypeStruct((ms, ATTN_WIDTH), BF16),
        jax.ShapeDtypeStruct((ms, ATTN_WIDTH), F32),
        jax.ShapeDtypeStruct((ms, ATTN_WIDTH), F32),
        jax.ShapeDtypeStruct((ms, 2 * D_MODEL), F32),
    )
    return pl.pallas_call(
        _in_kernel,
        grid=(m // tm, IN_STEPS),
        in_specs=[
            pl.BlockSpec((tm, D_MODEL), lambda i, j: (i, 0)),
            pl.BlockSpec((ms, D_MODEL), lambda i, j: (0, 0)),
            pl.BlockSpec((None, 1, D_MODEL), lambda i, j: (layer, 0, 0)),
            pl.BlockSpec((None, D_MODEL, 4 * IN_SEG_TN), lambda i, j: (0, 0, j)),
            pl.BlockSpec((None, D_MODEL, IN_GATE_TN), lambda i, j: (0, 0, j)),
            pl.BlockSpec((None, 1, IN_GATE_TN), lambda i, j: (layer, 0, j)),
            pl.BlockSpec((None, 1, HEAD_DIM), lambda i, j: (layer, 0, 0)),
            pl.BlockSpec((None, 1, HEAD_DIM), lambda i, j: (layer, 0, 0)),
        ],
        out_specs=(head_spec,) * 6 + (pl.BlockSpec((tm, IN_GATE_TN), lambda i, j: (i, j)),)
                  + (sample_spec(IN_SEG_TN),) * 4 + (sample_spec(IN_GATE_TN),),
        out_shape=out_shape,
        scratch_shapes=[pltpu.VMEM((tm, D_MODEL), BF16), pltpu.VMEM((ms, D_MODEL), BF16)],
        compiler_params=_params(("arbitrary", "arbitrary")),
        name="in_proj",
    )(x, xs, g1, w_mixed, w_gate, b_gate, qg, kg)


def _attn_prompt_kernel(q_ref, k_ref, v_ref, bias_ref, *rest):
    i = pl.program_id(1)
    if len(rest) > 1:
        wm_src, wg_src, o_ref, wm_dst, wg_dst = rest
        wm_dst[...] = wm_src[...].astype(BF16)
        wg_dst[...] = wg_src[...].astype(BF16)
    else:
        o_ref, = rest

    def heads(n_valid):
        for h in range(N_HEADS):
            sl = slice(h * HEAD_DIM, (h + 1) * HEAD_DIM)
            qh = q_ref[:, sl]
            m = l = o = None
            for r in range(n_valid):
                rows = slice(r * Q_BLOCK, (r + 1) * Q_BLOCK)
                d = KEY_BLOCKS - n_valid + r
                s = _dot_t(qh, k_ref[rows, sl]) + bias_ref[h, :, d * Q_BLOCK:(d + 1) * Q_BLOCK]
                m_r = jnp.max(s, axis=-1, keepdims=True)
                if m is None:
                    m = m_r
                    e = jnp.exp2(s - m)
                    l = jnp.sum(e, axis=-1, keepdims=True)
                    o = _dot(e.astype(BF16), v_ref[rows, sl])
                else:
                    m_new = jnp.maximum(m, m_r)
                    alpha = jnp.exp2(m - m_new)
                    e = jnp.exp2(s - m_new)
                    l = alpha * l + jnp.sum(e, axis=-1, keepdims=True)
                    o = alpha * o + _dot(e.astype(BF16), v_ref[rows, sl])
                    m = m_new
            o_ref[:, sl] = (o / l).astype(BF16)

    for n_valid in range(1, KEY_BLOCKS):
        pl.when(i == n_valid - 1)(functools.partial(heads, n_valid))
    pl.when(i >= KEY_BLOCKS - 1)(functools.partial(heads, KEY_BLOCKS))


def _attn_prompt(q, k, v, bias, *, layer, n, s, w_in=None):
    nb = s // Q_BLOCK
    q_spec = pl.BlockSpec((Q_BLOCK, ATTN_WIDTH), lambda b, i: (b * nb + i, 0))
    operands = [q, k, v, bias]
    extra_in, extra_out, extra_shape = [], [], []
    if w_in is not None:
        assert n * nb == IN_SLABS
        extra_in, extra_out = _regroup_slab_specs(layer + 1, lambda b, i: b * nb + i)
        extra_shape = [jax.ShapeDtypeStruct((1, D_MODEL, 4 * ATTN_WIDTH), BF16),
                       jax.ShapeDtypeStruct((1, D_MODEL, 2 * D_MODEL), BF16)]
        operands += [w_in, w_in]

    kv_spec = pl.BlockSpec(
        (pl.Element(KEY_BLOCKS * Q_BLOCK), pl.Element(ATTN_WIDTH)),
        lambda b, i: ((b * nb + jnp.maximum(i - (KEY_BLOCKS - 1), 0)) * Q_BLOCK, 0))
    bias_spec = pl.BlockSpec((N_HEADS, Q_BLOCK, KEY_BLOCKS * Q_BLOCK), lambda b, i: (layer, 0, 0),
                             pipeline_mode=pl.Buffered(1))
    outs = pl.pallas_call(
        _attn_prompt_kernel,
        grid=(n, nb),
        in_specs=[q_spec, kv_spec, kv_spec, bias_spec] + extra_in,
        out_specs=tuple([q_spec] + extra_out),
        out_shape=tuple([jax.ShapeDtypeStruct((n * s, ATTN_WIDTH), BF16)] + extra_shape),
        compiler_params=_params(("parallel", "arbitrary")),
        name="attn_prompt",
    )(*operands)
    return outs if w_in is not None else outs[0]


def _shift_cache_block(src_ref, dst_ref, *, new_rows):
    keep = src_ref.shape[0] - new_rows
    dst_ref[0:keep, :] = src_ref[new_rows:, :]
    dst_ref[keep:, :] = jnp.zeros((new_rows, HEAD_DIM), F32)


def _attn_sample_kernel(q_ref, kf_ref, vf_ref, ck_ref, cv_ref, bias_ref, k_alias, v_alias,
                        o_ref, ko_ref, vo_ref, kk_scr, vv_scr, *, t, past):
    del k_alias, v_alias
    for new_ref, cache_ref, out_ref, scr in ((kf_ref, ck_ref, ko_ref, kk_scr),
                                             (vf_ref, cv_ref, vo_ref, vv_scr)):
        for h in range(N_HEADS):
            sl = slice(h * HEAD_DIM, (h + 1) * HEAD_DIM)
            out_ref[pl.ds(h, t, stride=N_HEADS), :] = new_ref[:, sl]
            scr[0:past, sl] = cache_ref[pl.ds(h, past, stride=N_HEADS), :].astype(BF16)
        scr[past:past + t, :] = new_ref[...].astype(BF16)
        scr[past + t:, :] = jnp.zeros((SAMPLE_KEYS_PAD - past - t, ATTN_WIDTH), BF16)
    for h in range(N_HEADS):
        sl = slice(h * HEAD_DIM, (h + 1) * HEAD_DIM)
        s = _dot_t(q_ref[:, sl], kk_scr[:, sl]) + bias_ref[h]
        m = jnp.max(s, axis=-1, keepdims=True)
        e = jnp.exp2(s - m)
        l = jnp.sum(e, axis=-1, keepdims=True)
        o = _dot(e.astype(BF16), vv_scr[:, sl])
        o_ref[:, sl] = (o / l).astype(BF16)


def _attn_sample(q, kf, vf, cache_k, cache_v, bias, k_upd, v_upd, *, layer, n, t):
    depth, _, past = cache_k.shape[:3]
    rows = past * N_HEADS
    new_rows = t * N_HEADS
    row_spec = pl.BlockSpec((t, ATTN_WIDTH), lambda b: (b, 0))
    cache_spec = pl.BlockSpec((None, None, rows, HEAD_DIM), lambda b: (layer, b, 0, 0))
    new_spec = pl.BlockSpec((None, None, new_rows, HEAD_DIM), lambda b: (layer, b, rows // new_rows - 1, 0))
    bias_spec = pl.BlockSpec((N_HEADS, t, SAMPLE_KEYS_PAD), lambda b: (layer, 0, 0),
                             pipeline_mode=pl.Buffered(1))
    any_spec = pl.BlockSpec(memory_space=pl.ANY)
    upd = jax.ShapeDtypeStruct(k_upd.shape, F32)
    return pl.pallas_call(
        functools.partial(_attn_sample_kernel, t=t, past=past),
        grid=(n,),
        in_specs=[row_spec, row_spec, row_spec, cache_spec, cache_spec, bias_spec, any_spec, any_spec],
        out_specs=(row_spec, new_spec, new_spec),
        out_shape=(jax.ShapeDtypeStruct((n * t, ATTN_WIDTH), BF16), upd, upd),
        input_output_aliases={6: 1, 7: 2},
        scratch_shapes=[pltpu.VMEM((SAMPLE_KEYS_PAD, ATTN_WIDTH), BF16),
                        pltpu.VMEM((SAMPLE_KEYS_PAD, ATTN_WIDTH), BF16)],
        compiler_params=_params(("arbitrary",)),
        name="attn_sample",
    )(q, kf, vf, cache_k.reshape(depth, n, rows, HEAD_DIM), cache_v.reshape(depth, n, rows, HEAD_DIM),
      bias, k_upd, v_upd)


def _merge_kernel(*refs, nseq, t, blocks_per_seq, pos0, cache_blocks, new_rows):
    x_ref, p_ref, hist_ref, a_ref, gate_ref, pw_ref, ps_ref, wa_ref, wb_ref, wo_ref = refs[:10]
    i = pl.program_id(0)
    if cache_blocks:
        ck_ref, cv_ref, o_ref, ko_ref, vo_ref, xp_scr, stage_scr, d_scr = refs[10:]
        pl.when(i < cache_blocks)(
            functools.partial(_shift_cache_block, ck_ref, ko_ref, new_rows=new_rows))
        pl.when((i >= cache_blocks) & (i < 2 * cache_blocks))(
            functools.partial(_shift_cache_block, cv_ref, vo_ref, new_rows=new_rows))
    else:
        o_ref, xp_scr, stage_scr, d_scr = refs[10:]
    row = lax.broadcasted_iota(jnp.int32, (t, 1), 0)
    if blocks_per_seq is None:
        pos = pos0 + row
    else:
        pos = (i % blocks_per_seq) * t + row
    gated_b = gate_ref[:, D_MODEL:] * _dot(a_ref[...], wb_ref[...])

    t0 = POOL_PAD + HIST_ROWS
    xp_scr[0:POOL_PAD, :] = jnp.zeros((POOL_PAD, POOL_WIDTH), F32)
    stage_scr[:, 0:POOL_PAD, :] = jnp.zeros((2, POOL_PAD, POOL_GROUP), F32)
    for s in range(nseq):
        rows = slice(s * t, (s + 1) * t)
        hist = hist_ref[s * HIST_ROWS:(s + 1) * HIST_ROWS, :]
        if blocks_per_seq is not None:
            hist = jnp.where(i % blocks_per_seq == 0, 0.0, hist)
        xp_scr[POOL_PAD:t0, :] = hist
        xp_scr[t0:, :] = p_ref[rows, :]
        for gi, w in enumerate(POOL_WINDOWS):
            cols = slice(gi * POOL_GROUP, (gi + 1) * POOL_GROUP)
            src = xp_scr.at[:, cols]
            shift = 1
            while 2 * shift < w:
                dst = stage_scr.at[(shift.bit_length() - 1) % 2]
                dst[POOL_PAD:, :] = src[POOL_PAD:, :] + src[POOL_PAD - shift:t0 + t - shift, :]
                src = dst
                shift *= 2
            acc = src[t0:, :] + src[t0 - shift:t0 + t - shift, :]
            inv_cnt = 1.0 / jnp.minimum(pos + 1, w).astype(F32)
            d_scr[rows, cols] = (acc * inv_cnt - p_ref[rows, cols]).astype(BF16)
    pooled = jnp.concatenate(
        [_dot(d_scr[:, gi * POOL_GROUP:(gi + 1) * POOL_GROUP], pw_ref[gi]) for gi in range(len(POOL_WINDOWS))],
        axis=-1) * ps_ref[...]
    branch_a = _dot(pooled.astype(BF16), wa_ref[...])
    merged = gate_ref[:, :D_MODEL] * branch_a + gated_b
    o_ref[...] = x_ref[...] + _dot(merged.astype(BF16), wo_ref[...])


def _merge(x, p, hist, attn, gates, pool_w, pool_scale, wa, wb, wo, *, layer, tm, nseq, t, blocks_per_seq, pos0,
           hist_spec, caches=None, new_frames=0):
    m = x.shape[0]
    steps = m // tm

    def row_spec(width):
        return pl.BlockSpec((tm, width), lambda i: (i, 0))

    operands = [x, p, hist, attn, gates, pool_w, pool_scale, wa, wb, wo]
    in_specs = [row_spec(D_MODEL), row_spec(POOL_WIDTH), hist_spec, row_spec(ATTN_WIDTH),
                row_spec(2 * D_MODEL),
                _layer_spec(pool_w.shape[1:], layer), _layer_spec(pool_scale.shape[1:], layer),
                _layer_spec(wa.shape[1:], layer), _layer_spec(wb.shape[1:], layer),
                _layer_spec(wo.shape[1:], layer)]
    out_specs = [row_spec(D_MODEL)]
    out_shape = [jax.ShapeDtypeStruct((m, D_MODEL), F32)]
    cache_blocks = 0
    if caches is not None:
        depth, n, past = caches[0].shape[:3]
        cache_blocks = depth * n
        assert steps >= 2 * cache_blocks
        rows = past * N_HEADS
        k_spec = pl.BlockSpec((None, rows, HEAD_DIM), lambda i: (jnp.minimum(i, cache_blocks - 1), 0, 0))
        v_spec = pl.BlockSpec((None, rows, HEAD_DIM),
                              lambda i: (jnp.clip(i - cache_blocks, 0, cache_blocks - 1), 0, 0))
        operands += [c.reshape(cache_blocks, rows, HEAD_DIM) for c in caches]
        in_specs += [k_spec, v_spec]
        out_specs += [k_spec, v_spec]
        out_shape += [jax.ShapeDtypeStruct((cache_blocks, rows, HEAD_DIM), F32)] * 2
    outs = pl.pallas_call(
        functools.partial(_merge_kernel, nseq=nseq, t=t, blocks_per_seq=blocks_per_seq, pos0=pos0,
                          cache_blocks=cache_blocks, new_rows=new_frames * N_HEADS),
        grid=(steps,),
        in_specs=in_specs,
        out_specs=tuple(out_specs),
        out_shape=tuple(out_shape),
        scratch_shapes=[pltpu.VMEM((POOL_PAD + HIST_ROWS + t, POOL_WIDTH), F32),
                        pltpu.VMEM((2, POOL_PAD + HIST_ROWS + t, POOL_GROUP), F32),
                        pltpu.VMEM((tm, POOL_WIDTH), BF16)],
        compiler_params=_params(("arbitrary",)),
        name="merge",
    )(*operands)
    if caches is None:
        return outs[0]
    return outs[0], outs[1].reshape(depth, n, rows, HEAD_DIM), outs[2].reshape(depth, n, rows, HEAD_DIM)


FFN_TF = 512


def _ffn_kernel(x_ref, xs_ref, g2_ref, wu_ref, wd_ref, o_ref, os_ref, h_scr, hs_scr):
    i = pl.program_id(0)
    j = pl.program_id(1)

    def start(src_ref, acc_ref, scr):
        x = src_ref[...]
        scr[...] = _rms(x, g2_ref[...]).astype(BF16)
        acc_ref[...] = x

    def step(acc_ref, scr):
        u = jnp.square(jnp.maximum(_dot(scr[...], wu_ref[...].astype(BF16)), 0.0))
        acc_ref[...] += _dot(u.astype(BF16), wd_ref[...].astype(BF16))

    pl.when(j == 0)(functools.partial(start, x_ref, o_ref, h_scr))
    pl.when((i == 0) & (j == 0))(functools.partial(start, xs_ref, os_ref, hs_scr))
    step(o_ref, h_scr)
    pl.when(i == 0)(functools.partial(step, os_ref, hs_scr))


def _ffn(x, xs, g2, wu, wd, *, layer, tm):
    m = x.shape[0]
    ms = xs.shape[0]
    sample_spec = pl.BlockSpec((ms, D_MODEL), lambda i, j: (0, 0))
    return pl.pallas_call(
        _ffn_kernel,
        grid=(m // tm, D_FF // FFN_TF),
        in_specs=[pl.BlockSpec((tm, D_MODEL), lambda i, j: (i, 0)),
                  sample_spec,
                  pl.BlockSpec((None, 1, D_MODEL), lambda i, j: (layer, 0, 0)),
                  pl.BlockSpec((None, D_MODEL, FFN_TF), lambda i, j: (layer, 0, j)),
                  pl.BlockSpec((None, FFN_TF, D_MODEL), lambda i, j: (layer, j, 0))],
        out_specs=(pl.BlockSpec((tm, D_MODEL), lambda i, j: (i, 0)), sample_spec),
        out_shape=(jax.ShapeDtypeStruct((m, D_MODEL), F32), jax.ShapeDtypeStruct((ms, D_MODEL), F32)),
        scratch_shapes=[pltpu.VMEM((tm, D_MODEL), BF16), pltpu.VMEM((ms, D_MODEL), BF16)],
        compiler_params=_params(("arbitrary", "arbitrary")),
        name="ffn",
    )(x, xs, g2, wu, wd)


PROMPT_TM = 1024
MERGE_TM = 256


def kernel(x_prompt, x_sample, state_pool, cache_k, cache_v, norm1_g, w_in, b_gate, pool_w, pool_scale,
           q_norm_g, k_norm_g, rel_bias, w_branch_a, w_branch_b, w_out, norm2_g, w_up, w_down):
    nb, seq, _ = x_prompt.shape
    ns, t, _ = x_sample.shape
    depth = w_in.shape[0]
    past = cache_k.shape[2]
    assert seq % PROMPT_TM == 0 and seq % MERGE_TM == 0 and seq % Q_BLOCK == 0 and seq >= BAND_PAST
    assert t >= POOL_HIST and t % HIST_ROWS == 0 and past + t <= SAMPLE_KEYS_PAD and past <= BAND_PAST
    assert past % t == 0 and past > t

    xp = x_prompt.reshape(nb * seq, D_MODEL)
    xs = x_sample.reshape(ns * t, D_MODEL)
    blocks_per_seq = seq // MERGE_TM
    prompt_hist_spec = pl.BlockSpec(
        (HIST_ROWS, POOL_WIDTH), lambda i: (jnp.maximum(i * (MERGE_TM // HIST_ROWS) - 1, 0), 0))
    sample_hist_spec = pl.BlockSpec((ns * HIST_ROWS, POOL_WIDTH), lambda i: (0, 0))

    g1 = norm1_g.reshape(depth, 1, D_MODEL)
    g2 = norm2_g.reshape(depth, 1, D_MODEL)
    bg = b_gate.reshape(depth, 1, 2 * D_MODEL)
    qg = q_norm_g.reshape(depth, 1, HEAD_DIM)
    kg = k_norm_g.reshape(depth, 1, HEAD_DIM)
    ps = pool_scale.reshape(depth, 1, POOL_WIDTH)
    w_mixed, w_gate = _regroup_in_weights(w_in, 0)
    pw = pool_w.astype(BF16)
    wa = w_branch_a.astype(BF16)
    wb = w_branch_b.astype(BF16)
    wo = w_out.astype(BF16)
    hist_s = jnp.pad(state_pool, ((0, 0), (0, 0), (HIST_ROWS - POOL_HIST, 0), (0, 0)))
    hist_s = hist_s.reshape(depth, ns * HIST_ROWS, POOL_WIDTH)

    bias_p = _bias_table(rel_bias, q_rows=Q_BLOCK, keys=KEY_BLOCKS * Q_BLOCK,
                         offset=(KEY_BLOCKS - 1) * Q_BLOCK, visible_fn=_prompt_visible)
    bias_s = _bias_table(rel_bias, q_rows=t, keys=SAMPLE_KEYS_PAD, offset=past,
                         visible_fn=lambda qi, kj: kj < past + t)

    k_upd = v_upd = None
    pool_p, kp_l, vp_l, pool_s = [], [], [], []
    for l in range(depth):
        (p, q, k, v, kf, vf, gates,
         p_s, q_s, kf_s, vf_s, gates_s) = _in_proj(xp, xs, g1, w_mixed, w_gate, bg, qg, kg, layer=l, tm=PROMPT_TM)

        if l + 1 < depth:
            attn, w_mixed, w_gate = _attn_prompt(q, k, v, bias_p, layer=l, n=nb, s=seq, w_in=w_in)
        else:
            attn = _attn_prompt(q, k, v, bias_p, layer=l, n=nb, s=seq)
        pool_p.append(p.reshape(nb, seq, POOL_WIDTH)[:, seq - POOL_HIST:])
        kp_l.append(kf.reshape(nb, seq, N_HEADS, HEAD_DIM)[:, seq - BAND_PAST:])
        vp_l.append(vf.reshape(nb, seq, N_HEADS, HEAD_DIM)[:, seq - BAND_PAST:])
        merge_p = functools.partial(_merge, xp, p, p, attn, gates, pw, ps, wa, wb, wo, layer=l, tm=MERGE_TM,
                                    nseq=1, t=MERGE_TM, blocks_per_seq=blocks_per_seq, pos0=None,
                                    hist_spec=prompt_hist_spec)
        if l == 0:
            xp, k_upd, v_upd = merge_p(caches=(cache_k, cache_v), new_frames=t)
        else:
            xp = merge_p()

        attn_s, k_upd, v_upd = _attn_sample(q_s, kf_s, vf_s, cache_k, cache_v, bias_s, k_upd, v_upd,
                                            layer=l, n=ns, t=t)
        pool_s.append(p_s.reshape(ns, t, POOL_WIDTH)[:, t - POOL_HIST:])
        xs = _merge(xs, p_s, hist_s[l], attn_s, gates_s, pw, ps, wa, wb, wo, layer=l, tm=ns * t, nseq=ns, t=t,
                    blocks_per_seq=None, pos0=PAST_LEN, hist_spec=sample_hist_spec)

        xp, xs = _ffn(xp, xs, g2, w_up, w_down, layer=l, tm=PROMPT_TM)

    return (xp.reshape(nb, seq, D_MODEL), xs.reshape(ns, t, D_MODEL),
            jnp.stack(pool_p), jnp.stack(kp_l), jnp.stack(vp_l),
            jnp.stack(pool_s), k_upd.reshape(cache_k.shape), v_upd.reshape(cache_v.shape))
```

```python
import functools

import numpy as np

import jax
import jax.numpy as jnp
from jax import lax
from jax.experimental import pallas as pl
from jax.experimental.pallas import tpu as pltpu

D_MODEL = 2048
PAST_LEN = 1024
CHUNK = 64
N_PREV_CHUNKS = 8
BAND_PAST = N_PREV_CHUNKS * CHUNK
POOL_WIDTH = D_MODEL // 2
POOL_WINDOWS = (2, 4, 8, 16)
POOL_GROUP = POOL_WIDTH // len(POOL_WINDOWS)
POOL_HIST = max(POOL_WINDOWS) - 1
HEAD_DIM = 128
ATTN_WIDTH = D_MODEL // 2
N_HEADS = ATTN_WIDTH // HEAD_DIM
MAX_REL = 128
D_FF = 4 * D_MODEL
EPS = 1e-6
ATTN_SCALE = HEAD_DIM ** -0.5
LOG2E = 1.4426950408889634
Q_SCALE = ATTN_SCALE * LOG2E
NEG_INF = -1e30
IN_WIDTH = POOL_WIDTH + 3 * ATTN_WIDTH + 2 * D_MODEL

BF16 = jnp.bfloat16
F32 = jnp.float32

VMEM_LIMIT_BYTES = 63 * 1024 * 1024
LANES = 128

HIST_ROWS = 16
POOL_PAD = 8
Q_BLOCK = 4 * CHUNK
KEY_BLOCKS = 3
SAMPLE_KEYS_PAD = 640


def _rms(xf, g):
    ms = jnp.mean(xf * xf, axis=-1, keepdims=True)
    return xf * lax.rsqrt(ms + EPS) * g


def _dot(a, b):
    return jnp.dot(a, b, preferred_element_type=F32)


def _dot_t(a, b):
    return lax.dot_general(a, b, (((1,), (1,)), ((), ())), preferred_element_type=F32)


def _params(semantics):
    return pltpu.CompilerParams(dimension_semantics=semantics, vmem_limit_bytes=VMEM_LIMIT_BYTES)


def _layer_spec(shape, layer):
    nd = len(shape)
    return pl.BlockSpec((None,) + tuple(shape), lambda *_: (layer,) + (0,) * nd,
                        pipeline_mode=pl.Buffered(1))


def _bias_table_kernel(ext_ref, o_ref, *, q_rows, keys, visible_fn):
    width = ext_ref.shape[-1]
    qi = lax.broadcasted_iota(jnp.int32, (q_rows, keys), 0)
    kj = lax.broadcasted_iota(jnp.int32, (q_rows, keys), 1)
    visible = visible_fn(qi, kj)
    for g in range(ext_ref.shape[0]):
        row = jnp.broadcast_to(ext_ref[g:g + 1, :], (q_rows, width))
        skew = pltpu.roll(row, 0, 1, stride=1, stride_axis=0)
        o_ref[g] = jnp.where(visible, skew[:, :keys] * LOG2E, NEG_INF)


def _take_static(table, idx):
    pieces = []
    start = 0
    while start < len(idx):
        end = start + 1
        step = int(idx[end] - idx[start]) if end < len(idx) else 0
        if step in (-1, 0, 1):
            while end < len(idx) and int(idx[end] - idx[end - 1]) == step:
                end += 1
        lo, hi = sorted((int(idx[start]), int(idx[end - 1])))
        if step == 0 or end - start == 1:
            pieces.append(jnp.broadcast_to(table[:, lo:lo + 1], (table.shape[0], end - start)))
        else:
            ramp = table[:, lo:hi + 1]
            pieces.append(ramp if step == 1 else ramp[:, ::-1])
        start = end
    return jnp.concatenate(pieces, axis=1)


def _bias_table(rel_bias, *, q_rows, keys, offset, visible_fn):
    groups = rel_bias.shape[0] * rel_bias.shape[1]
    width = -(-(keys + q_rows - 1) // LANES) * LANES
    idx = np.zeros((width,), np.int32)
    idx[:keys] = np.clip(offset - np.arange(keys), -MAX_REL, MAX_REL) + MAX_REL
    back = np.arange(1, q_rows)
    idx[width - back] = np.clip(offset + back, -MAX_REL, MAX_REL) + MAX_REL
    ext = _take_static(rel_bias.reshape(groups, 2 * MAX_REL + 1).astype(F32), idx)
    return pl.pallas_call(
        functools.partial(_bias_table_kernel, q_rows=q_rows, keys=keys, visible_fn=visible_fn),
        out_shape=jax.ShapeDtypeStruct((groups, q_rows, keys), F32),
        compiler_params=pltpu.CompilerParams(vmem_limit_bytes=VMEM_LIMIT_BYTES),
        name="bias_table",
    )(ext)


def _prompt_visible(qi, kj):
    own = qi // CHUNK + (KEY_BLOCKS - 1) * (Q_BLOCK // CHUNK)
    kc = kj // CHUNK
    return (kc <= own) & (kc >= own - N_PREV_CHUNKS)


IN_HEADS = 2
IN_STEPS = N_HEADS // IN_HEADS
IN_SEG_TN = IN_HEADS * HEAD_DIM
IN_GATE_TN = 2 * D_MODEL // IN_STEPS


IN_SEG_ORDER = (1, 2, 0, 3)
IN_SLABS = 4 * ATTN_WIDTH // HEAD_DIM


def _regroup_kernel(wq_ref, wk_ref, wp_ref, wv_ref, wg_ref, mixed_ref, gate_ref):
    for s, src in enumerate((wq_ref, wk_ref, wp_ref, wv_ref)):
        mixed_ref[:, s * IN_SEG_TN:(s + 1) * IN_SEG_TN] = src[...].astype(BF16)
    gate_ref[...] = wg_ref[...].astype(BF16)


def _regroup_in_weights(w_in, layer):
    def head_cols(seg):
        return pl.BlockSpec((None, D_MODEL, IN_SEG_TN), lambda j: (layer, 0, seg * IN_STEPS + j))

    gate0 = 4 * ATTN_WIDTH // IN_GATE_TN
    return pl.pallas_call(
        _regroup_kernel,
        grid=(IN_STEPS,),
        in_specs=[head_cols(seg) for seg in IN_SEG_ORDER]
                 + [pl.BlockSpec((None, D_MODEL, IN_GATE_TN), lambda j: (layer, 0, gate0 + j))],
        out_specs=(pl.BlockSpec((None, D_MODEL, 4 * IN_SEG_TN), lambda j: (0, 0, j)),
                   pl.BlockSpec((None, D_MODEL, IN_GATE_TN), lambda j: (0, 0, j))),
        out_shape=(jax.ShapeDtypeStruct((1, D_MODEL, 4 * ATTN_WIDTH), BF16),
                   jax.ShapeDtypeStruct((1, D_MODEL, 2 * D_MODEL), BF16)),
        compiler_params=_params(("parallel",)),
        name="regroup_w_in",
    )(w_in, w_in, w_in, w_in, w_in)


def _regroup_slab_specs(layer, step_fn):
    def mixed_src(*ids):
        s = step_fn(*ids)
        group, within = s // (4 * IN_HEADS), s % (4 * IN_HEADS)
        seg, head = within // IN_HEADS, group * IN_HEADS + within % IN_HEADS
        src_seg = sum(jnp.where(seg == pos, src, 0) for pos, src in enumerate(IN_SEG_ORDER))
        return (layer, 0, src_seg * N_HEADS + head)

    slab = (None, D_MODEL, HEAD_DIM)
    in_specs = [pl.BlockSpec(slab, mixed_src),
                pl.BlockSpec(slab, lambda *ids: (layer, 0, IN_SLABS + step_fn(*ids)))]
    out_specs = [pl.BlockSpec(slab, lambda *ids: (0, 0, step_fn(*ids)))] * 2
    return in_specs, out_specs


def _in_rows(h, wm_ref, wg_ref, bg_ref, qg_ref, kg_ref, p_ref, q_ref, kf_ref, vf_ref, gate_ref,
             k_ref=None, v_ref=None):
    gate_ref[...] = jax.nn.sigmoid(_dot(h, wg_ref[...]) + bg_ref[...])
    z = _dot(h, wm_ref[...])
    for hd in range(IN_HEADS):
        sl = slice(hd * HEAD_DIM, (hd + 1) * HEAD_DIM)
        q_ref[:, sl] = (_rms(z[:, sl], qg_ref[...]) * Q_SCALE).astype(BF16)
        kn = _rms(z[:, IN_SEG_TN + hd * HEAD_DIM:IN_SEG_TN + (hd + 1) * HEAD_DIM], kg_ref[...])
        kf_ref[:, sl] = kn
        if k_ref is not None:
            k_ref[:, sl] = kn.astype(BF16)
    p_ref[...] = z[:, 2 * IN_SEG_TN:3 * IN_SEG_TN]
    v = z[:, 3 * IN_SEG_TN:4 * IN_SEG_TN]
    vf_ref[...] = v
    if v_ref is not None:
        v_ref[...] = v.astype(BF16)


def _in_kernel(x_ref, xs_ref, g1_ref, wm_ref, wg_ref, bg_ref, qg_ref, kg_ref,
               p_ref, q_ref, k_ref, v_ref, kf_ref, vf_ref, gate_ref,
               ps_ref, qs_ref, kfs_ref, vfs_ref, gates_ref, h_scr, hs_scr):
    i = pl.program_id(0)
    j = pl.program_id(1)
    weights = (wm_ref, wg_ref, bg_ref, qg_ref, kg_ref)

    @pl.when(j == 0)
    def _():
        h_scr[...] = _rms(x_ref[...], g1_ref[...]).astype(BF16)

    @pl.when((i == 0) & (j == 0))
    def _():
        hs_scr[...] = _rms(xs_ref[...], g1_ref[...]).astype(BF16)

    _in_rows(h_scr[...], *weights, p_ref, q_ref, kf_ref, vf_ref, gate_ref, k_ref, v_ref)

    @pl.when(i == 0)
    def _():
        _in_rows(hs_scr[...], *weights, ps_ref, qs_ref, kfs_ref, vfs_ref, gates_ref)


def _in_proj(x, xs, g1, w_mixed, w_gate, b_gate, qg, kg, *, layer, tm):
    m = x.shape[0]
    ms = xs.shape[0]
    head_spec = pl.BlockSpec((tm, IN_SEG_TN), lambda i, j: (i, j))

    def sample_spec(width):
        return pl.BlockSpec((ms, width), lambda i, j: (0, jnp.where(i == 0, j, IN_STEPS - 1)))

    out_shape = (
        jax.ShapeDtypeStruct((m, POOL_WIDTH), F32),
        jax.ShapeDtypeStruct((m, ATTN_WIDTH), BF16),
        jax.ShapeDtypeStruct((m, ATTN_WIDTH), BF16),
        jax.ShapeDtypeStruct((m, ATTN_WIDTH), BF16),
        jax.ShapeDtypeStruct((m, ATTN_WIDTH), F32),
        jax.ShapeDtypeStruct((m, ATTN_WIDTH), F32),
        jax.ShapeDtypeStruct((m, 2 * D_MODEL), F32),
        jax.ShapeDtypeStruct((ms, POOL_WIDTH), F32),
        jax.ShapeDtypeStruct((ms, ATTN_WIDTH), BF16),
        jax.ShapeDtypeStruct((ms, ATTN_WIDTH), F32),
        jax.ShapeDtypeStruct((ms, ATTN_WIDTH), F32),
        jax.ShapeDtypeStruct((ms, 2 * D_MODEL), F32),
    )
    return pl.pallas_call(
        _in_kernel,
        grid=(m // tm, IN_STEPS),
        in_specs=[
            pl.BlockSpec((tm, D_MODEL), lambda i, j: (i, 0)),
            pl.BlockSpec((ms, D_MODEL), lambda i, j: (0, 0)),
            pl.BlockSpec((None, 1, D_MODEL), lambda i, j: (layer, 0, 0)),
            pl.BlockSpec((None, D_MODEL, 4 * IN_SEG_TN), lambda i, j: (0, 0, j)),
            pl.BlockSpec((None, D_MODEL, IN_GATE_TN), lambda i, j: (0, 0, j)),
            pl.BlockSpec((None, 1, IN_GATE_TN), lambda i, j: (layer, 0, j)),
            pl.BlockSpec((None, 1, HEAD_DIM), lambda i, j: (layer, 0, 0)),
            pl.BlockSpec((None, 1, HEAD_DIM), lambda i, j: (layer, 0, 0)),
        ],
        out_specs=(head_spec,) * 6 + (pl.BlockSpec((tm, IN_GATE_TN), lambda i, j: (i, j)),)
                  + (sample_spec(IN_SEG_TN),) * 4 + (sample_spec(IN_GATE_TN),),
        out_shape=out_shape,
        scratch_shapes=[pltpu.VMEM((tm, D_MODEL), BF16), pltpu.VMEM((ms, D_MODEL), BF16)],
        compiler_params=_params(("arbitrary", "arbitrary")),
        name="in_proj",
    )(x, xs, g1, w_mixed, w_gate, b_gate, qg, kg)


def _attn_prompt_kernel(q_ref, k_new_ref, v_new_ref, bias_ref, *rest):
    i = pl.program_id(1)
    k_ref, v_ref = rest[-2:]
    n_casts = (len(rest) - 3) // 2
    o_ref = rest[n_casts]
    for src, dst in zip(rest[:n_casts], rest[n_casts + 1:-2]):
        dst[...] = src[...].astype(BF16)

    def heads(n_valid):
        first = KEY_BLOCKS - n_valid
        for win, new in ((k_ref, k_new_ref), (v_ref, v_new_ref)):
            for slot in range(first, KEY_BLOCKS - 1):
                win[slot * Q_BLOCK:(slot + 1) * Q_BLOCK, :] = win[(slot + 1) * Q_BLOCK:(slot + 2) * Q_BLOCK, :]
            win[(KEY_BLOCKS - 1) * Q_BLOCK:, :] = new[...]
        for h in range(N_HEADS):
            sl = slice(h * HEAD_DIM, (h + 1) * HEAD_DIM)
            qh = q_ref[:, sl]
            m = l = o = None
            for d in range(first, KEY_BLOCKS):
                rows = slice(d * Q_BLOCK, (d + 1) * Q_BLOCK)
                s = _dot_t(qh, k_ref[rows, sl]) + bias_ref[h, :, rows]
                m_r = jnp.max(s, axis=-1, keepdims=True)
                if m is None:
                    m = m_r
                    e = jnp.exp2(s - m)
                    l = jnp.sum(e, axis=-1, keepdims=True)
                    o = _dot(e.astype(BF16), v_ref[rows, sl])
                else:
                    m_new = jnp.maximum(m, m_r)
                    alpha = jnp.exp2(m - m_new)
                    e = jnp.exp2(s - m_new)
                    l = alpha * l + jnp.sum(e, axis=-1, keepdims=True)
                    o = alpha * o + _dot(e.astype(BF16), v_ref[rows, sl])
                    m = m_new
            o_ref[:, sl] = (o / l).astype(BF16)

    for n_valid in range(1, KEY_BLOCKS):
        pl.when(i == n_valid - 1)(functools.partial(heads, n_valid))
    pl.when(i >= KEY_BLOCKS - 1)(functools.partial(heads, KEY_BLOCKS))


def _attn_prompt(q, k, v, bias, *, layer, n, s, row_casts=(), w_in=None):
    nb = s // Q_BLOCK
    steps = n * nb
    q_spec = pl.BlockSpec((Q_BLOCK, ATTN_WIDTH), lambda b, i: (b * nb + i, 0))
    operands = [q, k, v, bias]
    extra_in, extra_out, extra_shape = [], [], []
    for w in row_casts:
        _, rows, cols = w.shape
        block = (None, rows // steps, cols)
        assert rows % (steps * 16) == 0
        extra_in.append(pl.BlockSpec(block, lambda b, i: (layer, b * nb + i, 0)))
        extra_out.append(pl.BlockSpec(block, lambda b, i: (0, b * nb + i, 0)))
        extra_shape.append(jax.ShapeDtypeStruct((1, rows, cols), BF16))
        operands.append(w)
    if w_in is not None:
        assert steps == IN_SLABS
        slab_in, slab_out = _regroup_slab_specs(layer + 1, lambda b, i: b * nb + i)
        extra_in += slab_in
        extra_out += slab_out
        extra_shape += [jax.ShapeDtypeStruct((1, D_MODEL, 4 * ATTN_WIDTH), BF16),
                        jax.ShapeDtypeStruct((1, D_MODEL, 2 * D_MODEL), BF16)]
        operands += [w_in, w_in]

    bias_spec = pl.BlockSpec((N_HEADS, Q_BLOCK, KEY_BLOCKS * Q_BLOCK), lambda b, i: (layer, 0, 0),
                             pipeline_mode=pl.Buffered(1))
    window = pltpu.VMEM((KEY_BLOCKS * Q_BLOCK, ATTN_WIDTH), BF16)
    outs = pl.pallas_call(
        _attn_prompt_kernel,
        grid=(n, nb),
        in_specs=[q_spec, q_spec, q_spec, bias_spec] + extra_in,
        out_specs=tuple([q_spec] + extra_out),
        out_shape=tuple([jax.ShapeDtypeStruct((n * s, ATTN_WIDTH), BF16)] + extra_shape),
        scratch_shapes=[window, window],
        compiler_params=_params(("arbitrary", "arbitrary")),
        name="attn_prompt",
    )(*operands)
    return outs


def _shift_cache_block(src_ref, dst_ref, *, new_rows):
    keep = src_ref.shape[0] - new_rows
    dst_ref[0:keep, :] = src_ref[new_rows:, :]
    dst_ref[keep:, :] = jnp.zeros((new_rows, HEAD_DIM), F32)


def _attn_sample_kernel(q_ref, kf_ref, vf_ref, ck_ref, cv_ref, bias_ref, k_alias, v_alias,
                        o_ref, ko_ref, vo_ref, kk_scr, vv_scr, *, t, past):
    del k_alias, v_alias
    for new_ref, cache_ref, out_ref, scr in ((kf_ref, ck_ref, ko_ref, kk_scr),
                                             (vf_ref, cv_ref, vo_ref, vv_scr)):
        for h in range(N_HEADS):
            sl = slice(h * HEAD_DIM, (h + 1) * HEAD_DIM)
            out_ref[pl.ds(h, t, stride=N_HEADS), :] = new_ref[:, sl]
            scr[0:past, sl] = cache_ref[pl.ds(h, past, stride=N_HEADS), :].astype(BF16)
        scr[past:past + t, :] = new_ref[...].astype(BF16)
        scr[past + t:, :] = jnp.zeros((SAMPLE_KEYS_PAD - past - t, ATTN_WIDTH), BF16)
    for h in range(N_HEADS):
        sl = slice(h * HEAD_DIM, (h + 1) * HEAD_DIM)
        s = _dot_t(q_ref[:, sl], kk_scr[:, sl]) + bias_ref[h]
        m = jnp.max(s, axis=-1, keepdims=True)
        e = jnp.exp2(s - m)
        l = jnp.sum(e, axis=-1, keepdims=True)
        o = _dot(e.astype(BF16), vv_scr[:, sl])
        o_ref[:, sl] = (o / l).astype(BF16)


def _attn_sample(q, kf, vf, cache_k, cache_v, bias, k_upd, v_upd, *, layer, n, t):
    depth, _, past = cache_k.shape[:3]
    rows = past * N_HEADS
    new_rows = t * N_HEADS
    row_spec = pl.BlockSpec((t, ATTN_WIDTH), lambda b: (b, 0))
    cache_spec = pl.BlockSpec((None, None, rows, HEAD_DIM), lambda b: (layer, b, 0, 0))
    new_spec = pl.BlockSpec((None, None, new_rows, HEAD_DIM), lambda b: (layer, b, rows // new_rows - 1, 0))
    bias_spec = pl.BlockSpec((N_HEADS, t, SAMPLE_KEYS_PAD), lambda b: (layer, 0, 0),
                             pipeline_mode=pl.Buffered(1))
    any_spec = pl.BlockSpec(memory_space=pl.ANY)
    upd = jax.ShapeDtypeStruct(k_upd.shape, F32)
    return pl.pallas_call(
        functools.partial(_attn_sample_kernel, t=t, past=past),
        grid=(n,),
        in_specs=[row_spec, row_spec, row_spec, cache_spec, cache_spec, bias_spec, any_spec, any_spec],
        out_specs=(row_spec, new_spec, new_spec),
        out_shape=(jax.ShapeDtypeStruct((n * t, ATTN_WIDTH), BF16), upd, upd),
        input_output_aliases={6: 1, 7: 2},
        scratch_shapes=[pltpu.VMEM((SAMPLE_KEYS_PAD, ATTN_WIDTH), BF16),
                        pltpu.VMEM((SAMPLE_KEYS_PAD, ATTN_WIDTH), BF16)],
        compiler_params=_params(("arbitrary",)),
        name="attn_sample",
    )(q, kf, vf, cache_k.reshape(depth, n, rows, HEAD_DIM), cache_v.reshape(depth, n, rows, HEAD_DIM),
      bias, k_upd, v_upd)


def _merge_kernel(*refs, nseq, t, blocks_per_seq, pos0, cache_blocks, new_rows):
    x_ref, p_ref, hist_ref, a_ref, gate_ref, pw_ref, ps_ref, wa_ref, wb_ref, wo_ref = refs[:10]
    i = pl.program_id(0)
    if cache_blocks:
        ck_ref, cv_ref, o_ref, ko_ref, vo_ref, xp_scr, stage_scr, d_scr = refs[10:]
        pl.when(i < cache_blocks)(
            functools.partial(_shift_cache_block, ck_ref, ko_ref, new_rows=new_rows))
        pl.when((i >= cache_blocks) & (i < 2 * cache_blocks))(
            functools.partial(_shift_cache_block, cv_ref, vo_ref, new_rows=new_rows))
    else:
        o_ref, xp_scr, stage_scr, d_scr = refs[10:]
    row = lax.broadcasted_iota(jnp.int32, (t, 1), 0)
    if blocks_per_seq is None:
        pos = pos0 + row
    else:
        pos = (i % blocks_per_seq) * t + row
    gated_b = gate_ref[:, D_MODEL:] * _dot(a_ref[...], wb_ref[...])

    t0 = POOL_PAD + HIST_ROWS
    xp_scr[0:POOL_PAD, :] = jnp.zeros((POOL_PAD, POOL_WIDTH), F32)
    stage_scr[:, 0:POOL_PAD, :] = jnp.zeros((2, POOL_PAD, POOL_GROUP), F32)
    for s in range(nseq):
        rows = slice(s * t, (s + 1) * t)
        hist = hist_ref[s * HIST_ROWS:(s + 1) * HIST_ROWS, :]
        if blocks_per_seq is not None:
            hist = jnp.where(i % blocks_per_seq == 0, 0.0, hist)
        xp_scr[POOL_PAD:t0, :] = hist
        xp_scr[t0:, :] = p_ref[rows, :]
        for gi, w in enumerate(POOL_WINDOWS):
            cols = slice(gi * POOL_GROUP, (gi + 1) * POOL_GROUP)
            src = xp_scr.at[:, cols]
            shift = 1
            while 2 * shift < w:
                dst = stage_scr.at[(shift.bit_length() - 1) % 2]
                dst[POOL_PAD:, :] = src[POOL_PAD:, :] + src[POOL_PAD - shift:t0 + t - shift, :]
                src = dst
                shift *= 2
            acc = src[t0:, :] + src[t0 - shift:t0 + t - shift, :]
            inv_cnt = 1.0 / jnp.minimum(pos + 1, w).astype(F32)
            d_scr[rows, cols] = (acc * inv_cnt - p_ref[rows, cols]).astype(BF16)
    pooled = jnp.concatenate(
        [_dot(d_scr[:, gi * POOL_GROUP:(gi + 1) * POOL_GROUP], pw_ref[gi]) for gi in range(len(POOL_WINDOWS))],
        axis=-1) * ps_ref[...]
    branch_a = _dot(pooled.astype(BF16), wa_ref[...])
    merged = gate_ref[:, :D_MODEL] * branch_a + gated_b
    o_ref[...] = x_ref[...] + _dot(merged.astype(BF16), wo_ref[...])


def _merge(x, p, hist, attn, gates, pool_w, pool_scale, wa, wb, wo, *, layer, tm, nseq, t, blocks_per_seq, pos0,
           hist_spec, caches=None, new_frames=0):
    m = x.shape[0]
    steps = m // tm

    def row_spec(width):
        return pl.BlockSpec((tm, width), lambda i: (i, 0))

    operands = [x, p, hist, attn, gates, pool_w, pool_scale, wa, wb, wo]
    in_specs = [row_spec(D_MODEL), row_spec(POOL_WIDTH), hist_spec, row_spec(ATTN_WIDTH),
                row_spec(2 * D_MODEL),
                _layer_spec(pool_w.shape[1:], 0), _layer_spec(pool_scale.shape[1:], layer),
                _layer_spec(wa.shape[1:], 0), _layer_spec(wb.shape[1:], 0),
                _layer_spec(wo.shape[1:], 0)]
    out_specs = [row_spec(D_MODEL)]
    out_shape = [jax.ShapeDtypeStruct((m, D_MODEL), F32)]
    cache_blocks = 0
    if caches is not None:
        depth, n, past = caches[0].shape[:3]
        cache_blocks = depth * n
        assert steps >= 2 * cache_blocks
        rows = past * N_HEADS
        k_spec = pl.BlockSpec((None, rows, HEAD_DIM), lambda i: (jnp.minimum(i, cache_blocks - 1), 0, 0))
        v_spec = pl.BlockSpec((None, rows, HEAD_DIM),
                              lambda i: (jnp.clip(i - cache_blocks, 0, cache_blocks - 1), 0, 0))
        operands += [c.reshape(cache_blocks, rows, HEAD_DIM) for c in caches]
        in_specs += [k_spec, v_spec]
        out_specs += [k_spec, v_spec]
        out_shape += [jax.ShapeDtypeStruct((cache_blocks, rows, HEAD_DIM), F32)] * 2
    outs = pl.pallas_call(
        functools.partial(_merge_kernel, nseq=nseq, t=t, blocks_per_seq=blocks_per_seq, pos0=pos0,
                          cache_blocks=cache_blocks, new_rows=new_frames * N_HEADS),
        grid=(steps,),
        in_specs=in_specs,
        out_specs=tuple(out_specs),
        out_shape=tuple(out_shape),
        scratch_shapes=[pltpu.VMEM((POOL_PAD + HIST_ROWS + t, POOL_WIDTH), F32),
                        pltpu.VMEM((2, POOL_PAD + HIST_ROWS + t, POOL_GROUP), F32),
                        pltpu.VMEM((tm, POOL_WIDTH), BF16)],
        compiler_params=_params(("arbitrary",)),
        name="merge",
    )(*operands)
    if caches is None:
        return outs[0]
    return outs[0], outs[1].reshape(depth, n, rows, HEAD_DIM), outs[2].reshape(depth, n, rows, HEAD_DIM)


FFN_TF = 512


def _ffn_kernel(x_ref, xs_ref, g2_ref, wu_ref, wd_ref, o_ref, os_ref, h_scr, hs_scr):
    i = pl.program_id(0)
    j = pl.program_id(1)

    def start(src_ref, acc_ref, scr):
        x = src_ref[...]
        scr[...] = _rms(x, g2_ref[...]).astype(BF16)
        acc_ref[...] = x

    def step(acc_ref, scr):
        u = jnp.square(jnp.maximum(_dot(scr[...], wu_ref[...].astype(BF16)), 0.0))
        acc_ref[...] += _dot(u.astype(BF16), wd_ref[...].astype(BF16))

    pl.when(j == 0)(functools.partial(start, x_ref, o_ref, h_scr))
    pl.when((i == 0) & (j == 0))(functools.partial(start, xs_ref, os_ref, hs_scr))
    step(o_ref, h_scr)
    pl.when(i == 0)(functools.partial(step, os_ref, hs_scr))


def _ffn(x, xs, g2, wu, wd, *, layer, tm):
    m = x.shape[0]
    ms = xs.shape[0]
    sample_spec = pl.BlockSpec((ms, D_MODEL), lambda i, j: (0, 0))
    return pl.pallas_call(
        _ffn_kernel,
        grid=(m // tm, D_FF // FFN_TF),
        in_specs=[pl.BlockSpec((tm, D_MODEL), lambda i, j: (i, 0)),
                  sample_spec,
                  pl.BlockSpec((None, 1, D_MODEL), lambda i, j: (layer, 0, 0)),
                  pl.BlockSpec((None, D_MODEL, FFN_TF), lambda i, j: (layer, 0, j)),
                  pl.BlockSpec((None, FFN_TF, D_MODEL), lambda i, j: (layer, j, 0))],
        out_specs=(pl.BlockSpec((tm, D_MODEL), lambda i, j: (i, 0)), sample_spec),
        out_shape=(jax.ShapeDtypeStruct((m, D_MODEL), F32), jax.ShapeDtypeStruct((ms, D_MODEL), F32)),
        scratch_shapes=[pltpu.VMEM((tm, D_MODEL), BF16), pltpu.VMEM((ms, D_MODEL), BF16)],
        compiler_params=_params(("arbitrary", "arbitrary")),
        name="ffn",
    )(x, xs, g2, wu, wd)


PROMPT_TM = 1024
MERGE_TM = 256


def kernel(x_prompt, x_sample, state_pool, cache_k, cache_v, norm1_g, w_in, b_gate, pool_w, pool_scale,
           q_norm_g, k_norm_g, rel_bias, w_branch_a, w_branch_b, w_out, norm2_g, w_up, w_down):
    nb, seq, _ = x_prompt.shape
    ns, t, _ = x_sample.shape
    depth = w_in.shape[0]
    past = cache_k.shape[2]
    assert seq % PROMPT_TM == 0 and seq % MERGE_TM == 0 and seq % Q_BLOCK == 0 and seq >= BAND_PAST
    assert t >= POOL_HIST and t % HIST_ROWS == 0 and past + t <= SAMPLE_KEYS_PAD and past <= BAND_PAST
    assert past % t == 0 and past > t

    xp = x_prompt.reshape(nb * seq, D_MODEL)
    xs = x_sample.reshape(ns * t, D_MODEL)
    blocks_per_seq = seq // MERGE_TM
    prompt_hist_spec = pl.BlockSpec(
        (HIST_ROWS, POOL_WIDTH), lambda i: (jnp.maximum(i * (MERGE_TM // HIST_ROWS) - 1, 0), 0))
    sample_hist_spec = pl.BlockSpec((ns * HIST_ROWS, POOL_WIDTH), lambda i: (0, 0))

    g1 = norm1_g.reshape(depth, 1, D_MODEL)
    g2 = norm2_g.reshape(depth, 1, D_MODEL)
    bg = b_gate.reshape(depth, 1, 2 * D_MODEL)
    qg = q_norm_g.reshape(depth, 1, HEAD_DIM)
    kg = k_norm_g.reshape(depth, 1, HEAD_DIM)
    ps = pool_scale.reshape(depth, 1, POOL_WIDTH)
    w_mixed, w_gate = _regroup_in_weights(w_in, 0)
    merge_weights = (pool_w.reshape(depth, POOL_WIDTH, POOL_GROUP), w_branch_a, w_branch_b, w_out)
    hist_s = jnp.pad(state_pool, ((0, 0), (0, 0), (HIST_ROWS - POOL_HIST, 0), (0, 0)))
    hist_s = hist_s.reshape(depth, ns * HIST_ROWS, POOL_WIDTH)

    bias_p = _bias_table(rel_bias, q_rows=Q_BLOCK, keys=KEY_BLOCKS * Q_BLOCK,
                         offset=(KEY_BLOCKS - 1) * Q_BLOCK, visible_fn=_prompt_visible)
    bias_s = _bias_table(rel_bias, q_rows=t, keys=SAMPLE_KEYS_PAD, offset=past,
                         visible_fn=lambda qi, kj: kj < past + t)

    k_upd = v_upd = None
    pool_p, kp_l, vp_l, pool_s = [], [], [], []
    for l in range(depth):
        (p, q, k, v, kf, vf, gates,
         p_s, q_s, kf_s, vf_s, gates_s) = _in_proj(xp, xs, g1, w_mixed, w_gate, bg, qg, kg, layer=l, tm=PROMPT_TM)

        attn, pw, wa, wb, wo, *next_w_in = _attn_prompt(
            q, k, v, bias_p, layer=l, n=nb, s=seq, row_casts=merge_weights,
            w_in=w_in if l + 1 < depth else None)
        pw = pw.reshape(1, len(POOL_WINDOWS), POOL_GROUP, POOL_GROUP)
        if next_w_in:
            w_mixed, w_gate = next_w_in
        pool_p.append(p.reshape(nb, seq, POOL_WIDTH)[:, seq - POOL_HIST:])
        kp_l.append(kf.reshape(nb, seq, N_HEADS, HEAD_DIM)[:, seq - BAND_PAST:])
        vp_l.append(vf.reshape(nb, seq, N_HEADS, HEAD_DIM)[:, seq - BAND_PAST:])
        merge_p = functools.partial(_merge, xp, p, p, attn, gates, pw, ps, wa, wb, wo, layer=l, tm=MERGE_TM,
                                    nseq=1, t=MERGE_TM, blocks_per_seq=blocks_per_seq, pos0=None,
                                    hist_spec=prompt_hist_spec)
        if l == 0:
            xp, k_upd, v_upd = merge_p(caches=(cache_k, cache_v), new_frames=t)
        else:
            xp = merge_p()

        attn_s, k_upd, v_upd = _attn_sample(q_s, kf_s, vf_s, cache_k, cache_v, bias_s, k_upd, v_upd,
                                            layer=l, n=ns, t=t)
        pool_s.append(p_s.reshape(ns, t, POOL_WIDTH)[:, t - POOL_HIST:])
        xs = _merge(xs, p_s, hist_s[l], attn_s, gates_s, pw, ps, wa, wb, wo, layer=l, tm=ns * t, nseq=ns, t=t,
                    blocks_per_seq=None, pos0=PAST_LEN, hist_spec=sample_hist_spec)

        xp, xs = _ffn(xp, xs, g2, w_up, w_down, layer=l, tm=PROMPT_TM)

    return (xp.reshape(nb, seq, D_MODEL), xs.reshape(ns, t, D_MODEL),
            jnp.stack(pool_p), jnp.stack(kp_l), jnp.stack(vp_l),
            jnp.stack(pool_s), k_upd.reshape(cache_k.shape), v_upd.reshape(cache_v.shape))
```

```python
import functools

import numpy as np

import jax
import jax.numpy as jnp
from jax import lax
from jax.experimental import pallas as pl
from jax.experimental.pallas import tpu as pltpu

D_MODEL = 2048
PAST_LEN = 1024
CHUNK = 64
N_PREV_CHUNKS = 8
BAND_PAST = N_PREV_CHUNKS * CHUNK
POOL_WIDTH = D_MODEL // 2
POOL_WINDOWS = (2, 4, 8, 16)
POOL_GROUP = POOL_WIDTH // len(POOL_WINDOWS)
POOL_HIST = max(POOL_WINDOWS) - 1
HEAD_DIM = 128
ATTN_WIDTH = D_MODEL // 2
N_HEADS = ATTN_WIDTH // HEAD_DIM
MAX_REL = 128
D_FF = 4 * D_MODEL
EPS = 1e-6
ATTN_SCALE = HEAD_DIM ** -0.5
LOG2E = 1.4426950408889634
Q_SCALE = ATTN_SCALE * LOG2E
NEG_INF = -1e30
IN_WIDTH = POOL_WIDTH + 3 * ATTN_WIDTH + 2 * D_MODEL

BF16 = jnp.bfloat16
F32 = jnp.float32

VMEM_LIMIT_BYTES = 63 * 1024 * 1024
LANES = 128

HIST_ROWS = 16
POOL_PAD = 8
Q_BLOCK = 4 * CHUNK
KEY_BLOCKS = 3
SAMPLE_KEYS_PAD = 640


def _rms(xf, g):
    ms = jnp.mean(xf * xf, axis=-1, keepdims=True)
    return xf * lax.rsqrt(ms + EPS) * g


def _dot(a, b):
    return jnp.dot(a, b, preferred_element_type=F32)


def _dot_t(a, b):
    return lax.dot_general(a, b, (((1,), (1,)), ((), ())), preferred_element_type=F32)


def _params(semantics):
    return pltpu.CompilerParams(dimension_semantics=semantics, vmem_limit_bytes=VMEM_LIMIT_BYTES)


def _layer_spec(shape, layer):
    nd = len(shape)
    return pl.BlockSpec((None,) + tuple(shape), lambda *_: (layer,) + (0,) * nd,
                        pipeline_mode=pl.Buffered(1))


def _bias_table_kernel(ext_ref, o_ref, *, q_rows, keys, visible_fn):
    width = ext_ref.shape[-1]
    qi = lax.broadcasted_iota(jnp.int32, (q_rows, keys), 0)
    kj = lax.broadcasted_iota(jnp.int32, (q_rows, keys), 1)
    visible = visible_fn(qi, kj)
    for g in range(ext_ref.shape[0]):
        row = jnp.broadcast_to(ext_ref[g:g + 1, :], (q_rows, width))
        skew = pltpu.roll(row, 0, 1, stride=1, stride_axis=0)
        o_ref[g] = jnp.where(visible, skew[:, :keys] * LOG2E, NEG_INF)


def _take_static(table, idx):
    pieces = []
    start = 0
    while start < len(idx):
        end = start + 1
        step = int(idx[end] - idx[start]) if end < len(idx) else 0
        if step in (-1, 0, 1):
            while end < len(idx) and int(idx[end] - idx[end - 1]) == step:
                end += 1
        lo, hi = sorted((int(idx[start]), int(idx[end - 1])))
        if step == 0 or end - start == 1:
            pieces.append(jnp.broadcast_to(table[:, lo:lo + 1], (table.shape[0], end - start)))
        else:
            ramp = table[:, lo:hi + 1]
            pieces.append(ramp if step == 1 else ramp[:, ::-1])
        start = end
    return jnp.concatenate(pieces, axis=1)


def _bias_table(rel_bias, *, q_rows, keys, offset, visible_fn):
    groups = rel_bias.shape[0] * rel_bias.shape[1]
    width = -(-(keys + q_rows - 1) // LANES) * LANES
    idx = np.zeros((width,), np.int32)
    idx[:keys] = np.clip(offset - np.arange(keys), -MAX_REL, MAX_REL) + MAX_REL
    back = np.arange(1, q_rows)
    idx[width - back] = np.clip(offset + back, -MAX_REL, MAX_REL) + MAX_REL
    ext = _take_static(rel_bias.reshape(groups, 2 * MAX_REL + 1).astype(F32), idx)
    return pl.pallas_call(
        functools.partial(_bias_table_kernel, q_rows=q_rows, keys=keys, visible_fn=visible_fn),
        out_shape=jax.ShapeDtypeStruct((groups, q_rows, keys), F32),
        compiler_params=pltpu.CompilerParams(vmem_limit_bytes=VMEM_LIMIT_BYTES),
        name="bias_table",
    )(ext)


def _prompt_visible(qi, kj):
    own = qi // CHUNK + (KEY_BLOCKS - 1) * (Q_BLOCK // CHUNK)
    kc = kj // CHUNK
    return (kc <= own) & (kc >= own - N_PREV_CHUNKS)


IN_HEADS = 2
IN_STEPS = N_HEADS // IN_HEADS
IN_SEG_TN = IN_HEADS * HEAD_DIM
IN_GATE_TN = 2 * D_MODEL // IN_STEPS


IN_SEG_ORDER = (1, 2, 0, 3)
IN_SLABS = 4 * ATTN_WIDTH // HEAD_DIM


def _regroup_kernel(wq_ref, wk_ref, wp_ref, wv_ref, wg_ref, mixed_ref, gate_ref):
    for s, src in enumerate((wq_ref, wk_ref, wp_ref, wv_ref)):
        mixed_ref[:, s * IN_SEG_TN:(s + 1) * IN_SEG_TN] = src[...].astype(BF16)
    gate_ref[...] = wg_ref[...].astype(BF16)


def _regroup_in_weights(w_in, layer):
    def head_cols(seg):
        return pl.BlockSpec((None, D_MODEL, IN_SEG_TN), lambda j: (layer, 0, seg * IN_STEPS + j))

    gate0 = 4 * ATTN_WIDTH // IN_GATE_TN
    return pl.pallas_call(
        _regroup_kernel,
        grid=(IN_STEPS,),
        in_specs=[head_cols(seg) for seg in IN_SEG_ORDER]
                 + [pl.BlockSpec((None, D_MODEL, IN_GATE_TN), lambda j: (layer, 0, gate0 + j))],
        out_specs=(pl.BlockSpec((None, D_MODEL, 4 * IN_SEG_TN), lambda j: (0, 0, j)),
                   pl.BlockSpec((None, D_MODEL, IN_GATE_TN), lambda j: (0, 0, j))),
        out_shape=(jax.ShapeDtypeStruct((1, D_MODEL, 4 * ATTN_WIDTH), BF16),
                   jax.ShapeDtypeStruct((1, D_MODEL, 2 * D_MODEL), BF16)),
        compiler_params=_params(("parallel",)),
        name="regroup_w_in",
    )(w_in, w_in, w_in, w_in, w_in)


def _regroup_slab_specs(layer, step_fn):
    def mixed_src(*ids):
        s = step_fn(*ids)
        group, within = s // (4 * IN_HEADS), s % (4 * IN_HEADS)
        seg, head = within // IN_HEADS, group * IN_HEADS + within % IN_HEADS
        src_seg = sum(jnp.where(seg == pos, src, 0) for pos, src in enumerate(IN_SEG_ORDER))
        return (layer, 0, src_seg * N_HEADS + head)

    slab = (None, D_MODEL, HEAD_DIM)
    in_specs = [pl.BlockSpec(slab, mixed_src),
                pl.BlockSpec(slab, lambda *ids: (layer, 0, IN_SLABS + step_fn(*ids)))]
    out_specs = [pl.BlockSpec(slab, lambda *ids: (0, 0, step_fn(*ids)))] * 2
    return in_specs, out_specs


def _in_rows(h, wm_ref, wg_ref, bg_ref, qg_ref, kg_ref, p_ref, q_ref, kf_ref, vf_ref, gate_ref,
             k_ref=None, v_ref=None):
    gate_ref[...] = jax.nn.sigmoid(_dot(h, wg_ref[...]) + bg_ref[...])
    z = _dot(h, wm_ref[...])
    for hd in range(IN_HEADS):
        sl = slice(hd * HEAD_DIM, (hd + 1) * HEAD_DIM)
        q_ref[:, sl] = (_rms(z[:, sl], qg_ref[...]) * Q_SCALE).astype(BF16)
        kn = _rms(z[:, IN_SEG_TN + hd * HEAD_DIM:IN_SEG_TN + (hd + 1) * HEAD_DIM], kg_ref[...])
        kf_ref[:, sl] = kn
        if k_ref is not None:
            k_ref[:, sl] = kn.astype(BF16)
    p_ref[...] = z[:, 2 * IN_SEG_TN:3 * IN_SEG_TN]
    v = z[:, 3 * IN_SEG_TN:4 * IN_SEG_TN]
    vf_ref[...] = v
    if v_ref is not None:
        v_ref[...] = v.astype(BF16)


def _in_kernel(x_ref, xs_ref, g1_ref, wm_ref, wg_ref, bg_ref, qg_ref, kg_ref,
               p_ref, q_ref, k_ref, v_ref, kf_ref, vf_ref, gate_ref,
               ps_ref, qs_ref, kfs_ref, vfs_ref, gates_ref, h_scr, hs_scr):
    i = pl.program_id(0)
    j = pl.program_id(1)
    weights = (wm_ref, wg_ref, bg_ref, qg_ref, kg_ref)

    @pl.when(j == 0)
    def _():
        h_scr[...] = _rms(x_ref[...], g1_ref[...]).astype(BF16)

    @pl.when((i == 0) & (j == 0))
    def _():
        hs_scr[...] = _rms(xs_ref[...], g1_ref[...]).astype(BF16)

    _in_rows(h_scr[...], *weights, p_ref, q_ref, kf_ref, vf_ref, gate_ref, k_ref, v_ref)

    @pl.when(i == 0)
    def _():
        _in_rows(hs_scr[...], *weights, ps_ref, qs_ref, kfs_ref, vfs_ref, gates_ref)


def _in_proj(x, xs, g1, w_mixed, w_gate, b_gate, qg, kg, *, layer, tm):
    m = x.shape[0]
    ms = xs.shape[0]
    head_spec = pl.BlockSpec((tm, IN_SEG_TN), lambda i, j: (i, j))

    def sample_spec(width):
        return pl.BlockSpec((ms, width), lambda i, j: (0, jnp.where(i == 0, j, IN_STEPS - 1)))

    out_shape = (
        jax.ShapeDtypeStruct((m, POOL_WIDTH), F32),
        jax.ShapeDtypeStruct((m, ATTN_WIDTH), BF16),
        jax.ShapeDtypeStruct((m, ATTN_WIDTH), BF16),
        jax.ShapeDtypeStruct((m, ATTN_WIDTH), BF16),
        jax.ShapeDtypeStruct((m, ATTN_WIDTH), F32),
        jax.ShapeDtypeStruct((m, ATTN_WIDTH), F32),
        jax.ShapeDtypeStruct((m, 2 * D_MODEL), F32),
        jax.ShapeDtypeStruct((ms, POOL_WIDTH), F32),
        jax.ShapeDtypeStruct((ms, ATTN_WIDTH), BF16),
        jax.ShapeDtypeStruct((ms, ATTN_WIDTH), F32),
        jax.ShapeDtypeStruct((ms, ATTN_WIDTH), F32),
        jax.ShapeDtypeStruct((ms, 2 * D_MODEL), F32),
    )
    return pl.pallas_call(
        _in_kernel,
        grid=(m // tm, IN_STEPS),
        in_specs=[
            pl.BlockSpec((tm, D_MODEL), lambda i, j: (i, 0)),
            pl.BlockSpec((ms, D_MODEL), lambda i, j: (0, 0)),
            pl.BlockSpec((None, 1, D_MODEL), lambda i, j: (layer, 0, 0)),
            pl.BlockSpec((None, D_MODEL, 4 * IN_SEG_TN), lambda i, j: (0, 0, j)),
            pl.BlockSpec((None, D_MODEL, IN_GATE_TN), lambda i, j: (0, 0, j)),
            pl.BlockSpec((None, 1, IN_GATE_TN), lambda i, j: (layer, 0, j)),
            pl.BlockSpec((None, 1, HEAD_DIM), lambda i, j: (layer, 0, 0)),
            pl.BlockSpec((None, 1, HEAD_DIM), lambda i, j: (layer, 0, 0)),
        ],
        out_specs=(head_spec,) * 6 + (pl.BlockSpec((tm, IN_GATE_TN), lambda i, j: (i, j)),)
                  + (sample_spec(IN_SEG_TN),) * 4 + (sample_spec(IN_GATE_TN),),
        out_shape=out_shape,
        scratch_shapes=[pltpu.VMEM((tm, D_MODEL), BF16), pltpu.VMEM((ms, D_MODEL), BF16)],
        compiler_params=_params(("arbitrary", "arbitrary")),
        name="in_proj",
    )(x, xs, g1, w_mixed, w_gate, b_gate, qg, kg)


def _attn_prompt_kernel(q_ref, k_new_ref, v_new_ref, bias_ref, *rest):
    i = pl.program_id(1)
    k_ref, v_ref = rest[-2:]
    n_casts = (len(rest) - 3) // 2
    o_ref = rest[n_casts]
    for src, dst in zip(rest[:n_casts], rest[n_casts + 1:-2]):
        dst[...] = src[...].astype(BF16)

    def heads(n_valid):
        first = KEY_BLOCKS - n_valid
        for win, new in ((k_ref, k_new_ref), (v_ref, v_new_ref)):
            for slot in range(first, KEY_BLOCKS - 1):
                win[slot * Q_BLOCK:(slot + 1) * Q_BLOCK, :] = win[(slot + 1) * Q_BLOCK:(slot + 2) * Q_BLOCK, :]
            win[(KEY_BLOCKS - 1) * Q_BLOCK:, :] = new[...]
        for h in range(N_HEADS):
            sl = slice(h * HEAD_DIM, (h + 1) * HEAD_DIM)
            qh = q_ref[:, sl]
            m = l = o = None
            for d in range(first, KEY_BLOCKS):
                rows = slice(d * Q_BLOCK, (d + 1) * Q_BLOCK)
                s = _dot_t(qh, k_ref[rows, sl]) + bias_ref[h, :, rows]
                m_r = jnp.max(s, axis=-1, keepdims=True)
                if m is None:
                    m = m_r
                    e = jnp.exp2(s - m)
                    l = jnp.sum(e, axis=-1, keepdims=True)
                    o = _dot(e.astype(BF16), v_ref[rows, sl])
                else:
                    m_new = jnp.maximum(m, m_r)
                    alpha = jnp.exp2(m - m_new)
                    e = jnp.exp2(s - m_new)
                    l = alpha * l + jnp.sum(e, axis=-1, keepdims=True)
                    o = alpha * o + _dot(e.astype(BF16), v_ref[rows, sl])
                    m = m_new
            o_ref[:, sl] = (o / l).astype(BF16)

    for n_valid in range(1, KEY_BLOCKS):
        pl.when(i == n_valid - 1)(functools.partial(heads, n_valid))
    pl.when(i >= KEY_BLOCKS - 1)(functools.partial(heads, KEY_BLOCKS))


def _attn_prompt(q, k, v, bias, *, layer, n, s, row_casts=(), w_in=None):
    nb = s // Q_BLOCK
    steps = n * nb
    q_spec = pl.BlockSpec((Q_BLOCK, ATTN_WIDTH), lambda b, i: (b * nb + i, 0))
    operands = [q, k, v, bias]
    extra_in, extra_out, extra_shape = [], [], []
    for w in row_casts:
        _, rows, cols = w.shape
        block = (None, rows // steps, cols)
        assert rows % (steps * 16) == 0
        extra_in.append(pl.BlockSpec(block, lambda b, i: (layer, b * nb + i, 0)))
        extra_out.append(pl.BlockSpec(block, lambda b, i: (0, b * nb + i, 0)))
        extra_shape.append(jax.ShapeDtypeStruct((1, rows, cols), BF16))
        operands.append(w)
    if w_in is not None:
        assert steps == IN_SLABS
        slab_in, slab_out = _regroup_slab_specs(layer + 1, lambda b, i: b * nb + i)
        extra_in += slab_in
        extra_out += slab_out
        extra_shape += [jax.ShapeDtypeStruct((1, D_MODEL, 4 * ATTN_WIDTH), BF16),
                        jax.ShapeDtypeStruct((1, D_MODEL, 2 * D_MODEL), BF16)]
        operands += [w_in, w_in]

    bias_spec = pl.BlockSpec((N_HEADS, Q_BLOCK, KEY_BLOCKS * Q_BLOCK), lambda b, i: (layer, 0, 0),
                             pipeline_mode=pl.Buffered(1))
    window = pltpu.VMEM((KEY_BLOCKS * Q_BLOCK, ATTN_WIDTH), BF16)
    outs = pl.pallas_call(
        _attn_prompt_kernel,
        grid=(n, nb),
        in_specs=[q_spec, q_spec, q_spec, bias_spec] + extra_in,
        out_specs=tuple([q_spec] + extra_out),
        out_shape=tuple([jax.ShapeDtypeStruct((n * s, ATTN_WIDTH), BF16)] + extra_shape),
        scratch_shapes=[window, window],
        compiler_params=_params(("arbitrary", "arbitrary")),
        name="attn_prompt",
    )(*operands)
    return outs


def _shift_cache_block(src_ref, dst_ref, *, new_rows):
    keep = src_ref.shape[0] - new_rows
    dst_ref[0:keep, :] = src_ref[new_rows:, :]
    dst_ref[keep:, :] = jnp.zeros((new_rows, HEAD_DIM), F32)


def _attn_sample_kernel(q_ref, kf_ref, vf_ref, ck_ref, cv_ref, bias_ref, k_alias, v_alias,
                        o_ref, ko_ref, vo_ref, kk_scr, vv_scr, *, t, past):
    del k_alias, v_alias
    for new_ref, cache_ref, out_ref, scr in ((kf_ref, ck_ref, ko_ref, kk_scr),
                                             (vf_ref, cv_ref, vo_ref, vv_scr)):
        for h in range(N_HEADS):
            sl = slice(h * HEAD_DIM, (h + 1) * HEAD_DIM)
            out_ref[pl.ds(h, t, stride=N_HEADS), :] = new_ref[:, sl]
            scr[0:past, sl] = cache_ref[pl.ds(h, past, stride=N_HEADS), :].astype(BF16)
        scr[past:past + t, :] = new_ref[...].astype(BF16)
        scr[past + t:, :] = jnp.zeros((SAMPLE_KEYS_PAD - past - t, ATTN_WIDTH), BF16)
    for h in range(N_HEADS):
        sl = slice(h * HEAD_DIM, (h + 1) * HEAD_DIM)
        s = _dot_t(q_ref[:, sl], kk_scr[:, sl]) + bias_ref[h]
        m = jnp.max(s, axis=-1, keepdims=True)
        e = jnp.exp2(s - m)
        l = jnp.sum(e, axis=-1, keepdims=True)
        o = _dot(e.astype(BF16), vv_scr[:, sl])
        o_ref[:, sl] = (o / l).astype(BF16)


def _attn_sample(q, kf, vf, cache_k, cache_v, bias, k_upd, v_upd, *, layer, n, t):
    depth, _, past = cache_k.shape[:3]
    rows = past * N_HEADS
    new_rows = t * N_HEADS
    row_spec = pl.BlockSpec((t, ATTN_WIDTH), lambda b: (b, 0))
    cache_spec = pl.BlockSpec((None, None, rows, HEAD_DIM), lambda b: (layer, b, 0, 0))
    new_spec = pl.BlockSpec((None, None, new_rows, HEAD_DIM), lambda b: (layer, b, rows // new_rows - 1, 0))
    bias_spec = pl.BlockSpec((N_HEADS, t, SAMPLE_KEYS_PAD), lambda b: (layer, 0, 0),
                             pipeline_mode=pl.Buffered(1))
    any_spec = pl.BlockSpec(memory_space=pl.ANY)
    upd = jax.ShapeDtypeStruct(k_upd.shape, F32)
    return pl.pallas_call(
        functools.partial(_attn_sample_kernel, t=t, past=past),
        grid=(n,),
        in_specs=[row_spec, row_spec, row_spec, cache_spec, cache_spec, bias_spec, any_spec, any_spec],
        out_specs=(row_spec, new_spec, new_spec),
        out_shape=(jax.ShapeDtypeStruct((n * t, ATTN_WIDTH), BF16), upd, upd),
        input_output_aliases={6: 1, 7: 2},
        scratch_shapes=[pltpu.VMEM((SAMPLE_KEYS_PAD, ATTN_WIDTH), BF16),
                        pltpu.VMEM((SAMPLE_KEYS_PAD, ATTN_WIDTH), BF16)],
        compiler_params=_params(("arbitrary",)),
        name="attn_sample",
    )(q, kf, vf, cache_k.reshape(depth, n, rows, HEAD_DIM), cache_v.reshape(depth, n, rows, HEAD_DIM),
      bias, k_upd, v_upd)


def _merge_kernel(*refs, nseq, t, blocks_per_seq, pos0, cache_blocks, new_rows):
    x_ref, p_ref, hist_ref, a_ref, gate_ref, pw_ref, ps_ref, wa_ref, wb_ref, wo_ref = refs[:10]
    i = pl.program_id(0)
    if cache_blocks:
        ck_ref, cv_ref, o_ref, ko_ref, vo_ref, xp_scr, stage_scr, d_scr = refs[10:]
        pl.when(i < cache_blocks)(
            functools.partial(_shift_cache_block, ck_ref, ko_ref, new_rows=new_rows))
        pl.when((i >= cache_blocks) & (i < 2 * cache_blocks))(
            functools.partial(_shift_cache_block, cv_ref, vo_ref, new_rows=new_rows))
    else:
        o_ref, xp_scr, stage_scr, d_scr = refs[10:]
    row = lax.broadcasted_iota(jnp.int32, (t, 1), 0)
    if blocks_per_seq is None:
        pos = pos0 + row
    else:
        pos = (i % blocks_per_seq) * t + row
    gated_b = gate_ref[:, D_MODEL:] * _dot(a_ref[...], wb_ref[...])

    t0 = POOL_PAD + HIST_ROWS
    xp_scr[0:POOL_PAD, :] = jnp.zeros((POOL_PAD, POOL_WIDTH), F32)
    stage_scr[:, 0:POOL_PAD, :] = jnp.zeros((2, POOL_PAD, POOL_GROUP), F32)
    for s in range(nseq):
        rows = slice(s * t, (s + 1) * t)
        hist = hist_ref[s * HIST_ROWS:(s + 1) * HIST_ROWS, :]
        if blocks_per_seq is not None:
            hist = jnp.where(i % blocks_per_seq == 0, 0.0, hist)
        xp_scr[POOL_PAD:t0, :] = hist
        xp_scr[t0:, :] = p_ref[rows, :]
        for gi, w in enumerate(POOL_WINDOWS):
            cols = slice(gi * POOL_GROUP, (gi + 1) * POOL_GROUP)
            src = xp_scr.at[:, cols]
            shift = 1
            while 2 * shift < w:
                dst = stage_scr.at[(shift.bit_length() - 1) % 2]
                dst[POOL_PAD:, :] = src[POOL_PAD:, :] + src[POOL_PAD - shift:t0 + t - shift, :]
                src = dst
                shift *= 2
            acc = src[t0:, :] + src[t0 - shift:t0 + t - shift, :]
            inv_cnt = 1.0 / jnp.minimum(pos + 1, w).astype(F32)
            d_scr[rows, cols] = (acc * inv_cnt - p_ref[rows, cols]).astype(BF16)
    pooled = jnp.concatenate(
        [_dot(d_scr[:, gi * POOL_GROUP:(gi + 1) * POOL_GROUP], pw_ref[gi]) for gi in range(len(POOL_WINDOWS))],
        axis=-1) * ps_ref[...]
    branch_a = _dot(pooled.astype(BF16), wa_ref[...])
    merged = gate_ref[:, :D_MODEL] * branch_a + gated_b
    o_ref[...] = x_ref[...] + _dot(merged.astype(BF16), wo_ref[...])


def _merge(x, p, hist, attn, gates, pool_w, pool_scale, wa, wb, wo, *, layer, tm, nseq, t, blocks_per_seq, pos0,
           hist_spec, caches=None, new_frames=0):
    m = x.shape[0]
    steps = m // tm

    def row_spec(width):
        return pl.BlockSpec((tm, width), lambda i: (i, 0))

    operands = [x, p, hist, attn, gates, pool_w, pool_scale, wa, wb, wo]
    in_specs = [row_spec(D_MODEL), row_spec(POOL_WIDTH), hist_spec, row_spec(ATTN_WIDTH),
                row_spec(2 * D_MODEL),
                _layer_spec(pool_w.shape[1:], 0), _layer_spec(pool_scale.shape[1:], layer),
                _layer_spec(wa.shape[1:], 0), _layer_spec(wb.shape[1:], 0),
                _layer_spec(wo.shape[1:], 0)]
    out_specs = [row_spec(D_MODEL)]
    out_shape = [jax.ShapeDtypeStruct((m, D_MODEL), F32)]
    cache_blocks = 0
    if caches is not None:
        depth, n, past = caches[0].shape[:3]
        cache_blocks = depth * n
        assert steps >= 2 * cache_blocks
        rows = past * N_HEADS
        k_spec = pl.BlockSpec((None, rows, HEAD_DIM), lambda i: (jnp.minimum(i, cache_blocks - 1), 0, 0))
        v_spec = pl.BlockSpec((None, rows, HEAD_DIM),
                              lambda i: (jnp.clip(i - cache_blocks, 0, cache_blocks - 1), 0, 0))
        operands += [c.reshape(cache_blocks, rows, HEAD_DIM) for c in caches]
        in_specs += [k_spec, v_spec]
        out_specs += [k_spec, v_spec]
        out_shape += [jax.ShapeDtypeStruct((cache_blocks, rows, HEAD_DIM), F32)] * 2
    outs = pl.pallas_call(
        functools.partial(_merge_kernel, nseq=nseq, t=t, blocks_per_seq=blocks_per_seq, pos0=pos0,
                          cache_blocks=cache_blocks, new_rows=new_frames * N_HEADS),
        grid=(steps,),
        in_specs=in_specs,
        out_specs=tuple(out_specs),
        out_shape=tuple(out_shape),
        scratch_shapes=[pltpu.VMEM((POOL_PAD + HIST_ROWS + t, POOL_WIDTH), F32),
                        pltpu.VMEM((2, POOL_PAD + HIST_ROWS + t, POOL_GROUP), F32),
                        pltpu.VMEM((tm, POOL_WIDTH), BF16)],
        compiler_params=_params(("arbitrary",)),
        name="merge",
    )(*operands)
    if caches is None:
        return outs[0]
    return outs[0], outs[1].reshape(depth, n, rows, HEAD_DIM), outs[2].reshape(depth, n, rows, HEAD_DIM)


FFN_TF = 512


def _ffn_kernel(x_ref, xs_ref, g2_ref, wu_ref, wd_ref, o_ref, os_ref, h_scr, hs_scr):
    i = pl.program_id(0)
    j = pl.program_id(1)

    def start(src_ref, acc_ref, scr):
        x = src_ref[...]
        scr[...] = _rms(x, g2_ref[...]).astype(BF16)
        acc_ref[...] = x

    def step(acc_ref, scr):
        u = jnp.square(jnp.maximum(_dot(scr[...], wu_ref[...].astype(BF16)), 0.0))
        acc_ref[...] += _dot(u.astype(BF16), wd_ref[...].astype(BF16))

    pl.when(j == 0)(functools.partial(start, x_ref, o_ref, h_scr))
    pl.when((i == 0) & (j == 0))(functools.partial(start, xs_ref, os_ref, hs_scr))
    step(o_ref, h_scr)
    pl.when(i == 0)(functools.partial(step, os_ref, hs_scr))


def _ffn(x, xs, g2, wu, wd, *, layer, tm):
    m = x.shape[0]
    ms = xs.shape[0]
    sample_spec = pl.BlockSpec((ms, D_MODEL), lambda i, j: (0, 0))
    return pl.pallas_call(
        _ffn_kernel,
        grid=(m // tm, D_FF // FFN_TF),
        in_specs=[pl.BlockSpec((tm, D_MODEL), lambda i, j: (i, 0)),
                  sample_spec,
                  pl.BlockSpec((None, 1, D_MODEL), lambda i, j: (layer, 0, 0)),
                  pl.BlockSpec((None, D_MODEL, FFN_TF), lambda i, j: (layer, 0, j)),
                  pl.BlockSpec((None, FFN_TF, D_MODEL), lambda i, j: (layer, j, 0))],
        out_specs=(pl.BlockSpec((tm, D_MODEL), lambda i, j: (i, 0)), sample_spec),
        out_shape=(jax.ShapeDtypeStruct((m, D_MODEL), F32), jax.ShapeDtypeStruct((ms, D_MODEL), F32)),
        scratch_shapes=[pltpu.VMEM((tm, D_MODEL), BF16), pltpu.VMEM((ms, D_MODEL), BF16)],
        compiler_params=_params(("arbitrary", "arbitrary")),
        name="ffn",
    )(x, xs, g2, wu, wd)


PROMPT_TM = 1024
MERGE_TM = 256


def kernel(x_prompt, x_sample, state_pool, cache_k, cache_v, norm1_g, w_in, b_gate, pool_w, pool_scale,
           q_norm_g, k_norm_g, rel_bias, w_branch_a, w_branch_b, w_out, norm2_g, w_up, w_down):
    nb, seq, _ = x_prompt.shape
    ns, t, _ = x_sample.shape
    depth = w_in.shape[0]
    past = cache_k.shape[2]
    assert seq % PROMPT_TM == 0 and seq % MERGE_TM == 0 and seq % Q_BLOCK == 0 and seq >= BAND_PAST
    assert t >= POOL_HIST and t % HIST_ROWS == 0 and past + t <= SAMPLE_KEYS_PAD and past <= BAND_PAST
    assert past % t == 0 and past > t

    xp = x_prompt.reshape(nb * seq, D_MODEL)
    xs = x_sample.reshape(ns * t, D_MODEL)
    blocks_per_seq = seq // MERGE_TM
    prompt_hist_spec = pl.BlockSpec(
        (HIST_ROWS, POOL_WIDTH), lambda i: (jnp.maximum(i * (MERGE_TM // HIST_ROWS) - 1, 0), 0))
    sample_hist_spec = pl.BlockSpec((ns * HIST_ROWS, POOL_WIDTH), lambda i: (0, 0))

    g1 = norm1_g.reshape(depth, 1, D_MODEL)
    g2 = norm2_g.reshape(depth, 1, D_MODEL)
    bg = b_gate.reshape(depth, 1, 2 * D_MODEL)
    qg = q_norm_g.reshape(depth, 1, HEAD_DIM)
    kg = k_norm_g.reshape(depth, 1, HEAD_DIM)
    ps = pool_scale.reshape(depth, 1, POOL_WIDTH)
    w_mixed, w_gate = _regroup_in_weights(w_in, 0)
    merge_weights = (pool_w.reshape(depth, POOL_WIDTH, POOL_GROUP), w_branch_a, w_branch_b, w_out)
    hist_s = jnp.pad(state_pool, ((0, 0), (0, 0), (HIST_ROWS - POOL_HIST, 0), (0, 0)))
    hist_s = hist_s.reshape(depth, ns * HIST_ROWS, POOL_WIDTH)

    bias_p = _bias_table(rel_bias, q_rows=Q_BLOCK, keys=KEY_BLOCKS * Q_BLOCK,
                         offset=(KEY_BLOCKS - 1) * Q_BLOCK, visible_fn=_prompt_visible)
    bias_s = _bias_table(rel_bias, q_rows=t, keys=SAMPLE_KEYS_PAD, offset=past,
                         visible_fn=lambda qi, kj: kj < past + t)

    k_upd = v_upd = None
    pool_p, kp_l, vp_l, pool_s = [], [], [], []
    for l in range(depth):
        (p, q, k, v, kf, vf, gates,
         p_s, q_s, kf_s, vf_s, gates_s) = _in_proj(xp, xs, g1, w_mixed, w_gate, bg, qg, kg, layer=l, tm=PROMPT_TM)

        attn, pw, wa, wb, wo, *next_w_in = _attn_prompt(
            q, k, v, bias_p, layer=l, n=nb, s=seq, row_casts=merge_weights,
            w_in=w_in if l + 1 < depth else None)
        pw = pw.reshape(1, len(POOL_WINDOWS), POOL_GROUP, POOL_GROUP)
        if next_w_in:
            w_mixed, w_gate = next_w_in
        pool_p.append(p.reshape(nb, seq, POOL_WIDTH)[:, seq - POOL_HIST:])
        kp_l.append(kf.reshape(nb, seq, ATTN_WIDTH)[:, seq - BAND_PAST:].reshape(nb, BAND_PAST, N_HEADS, HEAD_DIM))
        vp_l.append(vf.reshape(nb, seq, ATTN_WIDTH)[:, seq - BAND_PAST:].reshape(nb, BAND_PAST, N_HEADS, HEAD_DIM))
        merge_p = functools.partial(_merge, xp, p, p, attn, gates, pw, ps, wa, wb, wo, layer=l, tm=MERGE_TM,
                                    nseq=1, t=MERGE_TM, blocks_per_seq=blocks_per_seq, pos0=None,
                                    hist_spec=prompt_hist_spec)
        if l == 0:
            xp, k_upd, v_upd = merge_p(caches=(cache_k, cache_v), new_frames=t)
        else:
            xp = merge_p()

        attn_s, k_upd, v_upd = _attn_sample(q_s, kf_s, vf_s, cache_k, cache_v, bias_s, k_upd, v_upd,
                                            layer=l, n=ns, t=t)
        pool_s.append(p_s.reshape(ns, t, POOL_WIDTH)[:, t - POOL_HIST:])
        xs = _merge(xs, p_s, hist_s[l], attn_s, gates_s, pw, ps, wa, wb, wo, layer=l, tm=ns * t, nseq=ns, t=t,
                    blocks_per_seq=None, pos0=PAST_LEN, hist_spec=sample_hist_spec)

        xp, xs = _ffn(xp, xs, g2, w_up, w_down, layer=l, tm=PROMPT_TM)

    return (xp.reshape(nb, seq, D_MODEL), xs.reshape(ns, t, D_MODEL),
            jnp.stack(pool_p), jnp.stack(kp_l), jnp.stack(vp_l),
            jnp.stack(pool_s), k_upd.reshape(cache_k.shape), v_upd.reshape(cache_v.shape))
```

```python
import functools

import numpy as np

import jax
import jax.numpy as jnp
from jax import lax
from jax.experimental import pallas as pl
from jax.experimental.pallas import tpu as pltpu

D_MODEL = 2048
PAST_LEN = 1024
CHUNK = 64
N_PREV_CHUNKS = 8
BAND_PAST = N_PREV_CHUNKS * CHUNK
POOL_WIDTH = D_MODEL // 2
POOL_WINDOWS = (2, 4, 8, 16)
POOL_GROUP = POOL_WIDTH // len(POOL_WINDOWS)
POOL_HIST = max(POOL_WINDOWS) - 1
HEAD_DIM = 128
ATTN_WIDTH = D_MODEL // 2
N_HEADS = ATTN_WIDTH // HEAD_DIM
MAX_REL = 128
D_FF = 4 * D_MODEL
EPS = 1e-6
ATTN_SCALE = HEAD_DIM ** -0.5
LOG2E = 1.4426950408889634
Q_SCALE = ATTN_SCALE * LOG2E
NEG_INF = -1e30

BF16 = jnp.bfloat16
F32 = jnp.float32

V7X_VMEM_BYTES = 64 * 1024 * 1024
LANES = 128
BF16_SUBLANES = 16
VMEM_LIMIT_BYTES = V7X_VMEM_BYTES - 1024 * 1024

HIST_ROWS = 16
POOL_PAD = 8
Q_BLOCK = 4 * CHUNK
KEY_BLOCKS = 3
SAMPLE_KEYS_PAD = 640


def _rms(xf, g):
    ms = jnp.mean(xf * xf, axis=-1, keepdims=True)
    return xf * lax.rsqrt(ms + EPS) * g


def _dot(a, b):
    return jnp.dot(a, b, preferred_element_type=F32)


def _dot_t(a, b):
    return lax.dot_general(a, b, (((1,), (1,)), ((), ())), preferred_element_type=F32)


def _params(semantics):
    return pltpu.CompilerParams(dimension_semantics=semantics, vmem_limit_bytes=VMEM_LIMIT_BYTES)


def _layer_spec(shape, layer):
    nd = len(shape)
    return pl.BlockSpec((None,) + tuple(shape), lambda *_: (layer,) + (0,) * nd,
                        pipeline_mode=pl.Buffered(1))


def _bias_table_kernel(ext_ref, o_ref, *, q_rows, keys, visible_fn):
    width = ext_ref.shape[-1]
    qi = lax.broadcasted_iota(jnp.int32, (q_rows, keys), 0)
    kj = lax.broadcasted_iota(jnp.int32, (q_rows, keys), 1)
    visible = visible_fn(qi, kj)
    for g in range(ext_ref.shape[0]):
        row = jnp.broadcast_to(ext_ref[g:g + 1, :], (q_rows, width))
        skew = pltpu.roll(row, 0, 1, stride=1, stride_axis=0)
        o_ref[g] = jnp.where(visible, skew[:, :keys] * LOG2E, NEG_INF)


def _take_static(table, idx):
    pieces = []
    start = 0
    while start < len(idx):
        end = start + 1
        step = int(idx[end] - idx[start]) if end < len(idx) else 0
        if step in (-1, 0, 1):
            while end < len(idx) and int(idx[end] - idx[end - 1]) == step:
                end += 1
        lo, hi = sorted((int(idx[start]), int(idx[end - 1])))
        if step == 0 or end - start == 1:
            pieces.append(jnp.broadcast_to(table[:, lo:lo + 1], (table.shape[0], end - start)))
        else:
            ramp = table[:, lo:hi + 1]
            pieces.append(ramp if step == 1 else ramp[:, ::-1])
        start = end
    return jnp.concatenate(pieces, axis=1)


def _bias_table(rel_bias, *, q_rows, keys, offset, visible_fn):
    groups = rel_bias.shape[0] * rel_bias.shape[1]
    width = -(-(keys + q_rows - 1) // LANES) * LANES
    idx = np.zeros((width,), np.int32)
    idx[:keys] = np.clip(offset - np.arange(keys), -MAX_REL, MAX_REL) + MAX_REL
    back = np.arange(1, q_rows)
    idx[width - back] = np.clip(offset + back, -MAX_REL, MAX_REL) + MAX_REL
    ext = _take_static(rel_bias.reshape(groups, 2 * MAX_REL + 1).astype(F32), idx)
    return pl.pallas_call(
        functools.partial(_bias_table_kernel, q_rows=q_rows, keys=keys, visible_fn=visible_fn),
        out_shape=jax.ShapeDtypeStruct((groups, q_rows, keys), F32),
        compiler_params=pltpu.CompilerParams(vmem_limit_bytes=VMEM_LIMIT_BYTES),
        name="bias_table",
    )(ext)


def _prompt_visible(qi, kj):
    own = qi // CHUNK + (KEY_BLOCKS - 1) * (Q_BLOCK // CHUNK)
    kc = kj // CHUNK
    return (kc <= own) & (kc >= own - N_PREV_CHUNKS)


IN_HEADS = 2
IN_STEPS = N_HEADS // IN_HEADS
IN_SEG_TN = IN_HEADS * HEAD_DIM
IN_GATE_TN = 2 * D_MODEL // IN_STEPS


IN_SEG_ORDER = (1, 2, 0, 3)
IN_SLABS = 4 * ATTN_WIDTH // HEAD_DIM


def _regroup_kernel(wq_ref, wk_ref, wp_ref, wv_ref, wg_ref, mixed_ref, gate_ref):
    for s, src in enumerate((wq_ref, wk_ref, wp_ref, wv_ref)):
        mixed_ref[:, s * IN_SEG_TN:(s + 1) * IN_SEG_TN] = src[...].astype(BF16)
    gate_ref[...] = wg_ref[...].astype(BF16)


def _regroup_in_weights(w_in, layer):
    def head_cols(seg):
        return pl.BlockSpec((None, D_MODEL, IN_SEG_TN), lambda j: (layer, 0, seg * IN_STEPS + j))

    gate0 = 4 * ATTN_WIDTH // IN_GATE_TN
    return pl.pallas_call(
        _regroup_kernel,
        grid=(IN_STEPS,),
        in_specs=[head_cols(seg) for seg in IN_SEG_ORDER]
                 + [pl.BlockSpec((None, D_MODEL, IN_GATE_TN), lambda j: (layer, 0, gate0 + j))],
        out_specs=(pl.BlockSpec((None, D_MODEL, 4 * IN_SEG_TN), lambda j: (0, 0, j)),
                   pl.BlockSpec((None, D_MODEL, IN_GATE_TN), lambda j: (0, 0, j))),
        out_shape=(jax.ShapeDtypeStruct((1, D_MODEL, 4 * ATTN_WIDTH), BF16),
                   jax.ShapeDtypeStruct((1, D_MODEL, 2 * D_MODEL), BF16)),
        compiler_params=_params(("parallel",)),
        name="regroup_w_in",
    )(w_in, w_in, w_in, w_in, w_in)


def _regroup_slab_specs(layer, step_fn):
    def mixed_src(*ids):
        s = step_fn(*ids)
        group, within = s // (4 * IN_HEADS), s % (4 * IN_HEADS)
        seg, head = within // IN_HEADS, group * IN_HEADS + within % IN_HEADS
        src_seg = sum(jnp.where(seg == pos, src, 0) for pos, src in enumerate(IN_SEG_ORDER))
        return (layer, 0, src_seg * N_HEADS + head)

    slab = (None, D_MODEL, HEAD_DIM)
    in_specs = [pl.BlockSpec(slab, mixed_src),
                pl.BlockSpec(slab, lambda *ids: (layer, 0, IN_SLABS + step_fn(*ids)))]
    out_specs = [pl.BlockSpec(slab, lambda *ids: (0, 0, step_fn(*ids)))] * 2
    return in_specs, out_specs


def _in_rows(h, wm_ref, wg_ref, bg_ref, qg_ref, kg_ref, p_ref, q_ref, kf_ref, vf_ref, gate_ref,
             k_ref=None, v_ref=None):
    gate_ref[...] = jax.nn.sigmoid(_dot(h, wg_ref[...]) + bg_ref[...])
    z = _dot(h, wm_ref[...])
    for hd in range(IN_HEADS):
        sl = slice(hd * HEAD_DIM, (hd + 1) * HEAD_DIM)
        q_ref[:, sl] = (_rms(z[:, sl], qg_ref[...]) * Q_SCALE).astype(BF16)
        kn = _rms(z[:, IN_SEG_TN + hd * HEAD_DIM:IN_SEG_TN + (hd + 1) * HEAD_DIM], kg_ref[...])
        kf_ref[:, sl] = kn
        if k_ref is not None:
            k_ref[:, sl] = kn.astype(BF16)
    p_ref[...] = z[:, 2 * IN_SEG_TN:3 * IN_SEG_TN]
    v = z[:, 3 * IN_SEG_TN:4 * IN_SEG_TN]
    vf_ref[...] = v
    if v_ref is not None:
        v_ref[...] = v.astype(BF16)


def _in_kernel(x_ref, xs_ref, g1_ref, wm_ref, wg_ref, bg_ref, qg_ref, kg_ref,
               p_ref, q_ref, k_ref, v_ref, kf_ref, vf_ref, gate_ref,
               ps_ref, qs_ref, kfs_ref, vfs_ref, gates_ref, h_scr, hs_scr):
    i = pl.program_id(0)
    j = pl.program_id(1)
    weights = (wm_ref, wg_ref, bg_ref, qg_ref, kg_ref)

    @pl.when(j == 0)
    def _():
        h_scr[...] = _rms(x_ref[...], g1_ref[...]).astype(BF16)

    @pl.when((i == 0) & (j == 0))
    def _():
        hs_scr[...] = _rms(xs_ref[...], g1_ref[...]).astype(BF16)

    _in_rows(h_scr[...], *weights, p_ref, q_ref, kf_ref, vf_ref, gate_ref, k_ref, v_ref)

    @pl.when(i == 0)
    def _():
        _in_rows(hs_scr[...], *weights, ps_ref, qs_ref, kfs_ref, vfs_ref, gates_ref)


def _in_proj(x, xs, g1, w_mixed, w_gate, b_gate, qg, kg, *, layer, tm):
    m = x.shape[0]
    ms = xs.shape[0]
    head_spec = pl.BlockSpec((tm, IN_SEG_TN), lambda i, j: (i, j))

    def sample_spec(width):
        return pl.BlockSpec((ms, width), lambda i, j: (0, jnp.where(i == 0, j, IN_STEPS - 1)))

    out_shape = (
        jax.ShapeDtypeStruct((m, POOL_WIDTH), F32),
        jax.ShapeDtypeStruct((m, ATTN_WIDTH), BF16),
        jax.ShapeDtypeStruct((m, ATTN_WIDTH), BF16),
        jax.ShapeDtypeStruct((m, ATTN_WIDTH), BF16),
        jax.ShapeDtypeStruct((m, ATTN_WIDTH), F32),
        jax.ShapeDtypeStruct((m, ATTN_WIDTH), F32),
        jax.ShapeDtypeStruct((m, 2 * D_MODEL), F32),
        jax.ShapeDtypeStruct((ms, POOL_WIDTH), F32),
        jax.ShapeDtypeStruct((ms, ATTN_WIDTH), BF16),
        jax.ShapeDtypeStruct((ms, ATTN_WIDTH), F32),
        jax.ShapeDtypeStruct((ms, ATTN_WIDTH), F32),
        jax.ShapeDtypeStruct((ms, 2 * D_MODEL), F32),
    )
    return pl.pallas_call(
        _in_kernel,
        grid=(m // tm, IN_STEPS),
        in_specs=[
            pl.BlockSpec((tm, D_MODEL), lambda i, j: (i, 0)),
            pl.BlockSpec((ms, D_MODEL), lambda i, j: (0, 0)),
            pl.BlockSpec((None, 1, D_MODEL), lambda i, j: (layer, 0, 0)),
            pl.BlockSpec((None, D_MODEL, 4 * IN_SEG_TN), lambda i, j: (0, 0, j)),
            pl.BlockSpec((None, D_MODEL, IN_GATE_TN), lambda i, j: (0, 0, j)),
            pl.BlockSpec((None, 1, IN_GATE_TN), lambda i, j: (layer, 0, j)),
            pl.BlockSpec((None, 1, HEAD_DIM), lambda i, j: (layer, 0, 0)),
            pl.BlockSpec((None, 1, HEAD_DIM), lambda i, j: (layer, 0, 0)),
        ],
        out_specs=(head_spec,) * 6 + (pl.BlockSpec((tm, IN_GATE_TN), lambda i, j: (i, j)),)
                  + (sample_spec(IN_SEG_TN),) * 4 + (sample_spec(IN_GATE_TN),),
        out_shape=out_shape,
        scratch_shapes=[pltpu.VMEM((tm, D_MODEL), BF16), pltpu.VMEM((ms, D_MODEL), BF16)],
        compiler_params=_params(("arbitrary", "arbitrary")),
        name="in_proj",
    )(x, xs, g1, w_mixed, w_gate, b_gate, qg, kg)


def _attn_prompt_kernel(q_ref, k_new_ref, v_new_ref, bias_ref, *rest):
    i = pl.program_id(1)
    k_ref, v_ref = rest[-2:]
    n_casts = (len(rest) - 3) // 2
    o_ref = rest[n_casts]
    for src, dst in zip(rest[:n_casts], rest[n_casts + 1:-2]):
        dst[...] = src[...].astype(BF16)

    def heads(n_valid):
        first = KEY_BLOCKS - n_valid
        for win, new in ((k_ref, k_new_ref), (v_ref, v_new_ref)):
            for slot in range(first, KEY_BLOCKS - 1):
                win[slot * Q_BLOCK:(slot + 1) * Q_BLOCK, :] = win[(slot + 1) * Q_BLOCK:(slot + 2) * Q_BLOCK, :]
            win[(KEY_BLOCKS - 1) * Q_BLOCK:, :] = new[...]
        for h in range(N_HEADS):
            sl = slice(h * HEAD_DIM, (h + 1) * HEAD_DIM)
            qh = q_ref[:, sl]
            m = l = o = None
            for d in range(first, KEY_BLOCKS):
                rows = slice(d * Q_BLOCK, (d + 1) * Q_BLOCK)
                s = _dot_t(qh, k_ref[rows, sl]) + bias_ref[h, :, rows]
                m_r = jnp.max(s, axis=-1, keepdims=True)
                if m is None:
                    m = m_r
                    e = jnp.exp2(s - m)
                    l = jnp.sum(e, axis=-1, keepdims=True)
                    o = _dot(e.astype(BF16), v_ref[rows, sl])
                else:
                    m_new = jnp.maximum(m, m_r)
                    alpha = jnp.exp2(m - m_new)
                    e = jnp.exp2(s - m_new)
                    l = alpha * l + jnp.sum(e, axis=-1, keepdims=True)
                    o = alpha * o + _dot(e.astype(BF16), v_ref[rows, sl])
                    m = m_new
            o_ref[:, sl] = (o / l).astype(BF16)

    for n_valid in range(1, KEY_BLOCKS):
        pl.when(i == n_valid - 1)(functools.partial(heads, n_valid))
    pl.when(i >= KEY_BLOCKS - 1)(functools.partial(heads, KEY_BLOCKS))


def _attn_prompt(q, k, v, bias, *, layer, n, s, row_casts=(), w_in=None):
    nb = s // Q_BLOCK
    steps = n * nb
    q_spec = pl.BlockSpec((Q_BLOCK, ATTN_WIDTH), lambda b, i: (b * nb + i, 0))
    operands = [q, k, v, bias]
    extra_in, extra_out, extra_shape = [], [], []
    for w in row_casts:
        _, rows, cols = w.shape
        block = (None, rows // steps, cols)
        assert rows % (steps * BF16_SUBLANES) == 0
        extra_in.append(pl.BlockSpec(block, lambda b, i: (layer, b * nb + i, 0)))
        extra_out.append(pl.BlockSpec(block, lambda b, i: (0, b * nb + i, 0)))
        extra_shape.append(jax.ShapeDtypeStruct((1, rows, cols), BF16))
        operands.append(w)
    if w_in is not None:
        assert steps == IN_SLABS
        slab_in, slab_out = _regroup_slab_specs(layer + 1, lambda b, i: b * nb + i)
        extra_in += slab_in
        extra_out += slab_out
        extra_shape += [jax.ShapeDtypeStruct((1, D_MODEL, 4 * ATTN_WIDTH), BF16),
                        jax.ShapeDtypeStruct((1, D_MODEL, 2 * D_MODEL), BF16)]
        operands += [w_in, w_in]

    bias_spec = pl.BlockSpec((N_HEADS, Q_BLOCK, KEY_BLOCKS * Q_BLOCK), lambda b, i: (layer, 0, 0),
                             pipeline_mode=pl.Buffered(1))
    window = pltpu.VMEM((KEY_BLOCKS * Q_BLOCK, ATTN_WIDTH), BF16)
    outs = pl.pallas_call(
        _attn_prompt_kernel,
        grid=(n, nb),
        in_specs=[q_spec, q_spec, q_spec, bias_spec] + extra_in,
        out_specs=tuple([q_spec] + extra_out),
        out_shape=tuple([jax.ShapeDtypeStruct((n * s, ATTN_WIDTH), BF16)] + extra_shape),
        scratch_shapes=[window, window],
        compiler_params=_params(("arbitrary", "arbitrary")),
        name="attn_prompt",
    )(*operands)
    return outs


def _shift_cache_block(src_ref, dst_ref, *, new_rows):
    keep = src_ref.shape[0] - new_rows
    dst_ref[0:keep, :] = src_ref[new_rows:, :]
    dst_ref[keep:, :] = jnp.zeros((new_rows, HEAD_DIM), F32)


def _attn_sample_kernel(q_ref, kf_ref, vf_ref, ck_ref, cv_ref, bias_ref, k_alias, v_alias,
                        o_ref, ko_ref, vo_ref, kk_scr, vv_scr, *, t, past):
    del k_alias, v_alias
    for new_ref, cache_ref, out_ref, scr in ((kf_ref, ck_ref, ko_ref, kk_scr),
                                             (vf_ref, cv_ref, vo_ref, vv_scr)):
        for h in range(N_HEADS):
            sl = slice(h * HEAD_DIM, (h + 1) * HEAD_DIM)
            out_ref[pl.ds(h, t, stride=N_HEADS), :] = new_ref[:, sl]
            scr[0:past, sl] = cache_ref[pl.ds(h, past, stride=N_HEADS), :].astype(BF16)
        scr[past:past + t, :] = new_ref[...].astype(BF16)
        scr[past + t:, :] = jnp.zeros((SAMPLE_KEYS_PAD - past - t, ATTN_WIDTH), BF16)
    for h in range(N_HEADS):
        sl = slice(h * HEAD_DIM, (h + 1) * HEAD_DIM)
        s = _dot_t(q_ref[:, sl], kk_scr[:, sl]) + bias_ref[h]
        m = jnp.max(s, axis=-1, keepdims=True)
        e = jnp.exp2(s - m)
        l = jnp.sum(e, axis=-1, keepdims=True)
        o = _dot(e.astype(BF16), vv_scr[:, sl])
        o_ref[:, sl] = (o / l).astype(BF16)


def _attn_sample(q, kf, vf, cache_k, cache_v, bias, k_upd, v_upd, *, layer, n, t):
    depth, _, past = cache_k.shape[:3]
    rows = past * N_HEADS
    new_rows = t * N_HEADS
    row_spec = pl.BlockSpec((t, ATTN_WIDTH), lambda b: (b, 0))
    cache_spec = pl.BlockSpec((None, None, rows, HEAD_DIM), lambda b: (layer, b, 0, 0))
    new_spec = pl.BlockSpec((None, None, new_rows, HEAD_DIM), lambda b: (layer, b, rows // new_rows - 1, 0))
    bias_spec = pl.BlockSpec((N_HEADS, t, SAMPLE_KEYS_PAD), lambda b: (layer, 0, 0),
                             pipeline_mode=pl.Buffered(1))
    any_spec = pl.BlockSpec(memory_space=pl.ANY)
    upd = jax.ShapeDtypeStruct(k_upd.shape, F32)
    return pl.pallas_call(
        functools.partial(_attn_sample_kernel, t=t, past=past),
        grid=(n,),
        in_specs=[row_spec, row_spec, row_spec, cache_spec, cache_spec, bias_spec, any_spec, any_spec],
        out_specs=(row_spec, new_spec, new_spec),
        out_shape=(jax.ShapeDtypeStruct((n * t, ATTN_WIDTH), BF16), upd, upd),
        input_output_aliases={6: 1, 7: 2},
        scratch_shapes=[pltpu.VMEM((SAMPLE_KEYS_PAD, ATTN_WIDTH), BF16),
                        pltpu.VMEM((SAMPLE_KEYS_PAD, ATTN_WIDTH), BF16)],
        compiler_params=_params(("arbitrary",)),
        name="attn_sample",
    )(q, kf, vf, cache_k.reshape(depth, n, rows, HEAD_DIM), cache_v.reshape(depth, n, rows, HEAD_DIM),
      bias, k_upd, v_upd)


def _merge_kernel(*refs, nseq, t, blocks_per_seq, pos0, cache_blocks, new_rows):
    x_ref, p_ref, hist_ref, a_ref, gate_ref, pw_ref, ps_ref, wa_ref, wb_ref, wo_ref = refs[:10]
    i = pl.program_id(0)
    if cache_blocks:
        ck_ref, cv_ref, o_ref, ko_ref, vo_ref, xp_scr, stage_scr, d_scr = refs[10:]
        pl.when(i < cache_blocks)(
            functools.partial(_shift_cache_block, ck_ref, ko_ref, new_rows=new_rows))
        pl.when((i >= cache_blocks) & (i < 2 * cache_blocks))(
            functools.partial(_shift_cache_block, cv_ref, vo_ref, new_rows=new_rows))
    else:
        o_ref, xp_scr, stage_scr, d_scr = refs[10:]
    row = lax.broadcasted_iota(jnp.int32, (t, 1), 0)
    if blocks_per_seq is None:
        pos = pos0 + row
    else:
        pos = (i % blocks_per_seq) * t + row
    gated_b = gate_ref[:, D_MODEL:] * _dot(a_ref[...], wb_ref[...])

    t0 = POOL_PAD + HIST_ROWS
    xp_scr[0:POOL_PAD, :] = jnp.zeros((POOL_PAD, POOL_WIDTH), F32)
    stage_scr[:, 0:POOL_PAD, :] = jnp.zeros((2, POOL_PAD, POOL_GROUP), F32)
    for s in range(nseq):
        rows = slice(s * t, (s + 1) * t)
        hist = hist_ref[s * HIST_ROWS:(s + 1) * HIST_ROWS, :]
        if blocks_per_seq is not None:
            hist = jnp.where(i % blocks_per_seq == 0, 0.0, hist)
        xp_scr[POOL_PAD:t0, :] = hist
        xp_scr[t0:, :] = p_ref[rows, :]
        for gi, w in enumerate(POOL_WINDOWS):
            cols = slice(gi * POOL_GROUP, (gi + 1) * POOL_GROUP)
            src = xp_scr.at[:, cols]
            shift = 1
            while 2 * shift < w:
                dst = stage_scr.at[(shift.bit_length() - 1) % 2]
                dst[POOL_PAD:, :] = src[POOL_PAD:, :] + src[POOL_PAD - shift:t0 + t - shift, :]
                src = dst
                shift *= 2
            acc = src[t0:, :] + src[t0 - shift:t0 + t - shift, :]
            inv_cnt = 1.0 / jnp.minimum(pos + 1, w).astype(F32)
            d_scr[rows, cols] = (acc * inv_cnt - p_ref[rows, cols]).astype(BF16)
    pooled = jnp.concatenate(
        [_dot(d_scr[:, gi * POOL_GROUP:(gi + 1) * POOL_GROUP], pw_ref[gi]) for gi in range(len(POOL_WINDOWS))],
        axis=-1) * ps_ref[...]
    branch_a = _dot(pooled.astype(BF16), wa_ref[...])
    merged = gate_ref[:, :D_MODEL] * branch_a + gated_b
    o_ref[...] = x_ref[...] + _dot(merged.astype(BF16), wo_ref[...])


def _merge(x, p, hist, attn, gates, pool_w, pool_scale, wa, wb, wo, *, layer, tm, nseq, t, blocks_per_seq, pos0,
           hist_spec, caches=None, new_frames=0):
    m = x.shape[0]
    steps = m // tm

    def row_spec(width):
        return pl.BlockSpec((tm, width), lambda i: (i, 0))

    operands = [x, p, hist, attn, gates, pool_w, pool_scale, wa, wb, wo]
    in_specs = [row_spec(D_MODEL), row_spec(POOL_WIDTH), hist_spec, row_spec(ATTN_WIDTH),
                row_spec(2 * D_MODEL),
                _layer_spec(pool_w.shape[1:], 0), _layer_spec(pool_scale.shape[1:], layer),
                _layer_spec(wa.shape[1:], 0), _layer_spec(wb.shape[1:], 0),
                _layer_spec(wo.shape[1:], 0)]
    out_specs = [row_spec(D_MODEL)]
    out_shape = [jax.ShapeDtypeStruct((m, D_MODEL), F32)]
    cache_blocks = 0
    if caches is not None:
        depth, n, past = caches[0].shape[:3]
        cache_blocks = depth * n
        assert steps >= 2 * cache_blocks
        rows = past * N_HEADS
        k_spec = pl.BlockSpec((None, rows, HEAD_DIM), lambda i: (jnp.minimum(i, cache_blocks - 1), 0, 0))
        v_spec = pl.BlockSpec((None, rows, HEAD_DIM),
                              lambda i: (jnp.clip(i - cache_blocks, 0, cache_blocks - 1), 0, 0))
        operands += [c.reshape(cache_blocks, rows, HEAD_DIM) for c in caches]
        in_specs += [k_spec, v_spec]
        out_specs += [k_spec, v_spec]
        out_shape += [jax.ShapeDtypeStruct((cache_blocks, rows, HEAD_DIM), F32)] * 2
    outs = pl.pallas_call(
        functools.partial(_merge_kernel, nseq=nseq, t=t, blocks_per_seq=blocks_per_seq, pos0=pos0,
                          cache_blocks=cache_blocks, new_rows=new_frames * N_HEADS),
        grid=(steps,),
        in_specs=in_specs,
        out_specs=tuple(out_specs),
        out_shape=tuple(out_shape),
        scratch_shapes=[pltpu.VMEM((POOL_PAD + HIST_ROWS + t, POOL_WIDTH), F32),
                        pltpu.VMEM((2, POOL_PAD + HIST_ROWS + t, POOL_GROUP), F32),
                        pltpu.VMEM((tm, POOL_WIDTH), BF16)],
        compiler_params=_params(("arbitrary",)),
        name="merge",
    )(*operands)
    if caches is None:
        return outs[0]
    return outs[0], outs[1].reshape(depth, n, rows, HEAD_DIM), outs[2].reshape(depth, n, rows, HEAD_DIM)


FFN_TF = 512


def _ffn_kernel(x_ref, xs_ref, g2_ref, wu_ref, wd_ref, o_ref, os_ref, h_scr, hs_scr):
    i = pl.program_id(0)
    j = pl.program_id(1)

    def delta(scr):
        u = jnp.square(jnp.maximum(_dot(scr[...], wu_ref[...].astype(BF16)), 0.0))
        return _dot(u.astype(BF16), wd_ref[...].astype(BF16))

    def first_step(src_ref, acc_ref, scr):
        x = src_ref[...]
        scr[...] = _rms(x, g2_ref[...]).astype(BF16)
        acc_ref[...] = x + delta(scr)

    def later_step(acc_ref, scr):
        acc_ref[...] += delta(scr)

    pl.when(j == 0)(functools.partial(first_step, x_ref, o_ref, h_scr))
    pl.when(j > 0)(functools.partial(later_step, o_ref, h_scr))
    pl.when((i == 0) & (j == 0))(functools.partial(first_step, xs_ref, os_ref, hs_scr))
    pl.when((i == 0) & (j > 0))(functools.partial(later_step, os_ref, hs_scr))


def _ffn(x, xs, g2, wu, wd, *, layer, tm):
    m = x.shape[0]
    ms = xs.shape[0]
    sample_spec = pl.BlockSpec((ms, D_MODEL), lambda i, j: (0, 0))
    return pl.pallas_call(
        _ffn_kernel,
        grid=(m // tm, D_FF // FFN_TF),
        in_specs=[pl.BlockSpec((tm, D_MODEL), lambda i, j: (i, 0)),
                  sample_spec,
                  pl.BlockSpec((None, 1, D_MODEL), lambda i, j: (layer, 0, 0)),
                  pl.BlockSpec((None, D_MODEL, FFN_TF), lambda i, j: (layer, 0, j)),
                  pl.BlockSpec((None, FFN_TF, D_MODEL), lambda i, j: (layer, j, 0))],
        out_specs=(pl.BlockSpec((tm, D_MODEL), lambda i, j: (i, 0)), sample_spec),
        out_shape=(jax.ShapeDtypeStruct((m, D_MODEL), F32), jax.ShapeDtypeStruct((ms, D_MODEL), F32)),
        scratch_shapes=[pltpu.VMEM((tm, D_MODEL), BF16), pltpu.VMEM((ms, D_MODEL), BF16)],
        compiler_params=_params(("arbitrary", "arbitrary")),
        name="ffn",
    )(x, xs, g2, wu, wd)


PROMPT_TM = 1024
MERGE_TM = 256


def kernel(x_prompt, x_sample, state_pool, cache_k, cache_v, norm1_g, w_in, b_gate, pool_w, pool_scale,
           q_norm_g, k_norm_g, rel_bias, w_branch_a, w_branch_b, w_out, norm2_g, w_up, w_down):
    nb, seq, _ = x_prompt.shape
    ns, t, _ = x_sample.shape
    depth = w_in.shape[0]
    past = cache_k.shape[2]
    assert seq % PROMPT_TM == 0 and seq % MERGE_TM == 0 and seq % Q_BLOCK == 0 and seq >= BAND_PAST
    assert t >= POOL_HIST and t % HIST_ROWS == 0 and past + t <= SAMPLE_KEYS_PAD and past <= BAND_PAST
    assert past % t == 0 and past > t

    xp = x_prompt.reshape(nb * seq, D_MODEL)
    xs = x_sample.reshape(ns * t, D_MODEL)
    blocks_per_seq = seq // MERGE_TM
    prompt_hist_spec = pl.BlockSpec(
        (HIST_ROWS, POOL_WIDTH), lambda i: (jnp.maximum(i * (MERGE_TM // HIST_ROWS) - 1, 0), 0))
    sample_hist_spec = pl.BlockSpec((ns * HIST_ROWS, POOL_WIDTH), lambda i: (0, 0))

    g1 = norm1_g.reshape(depth, 1, D_MODEL)
    g2 = norm2_g.reshape(depth, 1, D_MODEL)
    bg = b_gate.reshape(depth, 1, 2 * D_MODEL)
    qg = q_norm_g.reshape(depth, 1, HEAD_DIM)
    kg = k_norm_g.reshape(depth, 1, HEAD_DIM)
    ps = pool_scale.reshape(depth, 1, POOL_WIDTH)
    w_mixed, w_gate = _regroup_in_weights(w_in, 0)
    merge_weights = (pool_w.reshape(depth, POOL_WIDTH, POOL_GROUP), w_branch_a, w_branch_b, w_out)
    hist_s = jnp.pad(state_pool, ((0, 0), (0, 0), (HIST_ROWS - POOL_HIST, 0), (0, 0)))
    hist_s = hist_s.reshape(depth, ns * HIST_ROWS, POOL_WIDTH)

    bias_p = _bias_table(rel_bias, q_rows=Q_BLOCK, keys=KEY_BLOCKS * Q_BLOCK,
                         offset=(KEY_BLOCKS - 1) * Q_BLOCK, visible_fn=_prompt_visible)
    bias_s = _bias_table(rel_bias, q_rows=t, keys=SAMPLE_KEYS_PAD, offset=past,
                         visible_fn=lambda qi, kj: kj < past + t)

    k_upd = v_upd = None
    pool_p, kp_l, vp_l, pool_s = [], [], [], []
    for l in range(depth):
        (p, q, k, v, kf, vf, gates,
         p_s, q_s, kf_s, vf_s, gates_s) = _in_proj(xp, xs, g1, w_mixed, w_gate, bg, qg, kg, layer=l, tm=PROMPT_TM)

        attn, pw, wa, wb, wo, *next_w_in = _attn_prompt(
            q, k, v, bias_p, layer=l, n=nb, s=seq, row_casts=merge_weights,
            w_in=w_in if l + 1 < depth else None)
        pw = pw.reshape(1, len(POOL_WINDOWS), POOL_GROUP, POOL_GROUP)
        if next_w_in:
            w_mixed, w_gate = next_w_in
        pool_p.append(p.reshape(nb, seq, POOL_WIDTH)[:, seq - POOL_HIST:])
        kp_l.append(kf.reshape(nb, seq, ATTN_WIDTH)[:, seq - BAND_PAST:].reshape(nb, BAND_PAST, N_HEADS, HEAD_DIM))
        vp_l.append(vf.reshape(nb, seq, ATTN_WIDTH)[:, seq - BAND_PAST:].reshape(nb, BAND_PAST, N_HEADS, HEAD_DIM))
        merge_p = functools.partial(_merge, xp, p, p, attn, gates, pw, ps, wa, wb, wo, layer=l, tm=MERGE_TM,
                                    nseq=1, t=MERGE_TM, blocks_per_seq=blocks_per_seq, pos0=None,
                                    hist_spec=prompt_hist_spec)
        if l == 0:
            xp, k_upd, v_upd = merge_p(caches=(cache_k, cache_v), new_frames=t)
        else:
            xp = merge_p()

        attn_s, k_upd, v_upd = _attn_sample(q_s, kf_s, vf_s, cache_k, cache_v, bias_s, k_upd, v_upd,
                                            layer=l, n=ns, t=t)
        pool_s.append(p_s.reshape(ns, t, POOL_WIDTH)[:, t - POOL_HIST:])
        xs = _merge(xs, p_s, hist_s[l], attn_s, gates_s, pw, ps, wa, wb, wo, layer=l, tm=ns * t, nseq=ns, t=t,
                    blocks_per_seq=None, pos0=PAST_LEN, hist_spec=sample_hist_spec)

        xp, xs = _ffn(xp, xs, g2, w_up, w_down, layer=l, tm=PROMPT_TM)

    return (xp.reshape(nb, seq, D_MODEL), xs.reshape(ns, t, D_MODEL),
            jnp.stack(pool_p), jnp.stack(kp_l), jnp.stack(vp_l),
            jnp.stack(pool_s), k_upd.reshape(cache_k.shape), v_upd.reshape(cache_v.shape))
```

```python
import functools

import numpy as np

import jax
import jax.numpy as jnp
from jax import lax
from jax.experimental import pallas as pl
from jax.experimental.pallas import tpu as pltpu

D_MODEL = 2048
PAST_LEN = 1024
CHUNK = 64
N_PREV_CHUNKS = 8
BAND_PAST = N_PREV_CHUNKS * CHUNK
POOL_WIDTH = D_MODEL // 2
POOL_WINDOWS = (2, 4, 8, 16)
POOL_GROUP = POOL_WIDTH // len(POOL_WINDOWS)
POOL_HIST = max(POOL_WINDOWS) - 1
HEAD_DIM = 128
ATTN_WIDTH = D_MODEL // 2
N_HEADS = ATTN_WIDTH // HEAD_DIM
MAX_REL = 128
D_FF = 4 * D_MODEL
EPS = 1e-6
ATTN_SCALE = HEAD_DIM ** -0.5
LOG2E = 1.4426950408889634
Q_SCALE = ATTN_SCALE * LOG2E
NEG_INF = -1e30

BF16 = jnp.bfloat16
F32 = jnp.float32

V7X_VMEM_BYTES = 64 * 1024 * 1024
LANES = 128
BF16_SUBLANES = 16
VMEM_LIMIT_BYTES = V7X_VMEM_BYTES - 1024 * 1024

HIST_ROWS = 16
POOL_PAD = 8
Q_BLOCK = 4 * CHUNK
KEY_BLOCKS = 3
SAMPLE_KEYS_PAD = 640


def _rms(xf, g):
    ms = jnp.mean(xf * xf, axis=-1, keepdims=True)
    return xf * lax.rsqrt(ms + EPS) * g


def _dot(a, b):
    return jnp.dot(a, b, preferred_element_type=F32)


def _dot_t(a, b):
    return lax.dot_general(a, b, (((1,), (1,)), ((), ())), preferred_element_type=F32)


def _params(semantics):
    return pltpu.CompilerParams(dimension_semantics=semantics, vmem_limit_bytes=VMEM_LIMIT_BYTES)


def _layer_spec(shape, layer):
    nd = len(shape)
    return pl.BlockSpec((None,) + tuple(shape), lambda *_: (layer,) + (0,) * nd,
                        pipeline_mode=pl.Buffered(1))


def _bias_table_kernel(ext_ref, o_ref, *, q_rows, keys, visible_fn):
    width = ext_ref.shape[-1]
    qi = lax.broadcasted_iota(jnp.int32, (q_rows, keys), 0)
    kj = lax.broadcasted_iota(jnp.int32, (q_rows, keys), 1)
    visible = visible_fn(qi, kj)
    for g in range(ext_ref.shape[0]):
        row = jnp.broadcast_to(ext_ref[g:g + 1, :], (q_rows, width))
        skew = pltpu.roll(row, 0, 1, stride=1, stride_axis=0)
        o_ref[g] = jnp.where(visible, skew[:, :keys] * LOG2E, NEG_INF)


def _take_static(table, idx):
    pieces = []
    start = 0
    while start < len(idx):
        end = start + 1
        step = int(idx[end] - idx[start]) if end < len(idx) else 0
        if step in (-1, 0, 1):
            while end < len(idx) and int(idx[end] - idx[end - 1]) == step:
                end += 1
        lo, hi = sorted((int(idx[start]), int(idx[end - 1])))
        if step == 0 or end - start == 1:
            pieces.append(jnp.broadcast_to(table[:, lo:lo + 1], (table.shape[0], end - start)))
        else:
            ramp = table[:, lo:hi + 1]
            pieces.append(ramp if step == 1 else ramp[:, ::-1])
        start = end
    return jnp.concatenate(pieces, axis=1)


def _bias_table(rel_bias, *, q_rows, keys, offset, visible_fn):
    groups = rel_bias.shape[0] * rel_bias.shape[1]
    width = -(-(keys + q_rows - 1) // LANES) * LANES
    idx = np.zeros((width,), np.int32)
    idx[:keys] = np.clip(offset - np.arange(keys), -MAX_REL, MAX_REL) + MAX_REL
    back = np.arange(1, q_rows)
    idx[width - back] = np.clip(offset + back, -MAX_REL, MAX_REL) + MAX_REL
    ext = _take_static(rel_bias.reshape(groups, 2 * MAX_REL + 1).astype(F32), idx)
    return pl.pallas_call(
        functools.partial(_bias_table_kernel, q_rows=q_rows, keys=keys, visible_fn=visible_fn),
        out_shape=jax.ShapeDtypeStruct((groups, q_rows, keys), F32),
        compiler_params=pltpu.CompilerParams(vmem_limit_bytes=VMEM_LIMIT_BYTES),
        name="bias_table",
    )(ext)


def _prompt_visible(qi, kj):
    own = qi // CHUNK + (KEY_BLOCKS - 1) * (Q_BLOCK // CHUNK)
    kc = kj // CHUNK
    return (kc <= own) & (kc >= own - N_PREV_CHUNKS)


IN_HEADS = 2
IN_STEPS = N_HEADS // IN_HEADS
IN_SEG_TN = IN_HEADS * HEAD_DIM
IN_GATE_TN = 2 * D_MODEL // IN_STEPS


IN_SEG_ORDER = (1, 2, 0, 3)
IN_SLABS = 4 * ATTN_WIDTH // HEAD_DIM


def _regroup_kernel(wq_ref, wk_ref, wp_ref, wv_ref, wg_ref, mixed_ref, gate_ref):
    for s, src in enumerate((wq_ref, wk_ref, wp_ref, wv_ref)):
        mixed_ref[:, s * IN_SEG_TN:(s + 1) * IN_SEG_TN] = src[...].astype(BF16)
    gate_ref[...] = wg_ref[...].astype(BF16)


def _regroup_in_weights(w_in, layer):
    def head_cols(seg):
        return pl.BlockSpec((None, D_MODEL, IN_SEG_TN), lambda j: (layer, 0, seg * IN_STEPS + j))

    gate0 = 4 * ATTN_WIDTH // IN_GATE_TN
    return pl.pallas_call(
        _regroup_kernel,
        grid=(IN_STEPS,),
        in_specs=[head_cols(seg) for seg in IN_SEG_ORDER]
                 + [pl.BlockSpec((None, D_MODEL, IN_GATE_TN), lambda j: (layer, 0, gate0 + j))],
        out_specs=(pl.BlockSpec((None, D_MODEL, 4 * IN_SEG_TN), lambda j: (0, 0, j)),
                   pl.BlockSpec((None, D_MODEL, IN_GATE_TN), lambda j: (0, 0, j))),
        out_shape=(jax.ShapeDtypeStruct((1, D_MODEL, 4 * ATTN_WIDTH), BF16),
                   jax.ShapeDtypeStruct((1, D_MODEL, 2 * D_MODEL), BF16)),
        compiler_params=_params(("parallel",)),
        name="regroup_w_in",
    )(w_in, w_in, w_in, w_in, w_in)


def _regroup_slab_specs(layer, step_fn):
    def mixed_src(*ids):
        s = step_fn(*ids)
        group, within = s // (4 * IN_HEADS), s % (4 * IN_HEADS)
        seg, head = within // IN_HEADS, group * IN_HEADS + within % IN_HEADS
        src_seg = sum(jnp.where(seg == pos, src, 0) for pos, src in enumerate(IN_SEG_ORDER))
        return (layer, 0, src_seg * N_HEADS + head)

    slab = (None, D_MODEL, HEAD_DIM)
    in_specs = [pl.BlockSpec(slab, mixed_src),
                pl.BlockSpec(slab, lambda *ids: (layer, 0, IN_SLABS + step_fn(*ids)))]
    out_specs = [pl.BlockSpec(slab, lambda *ids: (0, 0, step_fn(*ids)))] * 2
    return in_specs, out_specs


def _in_rows(h, wm_ref, wg_ref, bg_ref, qg_ref, kg_ref, p_ref, q_ref, kf_ref, vf_ref, gate_ref,
             k_ref=None, v_ref=None):
    gate_ref[...] = jax.nn.sigmoid(_dot(h, wg_ref[...]) + bg_ref[...])
    z = _dot(h, wm_ref[...])
    for hd in range(IN_HEADS):
        sl = slice(hd * HEAD_DIM, (hd + 1) * HEAD_DIM)
        q_ref[:, sl] = (_rms(z[:, sl], qg_ref[...]) * Q_SCALE).astype(BF16)
        kn = _rms(z[:, IN_SEG_TN + hd * HEAD_DIM:IN_SEG_TN + (hd + 1) * HEAD_DIM], kg_ref[...])
        kf_ref[:, sl] = kn
        if k_ref is not None:
            k_ref[:, sl] = kn.astype(BF16)
    p_ref[...] = z[:, 2 * IN_SEG_TN:3 * IN_SEG_TN]
    v = z[:, 3 * IN_SEG_TN:4 * IN_SEG_TN]
    vf_ref[...] = v
    if v_ref is not None:
        v_ref[...] = v.astype(BF16)


def _in_kernel(x_ref, xs_ref, g1_ref, wm_ref, wg_ref, bg_ref, qg_ref, kg_ref,
               p_ref, q_ref, k_ref, v_ref, kf_ref, vf_ref, gate_ref,
               ps_ref, qs_ref, kfs_ref, vfs_ref, gates_ref, h_scr, hs_scr):
    i = pl.program_id(0)
    j = pl.program_id(1)
    weights = (wm_ref, wg_ref, bg_ref, qg_ref, kg_ref)

    def first_step(src_ref, scr, *outs):
        scr[...] = _rms(src_ref[...], g1_ref[...]).astype(BF16)
        _in_rows(scr[...], *weights, *outs)

    def later_step(scr, *outs):
        _in_rows(scr[...], *weights, *outs)

    prompt_outs = (p_ref, q_ref, kf_ref, vf_ref, gate_ref, k_ref, v_ref)
    sample_outs = (ps_ref, qs_ref, kfs_ref, vfs_ref, gates_ref)
    pl.when(j == 0)(functools.partial(first_step, x_ref, h_scr, *prompt_outs))
    pl.when(j > 0)(functools.partial(later_step, h_scr, *prompt_outs))
    pl.when((i == 0) & (j == 0))(functools.partial(first_step, xs_ref, hs_scr, *sample_outs))
    pl.when((i == 0) & (j > 0))(functools.partial(later_step, hs_scr, *sample_outs))


def _in_proj(x, xs, g1, w_mixed, w_gate, b_gate, qg, kg, *, layer, tm):
    m = x.shape[0]
    ms = xs.shape[0]
    head_spec = pl.BlockSpec((tm, IN_SEG_TN), lambda i, j: (i, j))

    def sample_spec(width):
        return pl.BlockSpec((ms, width), lambda i, j: (0, jnp.where(i == 0, j, IN_STEPS - 1)))

    out_shape = (
        jax.ShapeDtypeStruct((m, POOL_WIDTH), F32),
        jax.ShapeDtypeStruct((m, ATTN_WIDTH), BF16),
        jax.ShapeDtypeStruct((m, ATTN_WIDTH), BF16),
        jax.ShapeDtypeStruct((m, ATTN_WIDTH), BF16),
        jax.ShapeDtypeStruct((m, ATTN_WIDTH), F32),
        jax.ShapeDtypeStruct((m, ATTN_WIDTH), F32),
        jax.ShapeDtypeStruct((m, 2 * D_MODEL), F32),
        jax.ShapeDtypeStruct((ms, POOL_WIDTH), F32),
        jax.ShapeDtypeStruct((ms, ATTN_WIDTH), BF16),
        jax.ShapeDtypeStruct((ms, ATTN_WIDTH), F32),
        jax.ShapeDtypeStruct((ms, ATTN_WIDTH), F32),
        jax.ShapeDtypeStruct((ms, 2 * D_MODEL), F32),
    )
    return pl.pallas_call(
        _in_kernel,
        grid=(m // tm, IN_STEPS),
        in_specs=[
            pl.BlockSpec((tm, D_MODEL), lambda i, j: (i, 0)),
            pl.BlockSpec((ms, D_MODEL), lambda i, j: (0, 0)),
            pl.BlockSpec((None, 1, D_MODEL), lambda i, j: (layer, 0, 0)),
            pl.BlockSpec((None, D_MODEL, 4 * IN_SEG_TN), lambda i, j: (0, 0, j)),
            pl.BlockSpec((None, D_MODEL, IN_GATE_TN), lambda i, j: (0, 0, j)),
            pl.BlockSpec((None, 1, IN_GATE_TN), lambda i, j: (layer, 0, j)),
            pl.BlockSpec((None, 1, HEAD_DIM), lambda i, j: (layer, 0, 0)),
            pl.BlockSpec((None, 1, HEAD_DIM), lambda i, j: (layer, 0, 0)),
        ],
        out_specs=(head_spec,) * 6 + (pl.BlockSpec((tm, IN_GATE_TN), lambda i, j: (i, j)),)
                  + (sample_spec(IN_SEG_TN),) * 4 + (sample_spec(IN_GATE_TN),),
        out_shape=out_shape,
        scratch_shapes=[pltpu.VMEM((tm, D_MODEL), BF16), pltpu.VMEM((ms, D_MODEL), BF16)],
        compiler_params=_params(("arbitrary", "arbitrary")),
        name="in_proj",
    )(x, xs, g1, w_mixed, w_gate, b_gate, qg, kg)


def _attn_prompt_kernel(q_ref, k_new_ref, v_new_ref, bias_ref, *rest):
    i = pl.program_id(1)
    k_ref, v_ref = rest[-2:]
    n_casts = (len(rest) - 3) // 2
    o_ref = rest[n_casts]
    for src, dst in zip(rest[:n_casts], rest[n_casts + 1:-2]):
        dst[...] = src[...].astype(BF16)

    def heads(n_valid):
        first = KEY_BLOCKS - n_valid
        for win, new in ((k_ref, k_new_ref), (v_ref, v_new_ref)):
            for slot in range(first, KEY_BLOCKS - 1):
                win[slot * Q_BLOCK:(slot + 1) * Q_BLOCK, :] = win[(slot + 1) * Q_BLOCK:(slot + 2) * Q_BLOCK, :]
            win[(KEY_BLOCKS - 1) * Q_BLOCK:, :] = new[...]
        for h in range(N_HEADS):
            sl = slice(h * HEAD_DIM, (h + 1) * HEAD_DIM)
            qh = q_ref[:, sl]
            m = l = o = None
            for d in range(first, KEY_BLOCKS):
                rows = slice(d * Q_BLOCK, (d + 1) * Q_BLOCK)
                s = _dot_t(qh, k_ref[rows, sl]) + bias_ref[h, :, rows]
                m_r = jnp.max(s, axis=-1, keepdims=True)
                if m is None:
                    m = m_r
                    e = jnp.exp2(s - m)
                    l = jnp.sum(e, axis=-1, keepdims=True)
                    o = _dot(e.astype(BF16), v_ref[rows, sl])
                else:
                    m_new = jnp.maximum(m, m_r)
                    alpha = jnp.exp2(m - m_new)
                    e = jnp.exp2(s - m_new)
                    l = alpha * l + jnp.sum(e, axis=-1, keepdims=True)
                    o = alpha * o + _dot(e.astype(BF16), v_ref[rows, sl])
                    m = m_new
            o_ref[:, sl] = (o / l).astype(BF16)

    for n_valid in range(1, KEY_BLOCKS):
        pl.when(i == n_valid - 1)(functools.partial(heads, n_valid))
    pl.when(i >= KEY_BLOCKS - 1)(functools.partial(heads, KEY_BLOCKS))


def _attn_prompt(q, k, v, bias, *, layer, n, s, row_casts=(), w_in=None):
    nb = s // Q_BLOCK
    steps = n * nb
    q_spec = pl.BlockSpec((Q_BLOCK, ATTN_WIDTH), lambda b, i: (b * nb + i, 0))
    operands = [q, k, v, bias]
    extra_in, extra_out, extra_shape = [], [], []
    for w in row_casts:
        _, rows, cols = w.shape
        block = (None, rows // steps, cols)
        assert rows % (steps * BF16_SUBLANES) == 0
        extra_in.append(pl.BlockSpec(block, lambda b, i: (layer, b * nb + i, 0)))
        extra_out.append(pl.BlockSpec(block, lambda b, i: (0, b * nb + i, 0)))
        extra_shape.append(jax.ShapeDtypeStruct((1, rows, cols), BF16))
        operands.append(w)
    if w_in is not None:
        assert steps == IN_SLABS
        slab_in, slab_out = _regroup_slab_specs(layer + 1, lambda b, i: b * nb + i)
        extra_in += slab_in
        extra_out += slab_out
        extra_shape += [jax.ShapeDtypeStruct((1, D_MODEL, 4 * ATTN_WIDTH), BF16),
                        jax.ShapeDtypeStruct((1, D_MODEL, 2 * D_MODEL), BF16)]
        operands += [w_in, w_in]

    bias_spec = pl.BlockSpec((N_HEADS, Q_BLOCK, KEY_BLOCKS * Q_BLOCK), lambda b, i: (layer, 0, 0),
                             pipeline_mode=pl.Buffered(1))
    window = pltpu.VMEM((KEY_BLOCKS * Q_BLOCK, ATTN_WIDTH), BF16)
    outs = pl.pallas_call(
        _attn_prompt_kernel,
        grid=(n, nb),
        in_specs=[q_spec, q_spec, q_spec, bias_spec] + extra_in,
        out_specs=tuple([q_spec] + extra_out),
        out_shape=tuple([jax.ShapeDtypeStruct((n * s, ATTN_WIDTH), BF16)] + extra_shape),
        scratch_shapes=[window, window],
        compiler_params=_params(("arbitrary", "arbitrary")),
        name="attn_prompt",
    )(*operands)
    return outs


def _shift_cache_block(src_ref, dst_ref, *, new_rows):
    keep = src_ref.shape[0] - new_rows
    dst_ref[0:keep, :] = src_ref[new_rows:, :]
    dst_ref[keep:, :] = jnp.zeros((new_rows, HEAD_DIM), F32)


def _attn_sample_kernel(q_ref, kf_ref, vf_ref, ck_ref, cv_ref, bias_ref, k_alias, v_alias,
                        o_ref, ko_ref, vo_ref, kk_scr, vv_scr, *, t, past):
    del k_alias, v_alias
    for new_ref, cache_ref, out_ref, scr in ((kf_ref, ck_ref, ko_ref, kk_scr),
                                             (vf_ref, cv_ref, vo_ref, vv_scr)):
        for h in range(N_HEADS):
            sl = slice(h * HEAD_DIM, (h + 1) * HEAD_DIM)
            out_ref[pl.ds(h, t, stride=N_HEADS), :] = new_ref[:, sl]
            scr[0:past, sl] = cache_ref[pl.ds(h, past, stride=N_HEADS), :].astype(BF16)
        scr[past:past + t, :] = new_ref[...].astype(BF16)
        scr[past + t:, :] = jnp.zeros((SAMPLE_KEYS_PAD - past - t, ATTN_WIDTH), BF16)
    for h in range(N_HEADS):
        sl = slice(h * HEAD_DIM, (h + 1) * HEAD_DIM)
        s = _dot_t(q_ref[:, sl], kk_scr[:, sl]) + bias_ref[h]
        m = jnp.max(s, axis=-1, keepdims=True)
        e = jnp.exp2(s - m)
        l = jnp.sum(e, axis=-1, keepdims=True)
        o = _dot(e.astype(BF16), vv_scr[:, sl])
        o_ref[:, sl] = (o / l).astype(BF16)


def _attn_sample(q, kf, vf, cache_k, cache_v, bias, k_upd, v_upd, *, layer, n, t):
    depth, _, past = cache_k.shape[:3]
    rows = past * N_HEADS
    new_rows = t * N_HEADS
    row_spec = pl.BlockSpec((t, ATTN_WIDTH), lambda b: (b, 0))
    cache_spec = pl.BlockSpec((None, None, rows, HEAD_DIM), lambda b: (layer, b, 0, 0))
    new_spec = pl.BlockSpec((None, None, new_rows, HEAD_DIM), lambda b: (layer, b, rows // new_rows - 1, 0))
    bias_spec = pl.BlockSpec((N_HEADS, t, SAMPLE_KEYS_PAD), lambda b: (layer, 0, 0),
                             pipeline_mode=pl.Buffered(1))
    any_spec = pl.BlockSpec(memory_space=pl.ANY)
    upd = jax.ShapeDtypeStruct(k_upd.shape, F32)
    return pl.pallas_call(
        functools.partial(_attn_sample_kernel, t=t, past=past),
        grid=(n,),
        in_specs=[row_spec, row_spec, row_spec, cache_spec, cache_spec, bias_spec, any_spec, any_spec],
        out_specs=(row_spec, new_spec, new_spec),
        out_shape=(jax.ShapeDtypeStruct((n * t, ATTN_WIDTH), BF16), upd, upd),
        input_output_aliases={6: 1, 7: 2},
        scratch_shapes=[pltpu.VMEM((SAMPLE_KEYS_PAD, ATTN_WIDTH), BF16),
                        pltpu.VMEM((SAMPLE_KEYS_PAD, ATTN_WIDTH), BF16)],
        compiler_params=_params(("arbitrary",)),
        name="attn_sample",
    )(q, kf, vf, cache_k.reshape(depth, n, rows, HEAD_DIM), cache_v.reshape(depth, n, rows, HEAD_DIM),
      bias, k_upd, v_upd)


def _merge_kernel(*refs, nseq, t, blocks_per_seq, pos0, cache_blocks, new_rows):
    x_ref, p_ref, hist_ref, a_ref, gate_ref, pw_ref, ps_ref, wa_ref, wb_ref, wo_ref = refs[:10]
    i = pl.program_id(0)
    if cache_blocks:
        ck_ref, cv_ref, o_ref, ko_ref, vo_ref, xp_scr, stage_scr, d_scr = refs[10:]
        pl.when(i < cache_blocks)(
            functools.partial(_shift_cache_block, ck_ref, ko_ref, new_rows=new_rows))
        pl.when((i >= cache_blocks) & (i < 2 * cache_blocks))(
            functools.partial(_shift_cache_block, cv_ref, vo_ref, new_rows=new_rows))
    else:
        o_ref, xp_scr, stage_scr, d_scr = refs[10:]
    row = lax.broadcasted_iota(jnp.int32, (t, 1), 0)
    if blocks_per_seq is None:
        pos = pos0 + row
    else:
        pos = (i % blocks_per_seq) * t + row
    gated_b = gate_ref[:, D_MODEL:] * _dot(a_ref[...], wb_ref[...])

    t0 = POOL_PAD + HIST_ROWS
    xp_scr[0:POOL_PAD, :] = jnp.zeros((POOL_PAD, POOL_WIDTH), F32)
    stage_scr[:, 0:POOL_PAD, :] = jnp.zeros((2, POOL_PAD, POOL_GROUP), F32)
    for s in range(nseq):
        rows = slice(s * t, (s + 1) * t)
        hist = hist_ref[s * HIST_ROWS:(s + 1) * HIST_ROWS, :]
        if blocks_per_seq is not None:
            hist = jnp.where(i % blocks_per_seq == 0, 0.0, hist)
        xp_scr[POOL_PAD:t0, :] = hist
        xp_scr[t0:, :] = p_ref[rows, :]
        for gi, w in enumerate(POOL_WINDOWS):
            cols = slice(gi * POOL_GROUP, (gi + 1) * POOL_GROUP)
            src = xp_scr.at[:, cols]
            shift = 1
            while 2 * shift < w:
                dst = stage_scr.at[(shift.bit_length() - 1) % 2]
                dst[POOL_PAD:, :] = src[POOL_PAD:, :] + src[POOL_PAD - shift:t0 + t - shift, :]
                src = dst
                shift *= 2
            acc = src[t0:, :] + src[t0 - shift:t0 + t - shift, :]
            inv_cnt = 1.0 / jnp.minimum(pos + 1, w).astype(F32)
            d_scr[rows, cols] = (acc * inv_cnt - p_ref[rows, cols]).astype(BF16)
    pooled = jnp.concatenate(
        [_dot(d_scr[:, gi * POOL_GROUP:(gi + 1) * POOL_GROUP], pw_ref[gi]) for gi in range(len(POOL_WINDOWS))],
        axis=-1) * ps_ref[...]
    branch_a = _dot(pooled.astype(BF16), wa_ref[...])
    merged = gate_ref[:, :D_MODEL] * branch_a + gated_b
    o_ref[...] = x_ref[...] + _dot(merged.astype(BF16), wo_ref[...])


def _merge(x, p, hist, attn, gates, pool_w, pool_scale, wa, wb, wo, *, layer, tm, nseq, t, blocks_per_seq, pos0,
           hist_spec, caches=None, new_frames=0):
    m = x.shape[0]
    steps = m // tm

    def row_spec(width):
        return pl.BlockSpec((tm, width), lambda i: (i, 0))

    operands = [x, p, hist, attn, gates, pool_w, pool_scale, wa, wb, wo]
    in_specs = [row_spec(D_MODEL), row_spec(POOL_WIDTH), hist_spec, row_spec(ATTN_WIDTH),
                row_spec(2 * D_MODEL),
                _layer_spec(pool_w.shape[1:], 0), _layer_spec(pool_scale.shape[1:], layer),
                _layer_spec(wa.shape[1:], 0), _layer_spec(wb.shape[1:], 0),
                _layer_spec(wo.shape[1:], 0)]
    out_specs = [row_spec(D_MODEL)]
    out_shape = [jax.ShapeDtypeStruct((m, D_MODEL), F32)]
    cache_blocks = 0
    if caches is not None:
        depth, n, past = caches[0].shape[:3]
        cache_blocks = depth * n
        assert steps >= 2 * cache_blocks
        rows = past * N_HEADS
        k_spec = pl.BlockSpec((None, rows, HEAD_DIM), lambda i: (jnp.minimum(i, cache_blocks - 1), 0, 0))
        v_spec = pl.BlockSpec((None, rows, HEAD_DIM),
                              lambda i: (jnp.clip(i - cache_blocks, 0, cache_blocks - 1), 0, 0))
        operands += [c.reshape(cache_blocks, rows, HEAD_DIM) for c in caches]
        in_specs += [k_spec, v_spec]
        out_specs += [k_spec, v_spec]
        out_shape += [jax.ShapeDtypeStruct((cache_blocks, rows, HEAD_DIM), F32)] * 2
    outs = pl.pallas_call(
        functools.partial(_merge_kernel, nseq=nseq, t=t, blocks_per_seq=blocks_per_seq, pos0=pos0,
                          cache_blocks=cache_blocks, new_rows=new_frames * N_HEADS),
        grid=(steps,),
        in_specs=in_specs,
        out_specs=tuple(out_specs),
        out_shape=tuple(out_shape),
        scratch_shapes=[pltpu.VMEM((POOL_PAD + HIST_ROWS + t, POOL_WIDTH), F32),
                        pltpu.VMEM((2, POOL_PAD + HIST_ROWS + t, POOL_GROUP), F32),
                        pltpu.VMEM((tm, POOL_WIDTH), BF16)],
        compiler_params=_params(("arbitrary",)),
        name="merge",
    )(*operands)
    if caches is None:
        return outs[0]
    return outs[0], outs[1].reshape(depth, n, rows, HEAD_DIM), outs[2].reshape(depth, n, rows, HEAD_DIM)


FFN_TF = 512


def _ffn_kernel(x_ref, xs_ref, g2_ref, wu_ref, wd_ref, o_ref, os_ref, h_scr, hs_scr):
    i = pl.program_id(0)
    j = pl.program_id(1)

    def delta(scr):
        u = jnp.square(jnp.maximum(_dot(scr[...], wu_ref[...].astype(BF16)), 0.0))
        return _dot(u.astype(BF16), wd_ref[...].astype(BF16))

    def first_step(src_ref, acc_ref, scr):
        x = src_ref[...]
        scr[...] = _rms(x, g2_ref[...]).astype(BF16)
        acc_ref[...] = x + delta(scr)

    def later_step(acc_ref, scr):
        acc_ref[...] += delta(scr)

    pl.when(j == 0)(functools.partial(first_step, x_ref, o_ref, h_scr))
    pl.when(j > 0)(functools.partial(later_step, o_ref, h_scr))
    pl.when((i == 0) & (j == 0))(functools.partial(first_step, xs_ref, os_ref, hs_scr))
    pl.when((i == 0) & (j > 0))(functools.partial(later_step, os_ref, hs_scr))


def _ffn(x, xs, g2, wu, wd, *, layer, tm):
    m = x.shape[0]
    ms = xs.shape[0]
    sample_spec = pl.BlockSpec((ms, D_MODEL), lambda i, j: (0, 0))
    return pl.pallas_call(
        _ffn_kernel,
        grid=(m // tm, D_FF // FFN_TF),
        in_specs=[pl.BlockSpec((tm, D_MODEL), lambda i, j: (i, 0)),
                  sample_spec,
                  pl.BlockSpec((None, 1, D_MODEL), lambda i, j: (layer, 0, 0)),
                  pl.BlockSpec((None, D_MODEL, FFN_TF), lambda i, j: (layer, 0, j)),
                  pl.BlockSpec((None, FFN_TF, D_MODEL), lambda i, j: (layer, j, 0))],
        out_specs=(pl.BlockSpec((tm, D_MODEL), lambda i, j: (i, 0)), sample_spec),
        out_shape=(jax.ShapeDtypeStruct((m, D_MODEL), F32), jax.ShapeDtypeStruct((ms, D_MODEL), F32)),
        scratch_shapes=[pltpu.VMEM((tm, D_MODEL), BF16), pltpu.VMEM((ms, D_MODEL), BF16)],
        compiler_params=_params(("arbitrary", "arbitrary")),
        name="ffn",
    )(x, xs, g2, wu, wd)


PROMPT_TM = 1024
MERGE_TM = 256


def kernel(x_prompt, x_sample, state_pool, cache_k, cache_v, norm1_g, w_in, b_gate, pool_w, pool_scale,
           q_norm_g, k_norm_g, rel_bias, w_branch_a, w_branch_b, w_out, norm2_g, w_up, w_down):
    nb, seq, _ = x_prompt.shape
    ns, t, _ = x_sample.shape
    depth = w_in.shape[0]
    past = cache_k.shape[2]
    assert seq % PROMPT_TM == 0 and seq % MERGE_TM == 0 and seq % Q_BLOCK == 0 and seq >= BAND_PAST
    assert t >= POOL_HIST and t % HIST_ROWS == 0 and past + t <= SAMPLE_KEYS_PAD and past <= BAND_PAST
    assert past % t == 0 and past > t

    xp = x_prompt.reshape(nb * seq, D_MODEL)
    xs = x_sample.reshape(ns * t, D_MODEL)
    blocks_per_seq = seq // MERGE_TM
    prompt_hist_spec = pl.BlockSpec(
        (HIST_ROWS, POOL_WIDTH), lambda i: (jnp.maximum(i * (MERGE_TM // HIST_ROWS) - 1, 0), 0))
    sample_hist_spec = pl.BlockSpec((ns * HIST_ROWS, POOL_WIDTH), lambda i: (0, 0))

    g1 = norm1_g.reshape(depth, 1, D_MODEL)
    g2 = norm2_g.reshape(depth, 1, D_MODEL)
    bg = b_gate.reshape(depth, 1, 2 * D_MODEL)
    qg = q_norm_g.reshape(depth, 1, HEAD_DIM)
    kg = k_norm_g.reshape(depth, 1, HEAD_DIM)
    ps = pool_scale.reshape(depth, 1, POOL_WIDTH)
    w_mixed, w_gate = _regroup_in_weights(w_in, 0)
    merge_weights = (pool_w.reshape(depth, POOL_WIDTH, POOL_GROUP), w_branch_a, w_branch_b, w_out)
    hist_s = jnp.pad(state_pool, ((0, 0), (0, 0), (HIST_ROWS - POOL_HIST, 0), (0, 0)))
    hist_s = hist_s.reshape(depth, ns * HIST_ROWS, POOL_WIDTH)

    bias_p = _bias_table(rel_bias, q_rows=Q_BLOCK, keys=KEY_BLOCKS * Q_BLOCK,
                         offset=(KEY_BLOCKS - 1) * Q_BLOCK, visible_fn=_prompt_visible)
    bias_s = _bias_table(rel_bias, q_rows=t, keys=SAMPLE_KEYS_PAD, offset=past,
                         visible_fn=lambda qi, kj: kj < past + t)

    k_upd = v_upd = None
    pool_p, kp_l, vp_l, pool_s = [], [], [], []
    for l in range(depth):
        (p, q, k, v, kf, vf, gates,
         p_s, q_s, kf_s, vf_s, gates_s) = _in_proj(xp, xs, g1, w_mixed, w_gate, bg, qg, kg, layer=l, tm=PROMPT_TM)

        attn, pw, wa, wb, wo, *next_w_in = _attn_prompt(
            q, k, v, bias_p, layer=l, n=nb, s=seq, row_casts=merge_weights,
            w_in=w_in if l + 1 < depth else None)
        pw = pw.reshape(1, len(POOL_WINDOWS), POOL_GROUP, POOL_GROUP)
        if next_w_in:
            w_mixed, w_gate = next_w_in
        pool_p.append(p.reshape(nb, seq, POOL_WIDTH)[:, seq - POOL_HIST:])
        kp_l.append(kf.reshape(nb, seq, ATTN_WIDTH)[:, seq - BAND_PAST:].reshape(nb, BAND_PAST, N_HEADS, HEAD_DIM))
        vp_l.append(vf.reshape(nb, seq, ATTN_WIDTH)[:, seq - BAND_PAST:].reshape(nb, BAND_PAST, N_HEADS, HEAD_DIM))
        merge_p = functools.partial(_merge, xp, p, p, attn, gates, pw, ps, wa, wb, wo, layer=l, tm=MERGE_TM,
                                    nseq=1, t=MERGE_TM, blocks_per_seq=blocks_per_seq, pos0=None,
                                    hist_spec=prompt_hist_spec)
        if l == 0:
            xp, k_upd, v_upd = merge_p(caches=(cache_k, cache_v), new_frames=t)
        else:
            xp = merge_p()

        attn_s, k_upd, v_upd = _attn_sample(q_s, kf_s, vf_s, cache_k, cache_v, bias_s, k_upd, v_upd,
                                            layer=l, n=ns, t=t)
        pool_s.append(p_s.reshape(ns, t, POOL_WIDTH)[:, t - POOL_HIST:])
        xs = _merge(xs, p_s, hist_s[l], attn_s, gates_s, pw, ps, wa, wb, wo, layer=l, tm=ns * t, nseq=ns, t=t,
                    blocks_per_seq=None, pos0=PAST_LEN, hist_spec=sample_hist_spec)

        xp, xs = _ffn(xp, xs, g2, w_up, w_down, layer=l, tm=PROMPT_TM)

    return (xp.reshape(nb, seq, D_MODEL), xs.reshape(ns, t, D_MODEL),
            jnp.stack(pool_p), jnp.stack(kp_l), jnp.stack(vp_l),
            jnp.stack(pool_s), k_upd.reshape(cache_k.shape), v_upd.reshape(cache_v.shape))
```

```python
import functools

import numpy as np

import jax
import jax.numpy as jnp
from jax import lax
from jax.experimental import pallas as pl
from jax.experimental.pallas import tpu as pltpu

D_MODEL = 2048
PAST_LEN = 1024
CHUNK = 64
N_PREV_CHUNKS = 8
BAND_PAST = N_PREV_CHUNKS * CHUNK
POOL_WIDTH = D_MODEL // 2
POOL_WINDOWS = (2, 4, 8, 16)
POOL_GROUP = POOL_WIDTH // len(POOL_WINDOWS)
POOL_HIST = max(POOL_WINDOWS) - 1
HEAD_DIM = 128
ATTN_WIDTH = D_MODEL // 2
N_HEADS = ATTN_WIDTH // HEAD_DIM
MAX_REL = 128
D_FF = 4 * D_MODEL
EPS = 1e-6
ATTN_SCALE = HEAD_DIM ** -0.5
LOG2E = 1.4426950408889634
Q_SCALE = ATTN_SCALE * LOG2E
NEG_INF = -1e30

BF16 = jnp.bfloat16
F32 = jnp.float32

V7X_VMEM_BYTES = 64 * 1024 * 1024
LANES = 128
BF16_SUBLANES = 16
VMEM_LIMIT_BYTES = V7X_VMEM_BYTES - 1024 * 1024

HIST_ROWS = 16
Q_BLOCK = 4 * CHUNK
KEY_BLOCKS = 3
SAMPLE_KEYS_PAD = 640


def _rms(xf, g):
    ms = jnp.mean(xf * xf, axis=-1, keepdims=True)
    return xf * lax.rsqrt(ms + EPS) * g


def _dot(a, b):
    return jnp.dot(a, b, preferred_element_type=F32)


def _dot_t(a, b):
    return lax.dot_general(a, b, (((1,), (1,)), ((), ())), preferred_element_type=F32)


def _params(semantics):
    return pltpu.CompilerParams(dimension_semantics=semantics, vmem_limit_bytes=VMEM_LIMIT_BYTES)


def _layer_spec(shape, layer):
    nd = len(shape)
    return pl.BlockSpec((None,) + tuple(shape), lambda *_: (layer,) + (0,) * nd,
                        pipeline_mode=pl.Buffered(1))


def _bias_table_kernel(ext_ref, o_ref, *, q_rows, keys, visible_fn):
    width = ext_ref.shape[-1]
    qi = lax.broadcasted_iota(jnp.int32, (q_rows, keys), 0)
    kj = lax.broadcasted_iota(jnp.int32, (q_rows, keys), 1)
    visible = visible_fn(qi, kj)
    for g in range(ext_ref.shape[0]):
        row = jnp.broadcast_to(ext_ref[g:g + 1, :], (q_rows, width))
        skew = pltpu.roll(row, 0, 1, stride=1, stride_axis=0)
        o_ref[g] = jnp.where(visible, skew[:, :keys] * LOG2E, NEG_INF)


def _take_static(table, idx):
    pieces = []
    start = 0
    while start < len(idx):
        end = start + 1
        step = int(idx[end] - idx[start]) if end < len(idx) else 0
        if step in (-1, 0, 1):
            while end < len(idx) and int(idx[end] - idx[end - 1]) == step:
                end += 1
        lo, hi = sorted((int(idx[start]), int(idx[end - 1])))
        if step == 0 or end - start == 1:
            pieces.append(jnp.broadcast_to(table[:, lo:lo + 1], (table.shape[0], end - start)))
        else:
            ramp = table[:, lo:hi + 1]
            pieces.append(ramp if step == 1 else ramp[:, ::-1])
        start = end
    return jnp.concatenate(pieces, axis=1)


def _bias_table(rel_bias, *, q_rows, keys, offset, visible_fn):
    groups = rel_bias.shape[0] * rel_bias.shape[1]
    width = -(-(keys + q_rows - 1) // LANES) * LANES
    idx = np.zeros((width,), np.int32)
    idx[:keys] = np.clip(offset - np.arange(keys), -MAX_REL, MAX_REL) + MAX_REL
    back = np.arange(1, q_rows)
    idx[width - back] = np.clip(offset + back, -MAX_REL, MAX_REL) + MAX_REL
    ext = _take_static(rel_bias.reshape(groups, 2 * MAX_REL + 1).astype(F32), idx)
    return pl.pallas_call(
        functools.partial(_bias_table_kernel, q_rows=q_rows, keys=keys, visible_fn=visible_fn),
        out_shape=jax.ShapeDtypeStruct((groups, q_rows, keys), F32),
        compiler_params=pltpu.CompilerParams(vmem_limit_bytes=VMEM_LIMIT_BYTES),
        name="bias_table",
    )(ext)


def _prompt_visible(qi, kj):
    own = qi // CHUNK + (KEY_BLOCKS - 1) * (Q_BLOCK // CHUNK)
    kc = kj // CHUNK
    return (kc <= own) & (kc >= own - N_PREV_CHUNKS)


IN_HEADS = 2
IN_STEPS = N_HEADS // IN_HEADS
IN_SEG_TN = IN_HEADS * HEAD_DIM
IN_GATE_TN = 2 * D_MODEL // IN_STEPS


IN_SEG_ORDER = (1, 2, 0, 3)
IN_SLABS = 4 * ATTN_WIDTH // HEAD_DIM


def _regroup_kernel(wq_ref, wk_ref, wp_ref, wv_ref, wg_ref, mixed_ref, gate_ref):
    for s, src in enumerate((wq_ref, wk_ref, wp_ref, wv_ref)):
        mixed_ref[:, s * IN_SEG_TN:(s + 1) * IN_SEG_TN] = src[...].astype(BF16)
    gate_ref[...] = wg_ref[...].astype(BF16)


def _regroup_in_weights(w_in, layer):
    def head_cols(seg):
        return pl.BlockSpec((None, D_MODEL, IN_SEG_TN), lambda j: (layer, 0, seg * IN_STEPS + j))

    gate0 = 4 * ATTN_WIDTH // IN_GATE_TN
    return pl.pallas_call(
        _regroup_kernel,
        grid=(IN_STEPS,),
        in_specs=[head_cols(seg) for seg in IN_SEG_ORDER]
                 + [pl.BlockSpec((None, D_MODEL, IN_GATE_TN), lambda j: (layer, 0, gate0 + j))],
        out_specs=(pl.BlockSpec((None, D_MODEL, 4 * IN_SEG_TN), lambda j: (0, 0, j)),
                   pl.BlockSpec((None, D_MODEL, IN_GATE_TN), lambda j: (0, 0, j))),
        out_shape=(jax.ShapeDtypeStruct((1, D_MODEL, 4 * ATTN_WIDTH), BF16),
                   jax.ShapeDtypeStruct((1, D_MODEL, 2 * D_MODEL), BF16)),
        compiler_params=_params(("parallel",)),
        name="regroup_w_in",
    )(w_in, w_in, w_in, w_in, w_in)


def _regroup_slab_specs(layer, step_fn):
    def mixed_src(*ids):
        s = step_fn(*ids)
        group, within = s // (4 * IN_HEADS), s % (4 * IN_HEADS)
        seg, head = within // IN_HEADS, group * IN_HEADS + within % IN_HEADS
        src_seg = sum(jnp.where(seg == pos, src, 0) for pos, src in enumerate(IN_SEG_ORDER))
        return (layer, 0, src_seg * N_HEADS + head)

    slab = (None, D_MODEL, HEAD_DIM)
    in_specs = [pl.BlockSpec(slab, mixed_src),
                pl.BlockSpec(slab, lambda *ids: (layer, 0, IN_SLABS + step_fn(*ids)))]
    out_specs = [pl.BlockSpec(slab, lambda *ids: (0, 0, step_fn(*ids)))] * 2
    return in_specs, out_specs


def _in_rows(h, wm_ref, wg_ref, bg_ref, qg_ref, kg_ref, p_ref, q_ref, kf_ref, vf_ref, gate_ref,
             k_ref=None, v_ref=None):
    gate_ref[...] = jax.nn.sigmoid(_dot(h, wg_ref[...]) + bg_ref[...])
    z = _dot(h, wm_ref[...])
    for hd in range(IN_HEADS):
        sl = slice(hd * HEAD_DIM, (hd + 1) * HEAD_DIM)
        q_ref[:, sl] = (_rms(z[:, sl], qg_ref[...]) * Q_SCALE).astype(BF16)
        kn = _rms(z[:, IN_SEG_TN + hd * HEAD_DIM:IN_SEG_TN + (hd + 1) * HEAD_DIM], kg_ref[...])
        kf_ref[:, sl] = kn
        if k_ref is not None:
            k_ref[:, sl] = kn.astype(BF16)
    p_ref[...] = z[:, 2 * IN_SEG_TN:3 * IN_SEG_TN]
    v = z[:, 3 * IN_SEG_TN:4 * IN_SEG_TN]
    vf_ref[...] = v
    if v_ref is not None:
        v_ref[...] = v.astype(BF16)


def _in_kernel(x_ref, xs_ref, g1_ref, wm_ref, wg_ref, bg_ref, qg_ref, kg_ref,
               p_ref, q_ref, k_ref, v_ref, kf_ref, vf_ref, gate_ref,
               ps_ref, qs_ref, kfs_ref, vfs_ref, gates_ref, h_scr, hs_scr):
    i = pl.program_id(0)
    j = pl.program_id(1)
    weights = (wm_ref, wg_ref, bg_ref, qg_ref, kg_ref)

    def first_step(src_ref, scr, *outs):
        scr[...] = _rms(src_ref[...], g1_ref[...]).astype(BF16)
        _in_rows(scr[...], *weights, *outs)

    def later_step(scr, *outs):
        _in_rows(scr[...], *weights, *outs)

    prompt_outs = (p_ref, q_ref, kf_ref, vf_ref, gate_ref, k_ref, v_ref)
    sample_outs = (ps_ref, qs_ref, kfs_ref, vfs_ref, gates_ref)
    pl.when(j == 0)(functools.partial(first_step, x_ref, h_scr, *prompt_outs))
    pl.when(j > 0)(functools.partial(later_step, h_scr, *prompt_outs))
    pl.when((i == 0) & (j == 0))(functools.partial(first_step, xs_ref, hs_scr, *sample_outs))
    pl.when((i == 0) & (j > 0))(functools.partial(later_step, hs_scr, *sample_outs))


def _in_proj(x, xs, g1, w_mixed, w_gate, b_gate, qg, kg, *, layer, tm):
    m = x.shape[0]
    ms = xs.shape[0]
    head_spec = pl.BlockSpec((tm, IN_SEG_TN), lambda i, j: (i, j))

    def sample_spec(width):
        return pl.BlockSpec((ms, width), lambda i, j: (0, jnp.where(i == 0, j, IN_STEPS - 1)))

    out_shape = (
        jax.ShapeDtypeStruct((m, POOL_WIDTH), F32),
        jax.ShapeDtypeStruct((m, ATTN_WIDTH), BF16),
        jax.ShapeDtypeStruct((m, ATTN_WIDTH), BF16),
        jax.ShapeDtypeStruct((m, ATTN_WIDTH), BF16),
        jax.ShapeDtypeStruct((m, ATTN_WIDTH), F32),
        jax.ShapeDtypeStruct((m, ATTN_WIDTH), F32),
        jax.ShapeDtypeStruct((m, 2 * D_MODEL), F32),
        jax.ShapeDtypeStruct((ms, POOL_WIDTH), F32),
        jax.ShapeDtypeStruct((ms, ATTN_WIDTH), BF16),
        jax.ShapeDtypeStruct((ms, ATTN_WIDTH), F32),
        jax.ShapeDtypeStruct((ms, ATTN_WIDTH), F32),
        jax.ShapeDtypeStruct((ms, 2 * D_MODEL), F32),
    )
    return pl.pallas_call(
        _in_kernel,
        grid=(m // tm, IN_STEPS),
        in_specs=[
            pl.BlockSpec((tm, D_MODEL), lambda i, j: (i, 0)),
            pl.BlockSpec((ms, D_MODEL), lambda i, j: (0, 0)),
            pl.BlockSpec((None, 1, D_MODEL), lambda i, j: (layer, 0, 0)),
            pl.BlockSpec((None, D_MODEL, 4 * IN_SEG_TN), lambda i, j: (0, 0, j)),
            pl.BlockSpec((None, D_MODEL, IN_GATE_TN), lambda i, j: (0, 0, j)),
            pl.BlockSpec((None, 1, IN_GATE_TN), lambda i, j: (layer, 0, j)),
            pl.BlockSpec((None, 1, HEAD_DIM), lambda i, j: (layer, 0, 0)),
            pl.BlockSpec((None, 1, HEAD_DIM), lambda i, j: (layer, 0, 0)),
        ],
        out_specs=(head_spec,) * 6 + (pl.BlockSpec((tm, IN_GATE_TN), lambda i, j: (i, j)),)
                  + (sample_spec(IN_SEG_TN),) * 4 + (sample_spec(IN_GATE_TN),),
        out_shape=out_shape,
        scratch_shapes=[pltpu.VMEM((tm, D_MODEL), BF16), pltpu.VMEM((ms, D_MODEL), BF16)],
        compiler_params=_params(("arbitrary", "arbitrary")),
        name="in_proj",
    )(x, xs, g1, w_mixed, w_gate, b_gate, qg, kg)


def _attn_prompt_kernel(q_ref, k_new_ref, v_new_ref, bias_ref, *rest):
    i = pl.program_id(1)
    k_ref, v_ref = rest[-2:]
    n_casts = (len(rest) - 3) // 2
    o_ref = rest[n_casts]
    for src, dst in zip(rest[:n_casts], rest[n_casts + 1:-2]):
        dst[...] = src[...].astype(BF16)

    def heads(n_valid):
        first = KEY_BLOCKS - n_valid
        for win, new in ((k_ref, k_new_ref), (v_ref, v_new_ref)):
            for slot in range(first, KEY_BLOCKS - 1):
                win[slot * Q_BLOCK:(slot + 1) * Q_BLOCK, :] = win[(slot + 1) * Q_BLOCK:(slot + 2) * Q_BLOCK, :]
            win[(KEY_BLOCKS - 1) * Q_BLOCK:, :] = new[...]
        for h in range(N_HEADS):
            sl = slice(h * HEAD_DIM, (h + 1) * HEAD_DIM)
            qh = q_ref[:, sl]
            m = l = o = None
            for d in range(first, KEY_BLOCKS):
                rows = slice(d * Q_BLOCK, (d + 1) * Q_BLOCK)
                s = _dot_t(qh, k_ref[rows, sl]) + bias_ref[h, :, rows]
                m_r = jnp.max(s, axis=-1, keepdims=True)
                if m is None:
                    m = m_r
                    e = jnp.exp2(s - m)
                    l = jnp.sum(e, axis=-1, keepdims=True)
                    o = _dot(e.astype(BF16), v_ref[rows, sl])
                else:
                    m_new = jnp.maximum(m, m_r)
                    alpha = jnp.exp2(m - m_new)
                    e = jnp.exp2(s - m_new)
                    l = alpha * l + jnp.sum(e, axis=-1, keepdims=True)
                    o = alpha * o + _dot(e.astype(BF16), v_ref[rows, sl])
                    m = m_new
            o_ref[:, sl] = (o / l).astype(BF16)

    for n_valid in range(1, KEY_BLOCKS):
        pl.when(i == n_valid - 1)(functools.partial(heads, n_valid))
    pl.when(i >= KEY_BLOCKS - 1)(functools.partial(heads, KEY_BLOCKS))


def _attn_prompt(q, k, v, bias, *, layer, n, s, row_casts=(), w_in=None):
    nb = s // Q_BLOCK
    steps = n * nb
    q_spec = pl.BlockSpec((Q_BLOCK, ATTN_WIDTH), lambda b, i: (b * nb + i, 0))
    operands = [q, k, v, bias]
    extra_in, extra_out, extra_shape = [], [], []
    for w in row_casts:
        _, rows, cols = w.shape
        block = (None, rows // steps, cols)
        assert rows % (steps * BF16_SUBLANES) == 0
        extra_in.append(pl.BlockSpec(block, lambda b, i: (layer, b * nb + i, 0)))
        extra_out.append(pl.BlockSpec(block, lambda b, i: (0, b * nb + i, 0)))
        extra_shape.append(jax.ShapeDtypeStruct((1, rows, cols), BF16))
        operands.append(w)
    if w_in is not None:
        assert steps == IN_SLABS
        slab_in, slab_out = _regroup_slab_specs(layer + 1, lambda b, i: b * nb + i)
        extra_in += slab_in
        extra_out += slab_out
        extra_shape += [jax.ShapeDtypeStruct((1, D_MODEL, 4 * ATTN_WIDTH), BF16),
                        jax.ShapeDtypeStruct((1, D_MODEL, 2 * D_MODEL), BF16)]
        operands += [w_in, w_in]

    bias_spec = pl.BlockSpec((N_HEADS, Q_BLOCK, KEY_BLOCKS * Q_BLOCK), lambda b, i: (layer, 0, 0),
                             pipeline_mode=pl.Buffered(1))
    window = pltpu.VMEM((KEY_BLOCKS * Q_BLOCK, ATTN_WIDTH), BF16)
    outs = pl.pallas_call(
        _attn_prompt_kernel,
        grid=(n, nb),
        in_specs=[q_spec, q_spec, q_spec, bias_spec] + extra_in,
        out_specs=tuple([q_spec] + extra_out),
        out_shape=tuple([jax.ShapeDtypeStruct((n * s, ATTN_WIDTH), BF16)] + extra_shape),
        scratch_shapes=[window, window],
        compiler_params=_params(("arbitrary", "arbitrary")),
        name="attn_prompt",
    )(*operands)
    return outs


def _shift_cache_block(src_ref, dst_ref, *, new_rows):
    keep = src_ref.shape[0] - new_rows
    dst_ref[0:keep, :] = src_ref[new_rows:, :]
    dst_ref[keep:, :] = jnp.zeros((new_rows, HEAD_DIM), F32)


def _attn_sample_kernel(q_ref, kf_ref, vf_ref, ck_ref, cv_ref, bias_ref, k_alias, v_alias,
                        o_ref, ko_ref, vo_ref, kk_scr, vv_scr, *, t, past):
    del k_alias, v_alias
    for new_ref, cache_ref, out_ref, scr in ((kf_ref, ck_ref, ko_ref, kk_scr),
                                             (vf_ref, cv_ref, vo_ref, vv_scr)):
        for h in range(N_HEADS):
            sl = slice(h * HEAD_DIM, (h + 1) * HEAD_DIM)
            out_ref[pl.ds(h, t, stride=N_HEADS), :] = new_ref[:, sl]
            scr[0:past, sl] = cache_ref[pl.ds(h, past, stride=N_HEADS), :].astype(BF16)
        scr[past:past + t, :] = new_ref[...].astype(BF16)
        scr[past + t:, :] = jnp.zeros((SAMPLE_KEYS_PAD - past - t, ATTN_WIDTH), BF16)
    for h in range(N_HEADS):
        sl = slice(h * HEAD_DIM, (h + 1) * HEAD_DIM)
        s = _dot_t(q_ref[:, sl], kk_scr[:, sl]) + bias_ref[h]
        m = jnp.max(s, axis=-1, keepdims=True)
        e = jnp.exp2(s - m)
        l = jnp.sum(e, axis=-1, keepdims=True)
        o = _dot(e.astype(BF16), vv_scr[:, sl])
        o_ref[:, sl] = (o / l).astype(BF16)


def _attn_sample(q, kf, vf, cache_k, cache_v, bias, k_upd, v_upd, *, layer, n, t):
    depth, _, past = cache_k.shape[:3]
    rows = past * N_HEADS
    new_rows = t * N_HEADS
    row_spec = pl.BlockSpec((t, ATTN_WIDTH), lambda b: (b, 0))
    cache_spec = pl.BlockSpec((None, None, rows, HEAD_DIM), lambda b: (layer, b, 0, 0))
    new_spec = pl.BlockSpec((None, None, new_rows, HEAD_DIM), lambda b: (layer, b, rows // new_rows - 1, 0))
    bias_spec = pl.BlockSpec((N_HEADS, t, SAMPLE_KEYS_PAD), lambda b: (layer, 0, 0),
                             pipeline_mode=pl.Buffered(1))
    any_spec = pl.BlockSpec(memory_space=pl.ANY)
    upd = jax.ShapeDtypeStruct(k_upd.shape, F32)
    return pl.pallas_call(
        functools.partial(_attn_sample_kernel, t=t, past=past),
        grid=(n,),
        in_specs=[row_spec, row_spec, row_spec, cache_spec, cache_spec, bias_spec, any_spec, any_spec],
        out_specs=(row_spec, new_spec, new_spec),
        out_shape=(jax.ShapeDtypeStruct((n * t, ATTN_WIDTH), BF16), upd, upd),
        input_output_aliases={6: 1, 7: 2},
        scratch_shapes=[pltpu.VMEM((SAMPLE_KEYS_PAD, ATTN_WIDTH), BF16),
                        pltpu.VMEM((SAMPLE_KEYS_PAD, ATTN_WIDTH), BF16)],
        compiler_params=_params(("arbitrary",)),
        name="attn_sample",
    )(q, kf, vf, cache_k.reshape(depth, n, rows, HEAD_DIM), cache_v.reshape(depth, n, rows, HEAD_DIM),
      bias, k_upd, v_upd)


def _merge_kernel(*refs, nseq, t, blocks_per_seq, pos0, cache_blocks, new_rows):
    x_ref, p_ref, hist_ref, a_ref, gate_ref, pw_ref, ps_ref, wa_ref, wb_ref, wo_ref = refs[:10]
    i = pl.program_id(0)
    if cache_blocks:
        ck_ref, cv_ref, o_ref, ko_ref, vo_ref, xp_scr, d_scr = refs[10:]
        pl.when(i < cache_blocks)(
            functools.partial(_shift_cache_block, ck_ref, ko_ref, new_rows=new_rows))
        pl.when((i >= cache_blocks) & (i < 2 * cache_blocks))(
            functools.partial(_shift_cache_block, cv_ref, vo_ref, new_rows=new_rows))
    else:
        o_ref, xp_scr, d_scr = refs[10:]
    row = lax.broadcasted_iota(jnp.int32, (t, 1), 0)
    if blocks_per_seq is None:
        pos = pos0 + row
    else:
        pos = (i % blocks_per_seq) * t + row
    gated_b = gate_ref[:, D_MODEL:] * _dot(a_ref[...], wb_ref[...])

    for s in range(nseq):
        rows = slice(s * t, (s + 1) * t)
        hist = hist_ref[s * HIST_ROWS:(s + 1) * HIST_ROWS, :]
        if blocks_per_seq is not None:
            hist = jnp.where(i % blocks_per_seq == 0, 0.0, hist)
        xp_scr[0:HIST_ROWS, :] = hist
        xp_scr[HIST_ROWS:, :] = p_ref[rows, :]
        for gi, w in enumerate(POOL_WINDOWS):
            cols = slice(gi * POOL_GROUP, (gi + 1) * POOL_GROUP)
            cur = xp_scr[:, cols]
            shift = 1
            while shift < w:
                cur = cur + pltpu.roll(cur, shift, 0)
                shift *= 2
            acc = cur[HIST_ROWS:, :]
            inv_cnt = 1.0 / jnp.minimum(pos + 1, w).astype(F32)
            d_scr[rows, cols] = (acc * inv_cnt - p_ref[rows, cols]).astype(BF16)
    pooled = jnp.concatenate(
        [_dot(d_scr[:, gi * POOL_GROUP:(gi + 1) * POOL_GROUP], pw_ref[gi]) for gi in range(len(POOL_WINDOWS))],
        axis=-1) * ps_ref[...]
    branch_a = _dot(pooled.astype(BF16), wa_ref[...])
    merged = gate_ref[:, :D_MODEL] * branch_a + gated_b
    o_ref[...] = x_ref[...] + _dot(merged.astype(BF16), wo_ref[...])


def _merge(x, p, hist, attn, gates, pool_w, pool_scale, wa, wb, wo, *, layer, tm, nseq, t, blocks_per_seq, pos0,
           hist_spec, caches=None, new_frames=0):
    m = x.shape[0]
    steps = m // tm

    def row_spec(width):
        return pl.BlockSpec((tm, width), lambda i: (i, 0))

    operands = [x, p, hist, attn, gates, pool_w, pool_scale, wa, wb, wo]
    in_specs = [row_spec(D_MODEL), row_spec(POOL_WIDTH), hist_spec, row_spec(ATTN_WIDTH),
                row_spec(2 * D_MODEL),
                _layer_spec(pool_w.shape[1:], 0), _layer_spec(pool_scale.shape[1:], layer),
                _layer_spec(wa.shape[1:], 0), _layer_spec(wb.shape[1:], 0),
                _layer_spec(wo.shape[1:], 0)]
    out_specs = [row_spec(D_MODEL)]
    out_shape = [jax.ShapeDtypeStruct((m, D_MODEL), F32)]
    cache_blocks = 0
    if caches is not None:
        depth, n, past = caches[0].shape[:3]
        cache_blocks = depth * n
        assert steps >= 2 * cache_blocks
        rows = past * N_HEADS
        k_spec = pl.BlockSpec((None, rows, HEAD_DIM), lambda i: (jnp.minimum(i, cache_blocks - 1), 0, 0))
        v_spec = pl.BlockSpec((None, rows, HEAD_DIM),
                              lambda i: (jnp.clip(i - cache_blocks, 0, cache_blocks - 1), 0, 0))
        operands += [c.reshape(cache_blocks, rows, HEAD_DIM) for c in caches]
        in_specs += [k_spec, v_spec]
        out_specs += [k_spec, v_spec]
        out_shape += [jax.ShapeDtypeStruct((cache_blocks, rows, HEAD_DIM), F32)] * 2
    outs = pl.pallas_call(
        functools.partial(_merge_kernel, nseq=nseq, t=t, blocks_per_seq=blocks_per_seq, pos0=pos0,
                          cache_blocks=cache_blocks, new_rows=new_frames * N_HEADS),
        grid=(steps,),
        in_specs=in_specs,
        out_specs=tuple(out_specs),
        out_shape=tuple(out_shape),
        scratch_shapes=[pltpu.VMEM((HIST_ROWS + t, POOL_WIDTH), F32),
                        pltpu.VMEM((tm, POOL_WIDTH), BF16)],
        compiler_params=_params(("arbitrary",)),
        name="merge",
    )(*operands)
    if caches is None:
        return outs[0]
    return outs[0], outs[1].reshape(depth, n, rows, HEAD_DIM), outs[2].reshape(depth, n, rows, HEAD_DIM)


FFN_TF = 512


def _ffn_kernel(x_ref, xs_ref, g2_ref, wu_ref, wd_ref, o_ref, os_ref, h_scr, hs_scr):
    i = pl.program_id(0)
    j = pl.program_id(1)

    def delta(scr):
        u = jnp.square(jnp.maximum(_dot(scr[...], wu_ref[...].astype(BF16)), 0.0))
        return _dot(u.astype(BF16), wd_ref[...].astype(BF16))

    def first_step(src_ref, acc_ref, scr):
        x = src_ref[...]
        scr[...] = _rms(x, g2_ref[...]).astype(BF16)
        acc_ref[...] = x + delta(scr)

    def later_step(acc_ref, scr):
        acc_ref[...] += delta(scr)

    pl.when(j == 0)(functools.partial(first_step, x_ref, o_ref, h_scr))
    pl.when(j > 0)(functools.partial(later_step, o_ref, h_scr))
    pl.when((i == 0) & (j == 0))(functools.partial(first_step, xs_ref, os_ref, hs_scr))
    pl.when((i == 0) & (j > 0))(functools.partial(later_step, os_ref, hs_scr))


def _ffn(x, xs, g2, wu, wd, *, layer, tm):
    m = x.shape[0]
    ms = xs.shape[0]
    sample_spec = pl.BlockSpec((ms, D_MODEL), lambda i, j: (0, 0))
    return pl.pallas_call(
        _ffn_kernel,
        grid=(m // tm, D_FF // FFN_TF),
        in_specs=[pl.BlockSpec((tm, D_MODEL), lambda i, j: (i, 0)),
                  sample_spec,
                  pl.BlockSpec((None, 1, D_MODEL), lambda i, j: (layer, 0, 0)),
                  pl.BlockSpec((None, D_MODEL, FFN_TF), lambda i, j: (layer, 0, j)),
                  pl.BlockSpec((None, FFN_TF, D_MODEL), lambda i, j: (layer, j, 0))],
        out_specs=(pl.BlockSpec((tm, D_MODEL), lambda i, j: (i, 0)), sample_spec),
        out_shape=(jax.ShapeDtypeStruct((m, D_MODEL), F32), jax.ShapeDtypeStruct((ms, D_MODEL), F32)),
        scratch_shapes=[pltpu.VMEM((tm, D_MODEL), BF16), pltpu.VMEM((ms, D_MODEL), BF16)],
        compiler_params=_params(("arbitrary", "arbitrary")),
        name="ffn",
    )(x, xs, g2, wu, wd)


PROMPT_TM = 1024
MERGE_TM = 256


def kernel(x_prompt, x_sample, state_pool, cache_k, cache_v, norm1_g, w_in, b_gate, pool_w, pool_scale,
           q_norm_g, k_norm_g, rel_bias, w_branch_a, w_branch_b, w_out, norm2_g, w_up, w_down):
    nb, seq, _ = x_prompt.shape
    ns, t, _ = x_sample.shape
    depth = w_in.shape[0]
    past = cache_k.shape[2]
    assert seq % PROMPT_TM == 0 and seq % MERGE_TM == 0 and seq % Q_BLOCK == 0 and seq >= BAND_PAST
    assert t >= POOL_HIST and t % HIST_ROWS == 0 and past + t <= SAMPLE_KEYS_PAD and past <= BAND_PAST
    assert past % t == 0 and past > t

    xp = x_prompt.reshape(nb * seq, D_MODEL)
    xs = x_sample.reshape(ns * t, D_MODEL)
    blocks_per_seq = seq // MERGE_TM
    prompt_hist_spec = pl.BlockSpec(
        (HIST_ROWS, POOL_WIDTH), lambda i: (jnp.maximum(i * (MERGE_TM // HIST_ROWS) - 1, 0), 0))
    sample_hist_spec = pl.BlockSpec((ns * HIST_ROWS, POOL_WIDTH), lambda i: (0, 0))

    g1 = norm1_g.reshape(depth, 1, D_MODEL)
    g2 = norm2_g.reshape(depth, 1, D_MODEL)
    bg = b_gate.reshape(depth, 1, 2 * D_MODEL)
    qg = q_norm_g.reshape(depth, 1, HEAD_DIM)
    kg = k_norm_g.reshape(depth, 1, HEAD_DIM)
    ps = pool_scale.reshape(depth, 1, POOL_WIDTH)
    w_mixed, w_gate = _regroup_in_weights(w_in, 0)
    merge_weights = (pool_w.reshape(depth, POOL_WIDTH, POOL_GROUP), w_branch_a, w_branch_b, w_out)
    hist_s = jnp.pad(state_pool, ((0, 0), (0, 0), (HIST_ROWS - POOL_HIST, 0), (0, 0)))
    hist_s = hist_s.reshape(depth, ns * HIST_ROWS, POOL_WIDTH)

    bias_p = _bias_table(rel_bias, q_rows=Q_BLOCK, keys=KEY_BLOCKS * Q_BLOCK,
                         offset=(KEY_BLOCKS - 1) * Q_BLOCK, visible_fn=_prompt_visible)
    bias_s = _bias_table(rel_bias, q_rows=t, keys=SAMPLE_KEYS_PAD, offset=past,
                         visible_fn=lambda qi, kj: kj < past + t)

    k_upd = v_upd = None
    pool_p, kp_l, vp_l, pool_s = [], [], [], []
    for l in range(depth):
        (p, q, k, v, kf, vf, gates,
         p_s, q_s, kf_s, vf_s, gates_s) = _in_proj(xp, xs, g1, w_mixed, w_gate, bg, qg, kg, layer=l, tm=PROMPT_TM)

        attn, pw, wa, wb, wo, *next_w_in = _attn_prompt(
            q, k, v, bias_p, layer=l, n=nb, s=seq, row_casts=merge_weights,
            w_in=w_in if l + 1 < depth else None)
        pw = pw.reshape(1, len(POOL_WINDOWS), POOL_GROUP, POOL_GROUP)
        if next_w_in:
            w_mixed, w_gate = next_w_in
        pool_p.append(p.reshape(nb, seq, POOL_WIDTH)[:, seq - POOL_HIST:])
        kp_l.append(kf.reshape(nb, seq, ATTN_WIDTH)[:, seq - BAND_PAST:].reshape(nb, BAND_PAST, N_HEADS, HEAD_DIM))
        vp_l.append(vf.reshape(nb, seq, ATTN_WIDTH)[:, seq - BAND_PAST:].reshape(nb, BAND_PAST, N_HEADS, HEAD_DIM))
        merge_p = functools.partial(_merge, xp, p, p, attn, gates, pw, ps, wa, wb, wo, layer=l, tm=MERGE_TM,
                                    nseq=1, t=MERGE_TM, blocks_per_seq=blocks_per_seq, pos0=None,
                                    hist_spec=prompt_hist_spec)
        if l == 0:
            xp, k_upd, v_upd = merge_p(caches=(cache_k, cache_v), new_frames=t)
        else:
            xp = merge_p()

        attn_s, k_upd, v_upd = _attn_sample(q_s, kf_s, vf_s, cache_k, cache_v, bias_s, k_upd, v_upd,
                                            layer=l, n=ns, t=t)
        pool_s.append(p_s.reshape(ns, t, POOL_WIDTH)[:, t - POOL_HIST:])
        xs = _merge(xs, p_s, hist_s[l], attn_s, gates_s, pw, ps, wa, wb, wo, layer=l, tm=ns * t, nseq=ns, t=t,
                    blocks_per_seq=None, pos0=PAST_LEN, hist_spec=sample_hist_spec)

        xp, xs = _ffn(xp, xs, g2, w_up, w_down, layer=l, tm=PROMPT_TM)

    return (xp.reshape(nb, seq, D_MODEL), xs.reshape(ns, t, D_MODEL),
            jnp.stack(pool_p), jnp.stack(kp_l), jnp.stack(vp_l),
            jnp.stack(pool_s), k_upd.reshape(cache_k.shape), v_upd.reshape(cache_v.shape))
```

```python
import functools

import numpy as np

import jax
import jax.numpy as jnp
from jax import lax
from jax.experimental import pallas as pl
from jax.experimental.pallas import tpu as pltpu

D_MODEL = 2048
PAST_LEN = 1024
CHUNK = 64
N_PREV_CHUNKS = 8
BAND_PAST = N_PREV_CHUNKS * CHUNK
POOL_WIDTH = D_MODEL // 2
POOL_WINDOWS = (2, 4, 8, 16)
POOL_GROUP = POOL_WIDTH // len(POOL_WINDOWS)
POOL_HIST = max(POOL_WINDOWS) - 1
HEAD_DIM = 128
ATTN_WIDTH = D_MODEL // 2
N_HEADS = ATTN_WIDTH // HEAD_DIM
MAX_REL = 128
D_FF = 4 * D_MODEL
EPS = 1e-6
ATTN_SCALE = HEAD_DIM ** -0.5
LOG2E = 1.4426950408889634
Q_SCALE = ATTN_SCALE * LOG2E
NEG_INF = -1e30

BF16 = jnp.bfloat16
F32 = jnp.float32

V7X_VMEM_BYTES = 64 * 1024 * 1024
LANES = 128
BF16_SUBLANES = 16
VMEM_LIMIT_BYTES = V7X_VMEM_BYTES - 1024 * 1024

HIST_ROWS = 16
Q_BLOCK = 4 * CHUNK
KEY_BLOCKS = 3
SAMPLE_KEYS_PAD = 640


def _rms(xf, g):
    ms = jnp.mean(xf * xf, axis=-1, keepdims=True)
    return xf * lax.rsqrt(ms + EPS) * g


def _dot(a, b):
    return jnp.dot(a, b, preferred_element_type=F32)


def _dot_t(a, b):
    return lax.dot_general(a, b, (((1,), (1,)), ((), ())), preferred_element_type=F32)


def _params(semantics):
    return pltpu.CompilerParams(dimension_semantics=semantics, vmem_limit_bytes=VMEM_LIMIT_BYTES)


def _layer_spec(shape, layer):
    nd = len(shape)
    return pl.BlockSpec((None,) + tuple(shape), lambda *_: (layer,) + (0,) * nd,
                        pipeline_mode=pl.Buffered(1))


def _bias_table_kernel(ext_ref, o_ref, *, q_rows, keys, visible_fn):
    width = ext_ref.shape[-1]
    qi = lax.broadcasted_iota(jnp.int32, (q_rows, keys), 0)
    kj = lax.broadcasted_iota(jnp.int32, (q_rows, keys), 1)
    visible = visible_fn(qi, kj)
    for g in range(ext_ref.shape[0]):
        row = jnp.broadcast_to(ext_ref[g:g + 1, :], (q_rows, width))
        skew = pltpu.roll(row, 0, 1, stride=1, stride_axis=0)
        o_ref[g] = jnp.where(visible, skew[:, :keys] * LOG2E, NEG_INF)


def _take_static(table, idx):
    pieces = []
    start = 0
    while start < len(idx):
        end = start + 1
        step = int(idx[end] - idx[start]) if end < len(idx) else 0
        if step in (-1, 0, 1):
            while end < len(idx) and int(idx[end] - idx[end - 1]) == step:
                end += 1
        lo, hi = sorted((int(idx[start]), int(idx[end - 1])))
        if step == 0 or end - start == 1:
            pieces.append(jnp.broadcast_to(table[:, lo:lo + 1], (table.shape[0], end - start)))
        else:
            ramp = table[:, lo:hi + 1]
            pieces.append(ramp if step == 1 else ramp[:, ::-1])
        start = end
    return jnp.concatenate(pieces, axis=1)


def _bias_table(rel_bias, *, q_rows, keys, offset, visible_fn):
    groups = rel_bias.shape[0] * rel_bias.shape[1]
    width = -(-(keys + q_rows - 1) // LANES) * LANES
    idx = np.zeros((width,), np.int32)
    idx[:keys] = np.clip(offset - np.arange(keys), -MAX_REL, MAX_REL) + MAX_REL
    back = np.arange(1, q_rows)
    idx[width - back] = np.clip(offset + back, -MAX_REL, MAX_REL) + MAX_REL
    ext = _take_static(rel_bias.reshape(groups, 2 * MAX_REL + 1).astype(F32), idx)
    return pl.pallas_call(
        functools.partial(_bias_table_kernel, q_rows=q_rows, keys=keys, visible_fn=visible_fn),
        out_shape=jax.ShapeDtypeStruct((groups, q_rows, keys), F32),
        compiler_params=pltpu.CompilerParams(vmem_limit_bytes=VMEM_LIMIT_BYTES),
        name="bias_table",
    )(ext)


def _prompt_visible(qi, kj):
    own = qi // CHUNK + (KEY_BLOCKS - 1) * (Q_BLOCK // CHUNK)
    kc = kj // CHUNK
    return (kc <= own) & (kc >= own - N_PREV_CHUNKS)


IN_HEADS = 2
IN_STEPS = N_HEADS // IN_HEADS
IN_SEG_TN = IN_HEADS * HEAD_DIM
IN_GATE_TN = 2 * D_MODEL // IN_STEPS


IN_SEG_ORDER = (1, 2, 0, 3)
IN_SLABS = 4 * ATTN_WIDTH // HEAD_DIM


def _regroup_kernel(wq_ref, wk_ref, wp_ref, wv_ref, wg_ref, mixed_ref, gate_ref):
    for s, src in enumerate((wq_ref, wk_ref, wp_ref, wv_ref)):
        mixed_ref[:, s * IN_SEG_TN:(s + 1) * IN_SEG_TN] = src[...].astype(BF16)
    gate_ref[...] = wg_ref[...].astype(BF16)


def _regroup_in_weights(w_in, layer):
    def head_cols(seg):
        return pl.BlockSpec((None, D_MODEL, IN_SEG_TN), lambda j: (layer, 0, seg * IN_STEPS + j))

    gate0 = 4 * ATTN_WIDTH // IN_GATE_TN
    return pl.pallas_call(
        _regroup_kernel,
        grid=(IN_STEPS,),
        in_specs=[head_cols(seg) for seg in IN_SEG_ORDER]
                 + [pl.BlockSpec((None, D_MODEL, IN_GATE_TN), lambda j: (layer, 0, gate0 + j))],
        out_specs=(pl.BlockSpec((None, D_MODEL, 4 * IN_SEG_TN), lambda j: (0, 0, j)),
                   pl.BlockSpec((None, D_MODEL, IN_GATE_TN), lambda j: (0, 0, j))),
        out_shape=(jax.ShapeDtypeStruct((1, D_MODEL, 4 * ATTN_WIDTH), BF16),
                   jax.ShapeDtypeStruct((1, D_MODEL, 2 * D_MODEL), BF16)),
        compiler_params=_params(("parallel",)),
        name="regroup_w_in",
    )(w_in, w_in, w_in, w_in, w_in)


def _regroup_slab_specs(layer, step_fn):
    def mixed_src(*ids):
        s = step_fn(*ids)
        group, within = s // (4 * IN_HEADS), s % (4 * IN_HEADS)
        seg, head = within // IN_HEADS, group * IN_HEADS + within % IN_HEADS
        src_seg = sum(jnp.where(seg == pos, src, 0) for pos, src in enumerate(IN_SEG_ORDER))
        return (layer, 0, src_seg * N_HEADS + head)

    slab = (None, D_MODEL, HEAD_DIM)
    in_specs = [pl.BlockSpec(slab, mixed_src),
                pl.BlockSpec(slab, lambda *ids: (layer, 0, IN_SLABS + step_fn(*ids)))]
    out_specs = [pl.BlockSpec(slab, lambda *ids: (0, 0, step_fn(*ids)))] * 2
    return in_specs, out_specs


def _in_rows(h, wm_ref, wg_ref, bg_ref, qg_ref, kg_ref, p_ref, q_ref, kf_ref, vf_ref, gate_ref,
             k_ref=None, v_ref=None):
    gate_ref[...] = jax.nn.sigmoid(_dot(h, wg_ref[...]) + bg_ref[...])
    z = _dot(h, wm_ref[...])
    for hd in range(IN_HEADS):
        sl = slice(hd * HEAD_DIM, (hd + 1) * HEAD_DIM)
        q_ref[:, sl] = (_rms(z[:, sl], qg_ref[...]) * Q_SCALE).astype(BF16)
        kn = _rms(z[:, IN_SEG_TN + hd * HEAD_DIM:IN_SEG_TN + (hd + 1) * HEAD_DIM], kg_ref[...])
        kf_ref[:, sl] = kn
        if k_ref is not None:
            k_ref[:, sl] = kn.astype(BF16)
    p_ref[...] = z[:, 2 * IN_SEG_TN:3 * IN_SEG_TN]
    v = z[:, 3 * IN_SEG_TN:4 * IN_SEG_TN]
    vf_ref[...] = v
    if v_ref is not None:
        v_ref[...] = v.astype(BF16)


def _in_kernel(x_ref, xs_ref, g1_ref, wm_ref, wg_ref, bg_ref, qg_ref, kg_ref,
               p_ref, q_ref, k_ref, v_ref, kf_ref, vf_ref, gate_ref,
               ps_ref, qs_ref, kfs_ref, vfs_ref, gates_ref, h_scr, hs_scr):
    i = pl.program_id(0)
    j = pl.program_id(1)
    weights = (wm_ref, wg_ref, bg_ref, qg_ref, kg_ref)

    def first_step(src_ref, scr, *outs):
        scr[...] = _rms(src_ref[...], g1_ref[...]).astype(BF16)
        _in_rows(scr[...], *weights, *outs)

    def later_step(scr, *outs):
        _in_rows(scr[...], *weights, *outs)

    prompt_outs = (p_ref, q_ref, kf_ref, vf_ref, gate_ref, k_ref, v_ref)
    sample_outs = (ps_ref, qs_ref, kfs_ref, vfs_ref, gates_ref)
    pl.when(j == 0)(functools.partial(first_step, x_ref, h_scr, *prompt_outs))
    pl.when(j > 0)(functools.partial(later_step, h_scr, *prompt_outs))
    pl.when((i == 0) & (j == 0))(functools.partial(first_step, xs_ref, hs_scr, *sample_outs))
    pl.when((i == 0) & (j > 0))(functools.partial(later_step, hs_scr, *sample_outs))


def _in_proj(x, xs, g1, w_mixed, w_gate, b_gate, qg, kg, *, layer, tm):
    m = x.shape[0]
    ms = xs.shape[0]
    head_spec = pl.BlockSpec((tm, IN_SEG_TN), lambda i, j: (i, j))

    def sample_spec(width):
        return pl.BlockSpec((ms, width), lambda i, j: (0, jnp.where(i == 0, j, IN_STEPS - 1)))

    out_shape = (
        jax.ShapeDtypeStruct((m, POOL_WIDTH), F32),
        jax.ShapeDtypeStruct((m, ATTN_WIDTH), BF16),
        jax.ShapeDtypeStruct((m, ATTN_WIDTH), BF16),
        jax.ShapeDtypeStruct((m, ATTN_WIDTH), BF16),
        jax.ShapeDtypeStruct((m, ATTN_WIDTH), F32),
        jax.ShapeDtypeStruct((m, ATTN_WIDTH), F32),
        jax.ShapeDtypeStruct((m, 2 * D_MODEL), F32),
        jax.ShapeDtypeStruct((ms, POOL_WIDTH), F32),
        jax.ShapeDtypeStruct((ms, ATTN_WIDTH), BF16),
        jax.ShapeDtypeStruct((ms, ATTN_WIDTH), F32),
        jax.ShapeDtypeStruct((ms, ATTN_WIDTH), F32),
        jax.ShapeDtypeStruct((ms, 2 * D_MODEL), F32),
    )
    return pl.pallas_call(
        _in_kernel,
        grid=(m // tm, IN_STEPS),
        in_specs=[
            pl.BlockSpec((tm, D_MODEL), lambda i, j: (i, 0)),
            pl.BlockSpec((ms, D_MODEL), lambda i, j: (0, 0)),
            pl.BlockSpec((None, 1, D_MODEL), lambda i, j: (layer, 0, 0)),
            pl.BlockSpec((None, D_MODEL, 4 * IN_SEG_TN), lambda i, j: (0, 0, j)),
            pl.BlockSpec((None, D_MODEL, IN_GATE_TN), lambda i, j: (0, 0, j)),
            pl.BlockSpec((None, 1, IN_GATE_TN), lambda i, j: (layer, 0, j)),
            pl.BlockSpec((None, 1, HEAD_DIM), lambda i, j: (layer, 0, 0)),
            pl.BlockSpec((None, 1, HEAD_DIM), lambda i, j: (layer, 0, 0)),
        ],
        out_specs=(head_spec,) * 6 + (pl.BlockSpec((tm, IN_GATE_TN), lambda i, j: (i, j)),)
                  + (sample_spec(IN_SEG_TN),) * 4 + (sample_spec(IN_GATE_TN),),
        out_shape=out_shape,
        scratch_shapes=[pltpu.VMEM((tm, D_MODEL), BF16), pltpu.VMEM((ms, D_MODEL), BF16)],
        compiler_params=_params(("arbitrary", "arbitrary")),
        name="in_proj",
    )(x, xs, g1, w_mixed, w_gate, b_gate, qg, kg)


def _attn_prompt_kernel(q_ref, k_new_ref, v_new_ref, bias_ref, *rest):
    i = pl.program_id(1)
    k_ref, v_ref = rest[-2:]
    n_casts = (len(rest) - 3) // 2
    o_ref = rest[n_casts]
    for src, dst in zip(rest[:n_casts], rest[n_casts + 1:-2]):
        dst[...] = src[...].astype(BF16)

    def heads(n_valid):
        first = KEY_BLOCKS - n_valid
        for win, new in ((k_ref, k_new_ref), (v_ref, v_new_ref)):
            for slot in range(first, KEY_BLOCKS - 1):
                win[slot * Q_BLOCK:(slot + 1) * Q_BLOCK, :] = win[(slot + 1) * Q_BLOCK:(slot + 2) * Q_BLOCK, :]
            win[(KEY_BLOCKS - 1) * Q_BLOCK:, :] = new[...]
        for h in range(N_HEADS):
            sl = slice(h * HEAD_DIM, (h + 1) * HEAD_DIM)
            qh = q_ref[:, sl]
            m = l = o = None
            for d in range(first, KEY_BLOCKS):
                rows = slice(d * Q_BLOCK, (d + 1) * Q_BLOCK)
                s = _dot_t(qh, k_ref[rows, sl]) + bias_ref[h, :, rows]
                m_r = jnp.max(s, axis=-1, keepdims=True)
                if m is None:
                    m = m_r
                    e = jnp.exp2(s - m)
                    l = jnp.sum(e, axis=-1, keepdims=True)
                    o = _dot(e.astype(BF16), v_ref[rows, sl])
                else:
                    m_new = jnp.maximum(m, m_r)
                    alpha = jnp.exp2(m - m_new)
                    e = jnp.exp2(s - m_new)
                    l = alpha * l + jnp.sum(e, axis=-1, keepdims=True)
                    o = alpha * o + _dot(e.astype(BF16), v_ref[rows, sl])
                    m = m_new
            o_ref[:, sl] = (o / l).astype(BF16)

    for n_valid in range(1, KEY_BLOCKS):
        pl.when(i == n_valid - 1)(functools.partial(heads, n_valid))
    pl.when(i >= KEY_BLOCKS - 1)(functools.partial(heads, KEY_BLOCKS))


def _attn_prompt(q, k, v, bias, *, layer, n, s, row_casts=(), w_in=None):
    nb = s // Q_BLOCK
    steps = n * nb
    q_spec = pl.BlockSpec((Q_BLOCK, ATTN_WIDTH), lambda b, i: (b * nb + i, 0))
    operands = [q, k, v, bias]
    extra_in, extra_out, extra_shape = [], [], []
    for w in row_casts:
        _, rows, cols = w.shape
        block = (None, rows // steps, cols)
        assert rows % (steps * BF16_SUBLANES) == 0
        extra_in.append(pl.BlockSpec(block, lambda b, i: (layer, b * nb + i, 0)))
        extra_out.append(pl.BlockSpec(block, lambda b, i: (0, b * nb + i, 0)))
        extra_shape.append(jax.ShapeDtypeStruct((1, rows, cols), BF16))
        operands.append(w)
    if w_in is not None:
        assert steps == IN_SLABS
        slab_in, slab_out = _regroup_slab_specs(layer + 1, lambda b, i: b * nb + i)
        extra_in += slab_in
        extra_out += slab_out
        extra_shape += [jax.ShapeDtypeStruct((1, D_MODEL, 4 * ATTN_WIDTH), BF16),
                        jax.ShapeDtypeStruct((1, D_MODEL, 2 * D_MODEL), BF16)]
        operands += [w_in, w_in]

    bias_spec = pl.BlockSpec((N_HEADS, Q_BLOCK, KEY_BLOCKS * Q_BLOCK), lambda b, i: (layer, 0, 0),
                             pipeline_mode=pl.Buffered(1))
    window = pltpu.VMEM((KEY_BLOCKS * Q_BLOCK, ATTN_WIDTH), BF16)
    outs = pl.pallas_call(
        _attn_prompt_kernel,
        grid=(n, nb),
        in_specs=[q_spec, q_spec, q_spec, bias_spec] + extra_in,
        out_specs=tuple([q_spec] + extra_out),
        out_shape=tuple([jax.ShapeDtypeStruct((n * s, ATTN_WIDTH), BF16)] + extra_shape),
        scratch_shapes=[window, window],
        compiler_params=_params(("arbitrary", "arbitrary")),
        name="attn_prompt",
    )(*operands)
    return outs


def _shift_cache_block(src_ref, dst_ref, *, new_rows):
    keep = src_ref.shape[0] - new_rows
    dst_ref[0:keep, :] = src_ref[new_rows:, :]
    dst_ref[keep:, :] = jnp.zeros((new_rows, HEAD_DIM), F32)


def _attn_sample_kernel(q_ref, kf_ref, vf_ref, ck_ref, cv_ref, bias_ref, k_alias, v_alias,
                        o_ref, ko_ref, vo_ref, kk_scr, vv_scr, *, t, past):
    del k_alias, v_alias
    for new_ref, cache_ref, out_ref, scr in ((kf_ref, ck_ref, ko_ref, kk_scr),
                                             (vf_ref, cv_ref, vo_ref, vv_scr)):
        for h in range(N_HEADS):
            sl = slice(h * HEAD_DIM, (h + 1) * HEAD_DIM)
            out_ref[pl.ds(h, t, stride=N_HEADS), :] = new_ref[:, sl]
            scr[0:past, sl] = cache_ref[pl.ds(h, past, stride=N_HEADS), :].astype(BF16)
        scr[past:past + t, :] = new_ref[...].astype(BF16)
        scr[past + t:, :] = jnp.zeros((SAMPLE_KEYS_PAD - past - t, ATTN_WIDTH), BF16)
    for h in range(N_HEADS):
        sl = slice(h * HEAD_DIM, (h + 1) * HEAD_DIM)
        s = _dot_t(q_ref[:, sl], kk_scr[:, sl]) + bias_ref[h]
        m = jnp.max(s, axis=-1, keepdims=True)
        e = jnp.exp2(s - m)
        l = jnp.sum(e, axis=-1, keepdims=True)
        o = _dot(e.astype(BF16), vv_scr[:, sl])
        o_ref[:, sl] = (o / l).astype(BF16)


def _attn_sample(q, kf, vf, cache_k, cache_v, bias, k_upd, v_upd, *, layer, n, t):
    depth, _, past = cache_k.shape[:3]
    rows = past * N_HEADS
    new_rows = t * N_HEADS
    row_spec = pl.BlockSpec((t, ATTN_WIDTH), lambda b: (b, 0))
    cache_spec = pl.BlockSpec((None, None, rows, HEAD_DIM), lambda b: (layer, b, 0, 0))
    new_spec = pl.BlockSpec((None, None, new_rows, HEAD_DIM), lambda b: (layer, b, rows // new_rows - 1, 0))
    bias_spec = pl.BlockSpec((N_HEADS, t, SAMPLE_KEYS_PAD), lambda b: (layer, 0, 0),
                             pipeline_mode=pl.Buffered(1))
    any_spec = pl.BlockSpec(memory_space=pl.ANY)
    upd = jax.ShapeDtypeStruct(k_upd.shape, F32)
    return pl.pallas_call(
        functools.partial(_attn_sample_kernel, t=t, past=past),
        grid=(n,),
        in_specs=[row_spec, row_spec, row_spec, cache_spec, cache_spec, bias_spec, any_spec, any_spec],
        out_specs=(row_spec, new_spec, new_spec),
        out_shape=(jax.ShapeDtypeStruct((n * t, ATTN_WIDTH), BF16), upd, upd),
        input_output_aliases={6: 1, 7: 2},
        scratch_shapes=[pltpu.VMEM((SAMPLE_KEYS_PAD, ATTN_WIDTH), BF16),
                        pltpu.VMEM((SAMPLE_KEYS_PAD, ATTN_WIDTH), BF16)],
        compiler_params=_params(("arbitrary",)),
        name="attn_sample",
    )(q, kf, vf, cache_k.reshape(depth, n, rows, HEAD_DIM), cache_v.reshape(depth, n, rows, HEAD_DIM),
      bias, k_upd, v_upd)


def _merge_kernel(*refs, nseq, t, blocks_per_seq, pos0, cache_blocks, new_rows):
    x_ref, p_ref, hist_ref, a_ref, gate_ref, pw_ref, ps_ref, wa_ref, wb_ref, wo_ref = refs[:10]
    i = pl.program_id(0)
    if cache_blocks:
        ck_ref, cv_ref, o_ref, ko_ref, vo_ref, xp_scr, d_scr = refs[10:]
        pl.when(i < cache_blocks)(
            functools.partial(_shift_cache_block, ck_ref, ko_ref, new_rows=new_rows))
        pl.when((i >= cache_blocks) & (i < 2 * cache_blocks))(
            functools.partial(_shift_cache_block, cv_ref, vo_ref, new_rows=new_rows))
    else:
        o_ref, xp_scr, d_scr = refs[10:]
    row = lax.broadcasted_iota(jnp.int32, (t, 1), 0)
    if blocks_per_seq is None:
        pos = pos0 + row
    else:
        pos = (i % blocks_per_seq) * t + row
    gated_b = gate_ref[:, D_MODEL:] * _dot(a_ref[...], wb_ref[...])

    for s in range(nseq):
        rows = slice(s * t, (s + 1) * t)
        hist = hist_ref[s * HIST_ROWS:(s + 1) * HIST_ROWS, :]
        if blocks_per_seq is not None:
            hist = jnp.where(i % blocks_per_seq == 0, 0.0, hist)
        xp_scr[0:HIST_ROWS, :] = hist
        xp_scr[HIST_ROWS:, :] = p_ref[rows, :]
        for gi, w in enumerate(POOL_WINDOWS):
            cols = slice(gi * POOL_GROUP, (gi + 1) * POOL_GROUP)
            cur = xp_scr[:, cols]
            shift = 1
            while shift < w:
                cur = cur + pltpu.roll(cur, shift, 0)
                shift *= 2
            acc = cur[HIST_ROWS:, :]
            inv_cnt = 1.0 / jnp.minimum(pos + 1, w).astype(F32)
            d_scr[rows, cols] = (acc * inv_cnt - p_ref[rows, cols]).astype(BF16)
    pooled = jnp.concatenate(
        [_dot(d_scr[:, gi * POOL_GROUP:(gi + 1) * POOL_GROUP], pw_ref[gi]) for gi in range(len(POOL_WINDOWS))],
        axis=-1) * ps_ref[...]
    branch_a = _dot(pooled.astype(BF16), wa_ref[...])
    merged = gate_ref[:, :D_MODEL] * branch_a + gated_b
    o_ref[...] = x_ref[...] + _dot(merged.astype(BF16), wo_ref[...])


def _merge(x, p, hist, attn, gates, pool_w, pool_scale, wa, wb, wo, *, layer, tm, nseq, t, blocks_per_seq, pos0,
           hist_spec, caches=None, new_frames=0):
    m = x.shape[0]
    steps = m // tm

    def row_spec(width):
        return pl.BlockSpec((tm, width), lambda i: (i, 0))

    operands = [x, p, hist, attn, gates, pool_w, pool_scale, wa, wb, wo]
    in_specs = [row_spec(D_MODEL), row_spec(POOL_WIDTH), hist_spec, row_spec(ATTN_WIDTH),
                row_spec(2 * D_MODEL),
                _layer_spec(pool_w.shape[1:], 0), _layer_spec(pool_scale.shape[1:], layer),
                _layer_spec(wa.shape[1:], 0), _layer_spec(wb.shape[1:], 0),
                _layer_spec(wo.shape[1:], 0)]
    out_specs = [row_spec(D_MODEL)]
    out_shape = [jax.ShapeDtypeStruct((m, D_MODEL), F32)]
    cache_blocks = 0
    if caches is not None:
        depth, n, past = caches[0].shape[:3]
        cache_blocks = depth * n
        assert steps >= 2 * cache_blocks
        rows = past * N_HEADS
        k_spec = pl.BlockSpec((None, rows, HEAD_DIM), lambda i: (jnp.minimum(i, cache_blocks - 1), 0, 0))
        v_spec = pl.BlockSpec((None, rows, HEAD_DIM),
                              lambda i: (jnp.clip(i - cache_blocks, 0, cache_blocks - 1), 0, 0))
        operands += [c.reshape(cache_blocks, rows, HEAD_DIM) for c in caches]
        in_specs += [k_spec, v_spec]
        out_specs += [k_spec, v_spec]
        out_shape += [jax.ShapeDtypeStruct((cache_blocks, rows, HEAD_DIM), F32)] * 2
    outs = pl.pallas_call(
        functools.partial(_merge_kernel, nseq=nseq, t=t, blocks_per_seq=blocks_per_seq, pos0=pos0,
                          cache_blocks=cache_blocks, new_rows=new_frames * N_HEADS),
        grid=(steps,),
        in_specs=in_specs,
        out_specs=tuple(out_specs),
        out_shape=tuple(out_shape),
        scratch_shapes=[pltpu.VMEM((HIST_ROWS + t, POOL_WIDTH), F32),
                        pltpu.VMEM((tm, POOL_WIDTH), BF16)],
        compiler_params=_params(("arbitrary",)),
        name="merge",
    )(*operands)
    if caches is None:
        return outs[0]
    return outs[0], outs[1].reshape(depth, n, rows, HEAD_DIM), outs[2].reshape(depth, n, rows, HEAD_DIM)


FFN_TF = 512


def _ffn_kernel(x_ref, xs_ref, g2_ref, wu_ref, wd_ref, o_ref, os_ref, h_scr):
    i = pl.program_id(0)
    j = pl.program_id(1)
    tm = x_ref.shape[0]

    def step(first, with_sample):
        if first:
            x = x_ref[...]
            h_scr[0:tm, :] = _rms(x, g2_ref[...]).astype(BF16)
            if with_sample:
                xs = xs_ref[...]
                h_scr[tm:, :] = _rms(xs, g2_ref[...]).astype(BF16)
        h = h_scr[...] if with_sample else h_scr[0:tm, :]
        u = jnp.square(jnp.maximum(_dot(h, wu_ref[...].astype(BF16)), 0.0))
        d = _dot(u.astype(BF16), wd_ref[...].astype(BF16))
        if first:
            o_ref[...] = x + d[0:tm, :]
            if with_sample:
                os_ref[...] = xs + d[tm:, :]
        else:
            o_ref[...] += d[0:tm, :]
            if with_sample:
                os_ref[...] += d[tm:, :]

    pl.when((i == 0) & (j == 0))(functools.partial(step, True, True))
    pl.when((i == 0) & (j > 0))(functools.partial(step, False, True))
    pl.when((i > 0) & (j == 0))(functools.partial(step, True, False))
    pl.when((i > 0) & (j > 0))(functools.partial(step, False, False))


def _ffn(x, xs, g2, wu, wd, *, layer, tm):
    m = x.shape[0]
    ms = xs.shape[0]
    sample_spec = pl.BlockSpec((ms, D_MODEL), lambda i, j: (0, 0))
    return pl.pallas_call(
        _ffn_kernel,
        grid=(m // tm, D_FF // FFN_TF),
        in_specs=[pl.BlockSpec((tm, D_MODEL), lambda i, j: (i, 0)),
                  sample_spec,
                  pl.BlockSpec((None, 1, D_MODEL), lambda i, j: (layer, 0, 0)),
                  pl.BlockSpec((None, D_MODEL, FFN_TF), lambda i, j: (layer, 0, j)),
                  pl.BlockSpec((None, FFN_TF, D_MODEL), lambda i, j: (layer, j, 0))],
        out_specs=(pl.BlockSpec((tm, D_MODEL), lambda i, j: (i, 0)), sample_spec),
        out_shape=(jax.ShapeDtypeStruct((m, D_MODEL), F32), jax.ShapeDtypeStruct((ms, D_MODEL), F32)),
        scratch_shapes=[pltpu.VMEM((tm + ms, D_MODEL), BF16)],
        compiler_params=_params(("arbitrary", "arbitrary")),
        name="ffn",
    )(x, xs, g2, wu, wd)


PROMPT_TM = 1024
MERGE_TM = 256


def kernel(x_prompt, x_sample, state_pool, cache_k, cache_v, norm1_g, w_in, b_gate, pool_w, pool_scale,
           q_norm_g, k_norm_g, rel_bias, w_branch_a, w_branch_b, w_out, norm2_g, w_up, w_down):
    nb, seq, _ = x_prompt.shape
    ns, t, _ = x_sample.shape
    depth = w_in.shape[0]
    past = cache_k.shape[2]
    assert seq % PROMPT_TM == 0 and seq % MERGE_TM == 0 and seq % Q_BLOCK == 0 and seq >= BAND_PAST
    assert t >= POOL_HIST and t % HIST_ROWS == 0 and past + t <= SAMPLE_KEYS_PAD and past <= BAND_PAST
    assert past % t == 0 and past > t

    xp = x_prompt.reshape(nb * seq, D_MODEL)
    xs = x_sample.reshape(ns * t, D_MODEL)
    blocks_per_seq = seq // MERGE_TM
    prompt_hist_spec = pl.BlockSpec(
        (HIST_ROWS, POOL_WIDTH), lambda i: (jnp.maximum(i * (MERGE_TM // HIST_ROWS) - 1, 0), 0))
    sample_hist_spec = pl.BlockSpec((ns * HIST_ROWS, POOL_WIDTH), lambda i: (0, 0))

    g1 = norm1_g.reshape(depth, 1, D_MODEL)
    g2 = norm2_g.reshape(depth, 1, D_MODEL)
    bg = b_gate.reshape(depth, 1, 2 * D_MODEL)
    qg = q_norm_g.reshape(depth, 1, HEAD_DIM)
    kg = k_norm_g.reshape(depth, 1, HEAD_DIM)
    ps = pool_scale.reshape(depth, 1, POOL_WIDTH)
    w_mixed, w_gate = _regroup_in_weights(w_in, 0)
    merge_weights = (pool_w.reshape(depth, POOL_WIDTH, POOL_GROUP), w_branch_a, w_branch_b, w_out)
    hist_s = jnp.pad(state_pool, ((0, 0), (0, 0), (HIST_ROWS - POOL_HIST, 0), (0, 0)))
    hist_s = hist_s.reshape(depth, ns * HIST_ROWS, POOL_WIDTH)

    bias_p = _bias_table(rel_bias, q_rows=Q_BLOCK, keys=KEY_BLOCKS * Q_BLOCK,
                         offset=(KEY_BLOCKS - 1) * Q_BLOCK, visible_fn=_prompt_visible)
    bias_s = _bias_table(rel_bias, q_rows=t, keys=SAMPLE_KEYS_PAD, offset=past,
                         visible_fn=lambda qi, kj: kj < past + t)

    k_upd = v_upd = None
    pool_p, kp_l, vp_l, pool_s = [], [], [], []
    for l in range(depth):
        (p, q, k, v, kf, vf, gates,
         p_s, q_s, kf_s, vf_s, gates_s) = _in_proj(xp, xs, g1, w_mixed, w_gate, bg, qg, kg, layer=l, tm=PROMPT_TM)

        attn, pw, wa, wb, wo, *next_w_in = _attn_prompt(
            q, k, v, bias_p, layer=l, n=nb, s=seq, row_casts=merge_weights,
            w_in=w_in if l + 1 < depth else None)
        pw = pw.reshape(1, len(POOL_WINDOWS), POOL_GROUP, POOL_GROUP)
        if next_w_in:
            w_mixed, w_gate = next_w_in
        pool_p.append(p.reshape(nb, seq, POOL_WIDTH)[:, seq - POOL_HIST:])
        kp_l.append(kf.reshape(nb, seq, ATTN_WIDTH)[:, seq - BAND_PAST:].reshape(nb, BAND_PAST, N_HEADS, HEAD_DIM))
        vp_l.append(vf.reshape(nb, seq, ATTN_WIDTH)[:, seq - BAND_PAST:].reshape(nb, BAND_PAST, N_HEADS, HEAD_DIM))
        merge_p = functools.partial(_merge, xp, p, p, attn, gates, pw, ps, wa, wb, wo, layer=l, tm=MERGE_TM,
                                    nseq=1, t=MERGE_TM, blocks_per_seq=blocks_per_seq, pos0=None,
                                    hist_spec=prompt_hist_spec)
        if l == 0:
            xp, k_upd, v_upd = merge_p(caches=(cache_k, cache_v), new_frames=t)
        else:
            xp = merge_p()

        attn_s, k_upd, v_upd = _attn_sample(q_s, kf_s, vf_s, cache_k, cache_v, bias_s, k_upd, v_upd,
                                            layer=l, n=ns, t=t)
        pool_s.append(p_s.reshape(ns, t, POOL_WIDTH)[:, t - POOL_HIST:])
        xs = _merge(xs, p_s, hist_s[l], attn_s, gates_s, pw, ps, wa, wb, wo, layer=l, tm=ns * t, nseq=ns, t=t,
                    blocks_per_seq=None, pos0=PAST_LEN, hist_spec=sample_hist_spec)

        xp, xs = _ffn(xp, xs, g2, w_up, w_down, layer=l, tm=PROMPT_TM)

    return (xp.reshape(nb, seq, D_MODEL), xs.reshape(ns, t, D_MODEL),
            jnp.stack(pool_p), jnp.stack(kp_l), jnp.stack(vp_l),
            jnp.stack(pool_s), k_upd.reshape(cache_k.shape), v_upd.reshape(cache_v.shape))
```
